```python
import math
import jax
import jax.numpy as jnp
from jax import lax
import numpy as np

D_MODEL = 2048
BATCH = 2
SEQ = 16384
DEPTH = 2

GRID_W = 64
CTX_LEN = 256

NA_HEADS = 8
NA_HEAD_DIM = 128
NA_WIDTH = NA_HEADS * NA_HEAD_DIM
NA_KH_MAX = 8
NA_KW = 16

FT_GROUPS = 4
FT_GROUP_DIM = 128
FT_WIDTH = FT_GROUPS * FT_GROUP_DIM

POOL_WINDOWS = (2, 4, 8, 16)
POOL_GROUP_DIM = 128
POOL_WIDTH = len(POOL_WINDOWS) * POOL_GROUP_DIM

SC_WIDTH = 512
SC_CONV = 3

N_BRANCH = 4
IN_WIDTH = 3 * NA_WIDTH + FT_WIDTH + POOL_WIDTH + 3 * SC_WIDTH
D_FF = 256 * math.ceil(8 * D_MODEL / (3 * 256))
N_MOD = 6
RMS_EPS = 1e-6

kernel_name = 'hybrid_natten_fnet_pool_shortconv_dit'


def rms_norm(x, w):
    xf = x.astype(jnp.float32)
    y = xf * lax.rsqrt(jnp.mean(xf * xf, axis=-1, keepdims=True) + RMS_EPS)
    return (y * w.astype(jnp.float32)).astype(x.dtype)


def modulate(h, shift, scale):
    return h * (1 + scale) + shift


def in_split_points():
    widths = (NA_WIDTH, NA_WIDTH, NA_WIDTH, FT_WIDTH, POOL_WIDTH, SC_WIDTH, SC_WIDTH)
    return [int(s) for s in np.cumsum(widths)]


def mixer_inputs(h, w_in, q_norm_w, k_norm_w):
    b, l, _ = h.shape
    q, k, v, ft, pool, sv, sb, sc = jnp.split(h @ w_in, in_split_points(), axis=-1)
    q = rms_norm(q.reshape(b, l, NA_HEADS, NA_HEAD_DIM), q_norm_w)
    k = rms_norm(k.reshape(b, l, NA_HEADS, NA_HEAD_DIM), k_norm_w)
    v = v.reshape(b, l, NA_HEADS, NA_HEAD_DIM)
    return q, k, v, ft, pool, sv, sb, sc


def context_kv(hc, w_in, k_norm_w):
    b, l, _ = hc.shape
    k = (hc @ w_in[:, NA_WIDTH:2 * NA_WIDTH]).reshape(b, l, NA_HEADS, NA_HEAD_DIM)
    v = (hc @ w_in[:, 2 * NA_WIDTH:3 * NA_WIDTH]).reshape(b, l, NA_HEADS, NA_HEAD_DIM)
    return rms_norm(k, k_norm_w), v


def context_attention(q, k, v):
    s = jnp.einsum('blhd,bmhd->bhlm', q, k, preferred_element_type=jnp.float32) * (NA_HEAD_DIM ** -0.5)
    p = jax.nn.softmax(s, axis=-1).astype(v.dtype)
    o = jnp.einsum('bhlm,bmhd->blhd', p, v)
    return o.reshape(q.shape[0], q.shape[1], NA_WIDTH)


def neighbourhood_attention(q, k, v, k_ctx, v_ctx, rpb):
    b, s, h, dh = q.shape
    rows = s // GRID_W
    kh = min(NA_KH_MAX, rows)
    qg = q.reshape(b, rows, GRID_W, h, dh)
    kg = k.reshape(b, rows, GRID_W, h, dh)
    vg = v.reshape(b, rows, GRID_W, h, dh)
    cols = np.arange(GRID_W)
    col_start = np.clip(cols - NA_KW // 2, 0, GRID_W - NA_KW)
    col_idx = col_start[:, None] + np.arange(NA_KW)[None, :]
    col_off = col_idx - cols[:, None] + (NA_KW - 1)
    bias_cols = jnp.transpose(rpb[:, :, col_off], (0, 2, 1, 3))
    scale = dh ** -0.5

    def row_block(r):
        rs = jnp.clip(r - kh // 2, 0, rows - kh)
        k_win = lax.dynamic_slice_in_dim(kg, rs, kh, axis=1)[:, :, col_idx]
        v_win = lax.dynamic_slice_in_dim(vg, rs, kh, axis=1)[:, :, col_idx]
        q_r = lax.dynamic_index_in_dim(qg, r, axis=1, keepdims=False)
        s_loc = jnp.einsum('bjhd,bajwhd->bhjaw', q_r, k_win, preferred_element_type=jnp.float32) * scale
        row_off = rs + jnp.arange(kh) - r + (NA_KH_MAX - 1)
        s_loc = s_loc + bias_cols[:, :, row_off].astype(jnp.float32)[None]
        s_ctx = jnp.einsum('bjhd,blhd->bhjl', q_r, k_ctx, preferred_element_type=jnp.float32) * scale
        s_all = jnp.concatenate([s_loc.reshape(b, h, GRID_W, kh * NA_KW), s_ctx], axis=-1)
        p = jax.nn.softmax(s_all, axis=-1).astype(v.dtype)
        p_loc = p[..., :kh * NA_KW].reshape(b, h, GRID_W, kh, NA_KW)
        p_ctx = p[..., kh * NA_KW:]
        return jnp.einsum('bhjaw,bajwhd->bjhd', p_loc, v_win) + jnp.einsum('bhjl,blhd->bjhd', p_ctx, v_ctx)

    out = lax.map(row_block, jnp.arange(rows))
    return jnp.transpose(out, (1, 0, 2, 3, 4)).reshape(b, s, h * dh)


def fourier_mix(u):
    b, l, _ = u.shape
    ug = u.astype(jnp.float32).reshape(b, l, FT_GROUPS, FT_GROUP_DIM)
    f = jnp.fft.fft2(ug, axes=(1, 3), norm='ortho').real
    return f.reshape(b, l, FT_WIDTH).astype(u.dtype)


def pool_mix(u, w_pool, pool_scale):
    b, l, _ = u.shape
    uf = u.astype(jnp.float32)
    cs = jnp.pad(jnp.cumsum(uf, axis=1), ((0, 0), (1, 0), (0, 0)))
    pos = jnp.arange(l)
    outs = []
    for g, w in enumerate(POOL_WINDOWS):
        lo = jnp.clip(pos - w // 2, 0, l)
        hi = jnp.clip(pos - w // 2 + w, 0, l)
        c0, c1 = g * POOL_GROUP_DIM, (g + 1) * POOL_GROUP_DIM
        win_sum = cs[:, hi, c0:c1] - cs[:, lo, c0:c1]
        cnt = (hi - lo).astype(jnp.float32)[None, :, None]
        outs.append(win_sum / cnt - uf[:, :, c0:c1])
    p = jnp.stack(outs, axis=2)
    y = jnp.einsum('blgc,gcd->blgd', p, w_pool.astype(jnp.float32)).reshape(b, l, POOL_WIDTH)
    return (y * pool_scale.astype(jnp.float32)).astype(u.dtype)


def short_conv_mix(sv, sb, sc, conv_w):
    u = sc * sv
    up = jnp.pad(u, ((0, 0), (1, 1), (0, 0)))
    y = up[:, :-2] * conv_w[0] + up[:, 1:-1] * conv_w[1] + up[:, 2:] * conv_w[2]
    return sb * y


def merge_branches(h, branches, w_gate, w_branch, w_o):
    y = jax.nn.sigmoid(h @ w_gate[0]) * (branches[0] @ w_branch[0])
    for i in range(1, N_BRANCH):
        y = y + jax.nn.sigmoid(h @ w_gate[i]) * (branches[i] @ w_branch[i])
    return y @ w_o


def swiglu(h, w_g, w_u, w_d):
    return (jax.nn.silu(h @ w_g) * (h @ w_u)) @ w_d


def setup_inputs(seed: int = 0) -> dict:
    key = jax.random.key(seed)
    ks = jax.random.split(key, 32)
    f32 = jnp.float32
    L = DEPTH
    D = D_MODEL

    def nrm(k, shape, scale):
        return jax.random.normal(k, shape, f32) * scale

    return {
        'x': nrm(ks[0], (BATCH, SEQ, D), 1.0),
        'c': nrm(ks[1], (BATCH, D), 1.0),
        'ctx': nrm(ks[2], (BATCH, CTX_LEN, D), 1.0),
        'c_ctx': nrm(ks[3], (D,), 1.0),
        'w_mod': nrm(ks[4], (L, D, N_MOD * D), D ** -0.5),
        'b_mod': nrm(ks[5], (L, N_MOD * D), 0.01),
        'norm1_w': 1.0 + nrm(ks[6], (L, D), 0.05),
        'w_in': nrm(ks[7], (L, D, IN_WIDTH), D ** -0.5),
        'q_norm_w': 1.0 + nrm(ks[8], (L, NA_HEAD_DIM), 0.05),
        'k_norm_w': 1.0 + nrm(ks[9], (L, NA_HEAD_DIM), 0.05),
        'rpb': nrm(ks[10], (L, NA_HEADS, 2 * NA_KH_MAX - 1, 2 * NA_KW - 1), 0.5),
        'w_pool': nrm(ks[11], (L, len(POOL_WINDOWS), POOL_GROUP_DIM, POOL_GROUP_DIM), POOL_GROUP_DIM ** -0.5),
        'pool_scale': 1.0 + nrm(ks[12], (L, POOL_WIDTH), 0.05),
        'conv_w': nrm(ks[13], (L, SC_CONV, SC_WIDTH), SC_CONV ** -0.5),
        'w_gate': nrm(ks[14], (L, N_BRANCH, D, D), D ** -0.5),
        'w_pa': nrm(ks[15], (L, NA_WIDTH, D), NA_WIDTH ** -0.5),
        'w_pb': nrm(ks[16], (L, FT_WIDTH, D), FT_WIDTH ** -0.5),
        'w_pc': nrm(ks[17], (L, POOL_WIDTH, D), POOL_WIDTH ** -0.5),
        'w_pd': nrm(ks[18], (L, SC_WIDTH, D), SC_WIDTH ** -0.5),
        'w_o': nrm(ks[19], (L, D, D), D ** -0.5),
        'norm2_w': 1.0 + nrm(ks[20], (L, D), 0.05),
        'w_ffn_gate': nrm(ks[21], (L, D, D_FF), D ** -0.5),
        'w_ffn_up': nrm(ks[22], (L, D, D_FF), D ** -0.5),
        'w_ffn_down': nrm(ks[23], (L, D_FF, D), D_FF ** -0.5),
    }


def reference(x, c, ctx, c_ctx, w_mod, b_mod, norm1_w, w_in, q_norm_w, k_norm_w, rpb, w_pool,
              pool_scale, conv_w, w_gate, w_pa, w_pb, w_pc, w_pd, w_o, norm2_w, w_ffn_gate,
              w_ffn_up, w_ffn_down):
    xc = ctx
    for i in range(DEPTH):
        w_branch = (w_pa[i], w_pb[i], w_pc[i], w_pd[i])
        mod = jnp.split(jax.nn.silu(c) @ w_mod[i] + b_mod[i], N_MOD, axis=-1)
        sh1, sc1, g1, sh2, sc2, g2 = [m[:, None, :] for m in mod]
        csh1, csc1, cg1, csh2, csc2, cg2 = jnp.split(jax.nn.silu(c_ctx) @ w_mod[i] + b_mod[i], N_MOD, axis=-1)

        hc = modulate(rms_norm(xc, norm1_w[i]), csh1, csc1)
        if i == DEPTH - 1:
            kc, vc = context_kv(hc, w_in[i], k_norm_w[i])
        else:
            qc, kc, vc, ftc, poolc, svc, sbc, scc = mixer_inputs(hc, w_in[i], q_norm_w[i], k_norm_w[i])
            branches_c = (context_attention(qc, kc, vc), fourier_mix(ftc),
                          pool_mix(poolc, w_pool[i], pool_scale[i]), short_conv_mix(svc, sbc, scc, conv_w[i]))
            xc = xc + cg1 * merge_branches(hc, branches_c, w_gate[i], w_branch, w_o[i])
            xc = xc + cg2 * swiglu(modulate(rms_norm(xc, norm2_w[i]), csh2, csc2),
                                   w_ffn_gate[i], w_ffn_up[i], w_ffn_down[i])

        h = modulate(rms_norm(x, norm1_w[i]), sh1, sc1)
        q, k, v, ft, pool, sv, sb, sc = mixer_inputs(h, w_in[i], q_norm_w[i], k_norm_w[i])
        branches = (neighbourhood_attention(q, k, v, kc, vc, rpb[i]), fourier_mix(ft),
                    pool_mix(pool, w_pool[i], pool_scale[i]), short_conv_mix(sv, sb, sc, conv_w[i]))
        x = x + g1 * merge_branches(h, branches, w_gate[i], w_branch, w_o[i])
        x = x + g2 * swiglu(modulate(rms_norm(x, norm2_w[i]), sh2, sc2),
                            w_ffn_gate[i], w_ffn_up[i], w_ffn_down[i])
    return x
```

```python
import functools
import math

import numpy as np
import jax
import jax.numpy as jnp
from jax import lax
from jax.experimental import pallas as pl
from jax.experimental.pallas import tpu as pltpu

F32 = jnp.float32
BF16 = jnp.bfloat16

GRID_W = 64
NA_HEADS = 8
HEAD_DIM = 128
NA_WIDTH = NA_HEADS * HEAD_DIM
NA_KH = 8
NA_KW = 16
NA_QROWS = 4
NA_KROWS = 12
GROUP_DIM = 128
FT_GROUPS = 4
POOL_WINDOWS = (2, 4, 8, 16)
MIX_WIDTH = 512
N_MOD = 6
RMS_EPS = 1e-6
HALO = 16
MOD_ROWS = 8
NEG_BIG = -1e30

COL_FT, COL_POOL, COL_SV, COL_SB, COL_SC = 6, 7, 8, 9, 10

VMEM_LIMIT = 56 * 1024 * 1024


def _params(n_axes):
    return pltpu.CompilerParams(dimension_semantics=("arbitrary",) * n_axes,
                                vmem_limit_bytes=VMEM_LIMIT)


def _mod_kernel(c_ref, w_ref, b_ref, o_ref):
    cv = c_ref[...]
    s = cv * jax.nn.sigmoid(cv)
    o_ref[...] = jnp.dot(s.astype(BF16), w_ref[...].astype(BF16),
                         preferred_element_type=F32) + b_ref[...]


def _modulation(cvec, w_mod, b_mod):
    depth, d, n = w_mod.shape
    tn = 1024
    return pl.pallas_call(
        _mod_kernel,
        grid=(depth, n // tn),
        in_specs=[
            pl.BlockSpec((MOD_ROWS, d), lambda l, j: (0, 0)),
            pl.BlockSpec((None, d, tn), lambda l, j: (l, 0, j)),
            pl.BlockSpec((None, 1, tn), lambda l, j: (l, 0, j)),
        ],
        out_specs=pl.BlockSpec((None, MOD_ROWS, tn), lambda l, j: (l, 0, j)),
        out_shape=jax.ShapeDtypeStruct((depth, MOD_ROWS, n), F32),
        compiler_params=_params(2),
        name="modulation",
    )(cvec, w_mod, b_mod.reshape(depth, 1, n))


def _norm_mod_store(h_scr, x_ref, nw_ref, sh_ref, sc_ref, chunk=256):
    tm = x_ref.shape[0]
    nw = nw_ref[...]
    mul = 1.0 + sc_ref[...]
    sh = sh_ref[...]
    for r in range(0, tm, chunk):
        xf = x_ref[r:r + chunk, :]
        ms = jnp.mean(xf * xf, axis=-1, keepdims=True)
        y = xf * lax.rsqrt(ms + RMS_EPS) * nw
        h_scr[r:r + chunk, :] = (y * mul + sh).astype(BF16)


def _row_fn(modrow, tm):
    base, tokens_per_row = modrow
    return lambda i: base + (i * tm) // tokens_per_row


def _mod_spec(d_block, chunk, row_fn):
    return pl.BlockSpec((None, 1, d_block), lambda i, j: (row_fn(i) * N_MOD + chunk, 0, 0))


def _in_proj_kernel(x_ref, sh_ref, sc_ref, nw_ref, w_ref, qw_ref, kw_ref, o_ref, h_scr, *, n_head_blocks):
    j = pl.program_id(1)

    @pl.when(j == 0)
    def _():
        _norm_mod_store(h_scr, x_ref, nw_ref, sh_ref, sc_ref)

    acc = jnp.dot(h_scr[...], w_ref[...], preferred_element_type=F32)

    def head_norm(w):
        for hh in range(acc.shape[1] // HEAD_DIM):
            a = acc[:, hh * HEAD_DIM:(hh + 1) * HEAD_DIM]
            ms = jnp.mean(a * a, axis=-1, keepdims=True)
            o_ref[:, hh * HEAD_DIM:(hh + 1) * HEAD_DIM] = (a * lax.rsqrt(ms + RMS_EPS) * w).astype(o_ref.dtype)

    @pl.when(j < n_head_blocks)
    def _():
        head_norm(qw_ref[...])

    @pl.when((j >= n_head_blocks) & (j < 2 * n_head_blocks))
    def _():
        head_norm(kw_ref[...])

    @pl.when(j >= 2 * n_head_blocks)
    def _():
        o_ref[...] = acc.astype(o_ref.dtype)


def _in_proj(x2, modt, modrow, nw, w_in, qw, kw, tm):
    t, d = x2.shape
    row_fn = _row_fn(modrow, tm)
    n = w_in.shape[1]
    tn = MIX_WIDTH
    return pl.pallas_call(
        functools.partial(_in_proj_kernel, n_head_blocks=NA_WIDTH // tn),
        grid=(t // tm, n // tn),
        in_specs=[
            pl.BlockSpec((tm, d), lambda i, j: (i, 0)),
            _mod_spec(d, 0, row_fn),
            _mod_spec(d, 1, row_fn),
            pl.BlockSpec((1, d), lambda i, j: (0, 0)),
            pl.BlockSpec((d, tn), lambda i, j: (0, j)),
            pl.BlockSpec((1, HEAD_DIM), lambda i, j: (0, 0)),
            pl.BlockSpec((1, HEAD_DIM), lambda i, j: (0, 0)),
        ],
        out_specs=pl.BlockSpec((tm, tn), lambda i, j: (i, j)),
        out_shape=jax.ShapeDtypeStruct((t, n), BF16),
        scratch_shapes=[pltpu.VMEM((tm, d), BF16)],
        compiler_params=_params(2),
        name="in_proj",
    )(x2, modt, modt, nw, w_in, qw, kw)


def _na_bias_tables(rpb, rows):
    tables = []
    for r0, kr0 in ((0, 0), (2 * NA_QROWS, 2 * NA_QROWS - NA_KH // 2), (rows - NA_QROWS, rows - NA_KROWS)):
        r = r0 + np.arange(NA_QROWS)
        rs = np.clip(r - NA_KH // 2, 0, rows - NA_KH)
        key_row = kr0 + np.arange(NA_KROWS)
        valid_r = (key_row[None, :] >= rs[:, None]) & (key_row[None, :] < rs[:, None] + NA_KH)
        row_off = np.clip(key_row[None, :] - r[:, None] + (NA_KH - 1), 0, 2 * NA_KH - 2)
        cols = np.arange(GRID_W)
        cs = np.clip(cols - NA_KW // 2, 0, GRID_W - NA_KW)
        valid_c = (cols[None, :] >= cs[:, None]) & (cols[None, :] < cs[:, None] + NA_KW)
        col_off = np.clip(cols[None, :] - cols[:, None] + (NA_KW - 1), 0, 2 * NA_KW - 2)
        shape = (NA_QROWS, GRID_W, NA_KROWS, GRID_W)
        valid = np.broadcast_to(valid_r[:, None, :, None] & valid_c[None, :, None, :], shape)
        ro = np.broadcast_to(row_off[:, None, :, None], shape).reshape(NA_QROWS * GRID_W, NA_KROWS * GRID_W)
        co = np.broadcast_to(col_off[None, :, None, :], shape).reshape(NA_QROWS * GRID_W, NA_KROWS * GRID_W)
        valid = valid.reshape(NA_QROWS * GRID_W, NA_KROWS * GRID_W)
        tables.append(jnp.where(valid[None], rpb[:, ro, co], NEG_BIG))
    return jnp.stack(tables).astype(F32)


def _qk(q, k):
    return lax.dot_general(q, k, (((1,), (1,)), ((), ())), preferred_element_type=F32)


def _na_kernel(q_ref, k0_ref, k1_ref, k2_ref, v0_ref, v1_ref, v2_ref, kc_ref, vc_ref, bias_ref, o_ref):
    blk = k0_ref.shape[0]
    for h in range(NA_HEADS):
        hs = slice(h * HEAD_DIM, (h + 1) * HEAD_DIM)
        q = q_ref[:, hs]
        s_loc = [_qk(q, kr[:, hs]) + bias_ref[h, :, m * blk:(m + 1) * blk]
                 for m, kr in enumerate((k0_ref, k1_ref, k2_ref))]
        s_ctx = _qk(q, kc_ref[:, hs])
        mx = jnp.max(s_ctx, axis=-1, keepdims=True)
        for s in s_loc:
            mx = jnp.maximum(mx, jnp.max(s, axis=-1, keepdims=True))
        p_ctx = jnp.exp(s_ctx - mx)
        den = jnp.sum(p_ctx, axis=-1, keepdims=True)
        o = jnp.dot(p_ctx.astype(BF16), vc_ref[:, hs], preferred_element_type=F32)
        for s, vr in zip(s_loc, (v0_ref, v1_ref, v2_ref)):
            p = jnp.exp(s - mx)
            den = den + jnp.sum(p, axis=-1, keepdims=True)
            o = o + jnp.dot(p.astype(BF16), vr[:, hs], preferred_element_type=F32)
        o_ref[:, hs] = (o / den).astype(o_ref.dtype)


def _neighbourhood_attention(proj, proj_ctx, bias, batch, seq):
    blk = NA_QROWS * GRID_W
    nrb = seq // blk
    n_kblk = NA_KROWS // NA_QROWS
    ctx_len = proj_ctx.shape[0] // batch

    def kblock(rb):
        return jnp.clip(rb - 1, 0, nrb - n_kblk)

    def kv_spec(m, col):
        return pl.BlockSpec((blk, NA_WIDTH), lambda b, rb: (b * nrb + kblock(rb) + m, col))

    def variant(rb):
        return jnp.where(rb == 0, 0, jnp.where(rb == nrb - 1, 2, 1))

    in_specs = [pl.BlockSpec((blk, NA_WIDTH), lambda b, rb: (b * nrb + rb, 0))]
    in_specs += [kv_spec(m, 1) for m in range(n_kblk)] + [kv_spec(m, 2) for m in range(n_kblk)]
    in_specs += [pl.BlockSpec((ctx_len, NA_WIDTH), lambda b, rb: (b, 1)),
                 pl.BlockSpec((ctx_len, NA_WIDTH), lambda b, rb: (b, 2)),
                 pl.BlockSpec((None, NA_HEADS, blk, NA_KROWS * GRID_W), lambda b, rb: (variant(rb), 0, 0, 0))]
    return pl.pallas_call(
        _na_kernel,
        grid=(batch, nrb),
        in_specs=in_specs,
        out_specs=pl.BlockSpec((blk, NA_WIDTH), lambda b, rb: (b * nrb + rb, 0)),
        out_shape=jax.ShapeDtypeStruct((batch * seq, NA_WIDTH), BF16),
        compiler_params=_params(2),
        name="neighbourhood_attention",
    )(*([proj] * (1 + 2 * n_kblk)), proj_ctx, proj_ctx, bias)


def _ctx_attn_kernel(q_ref, k_ref, v_ref, o_ref):
    for h in range(NA_HEADS):
        hs = slice(h * HEAD_DIM, (h + 1) * HEAD_DIM)
        s = _qk(q_ref[:, hs], k_ref[:, hs])
        p = jnp.exp(s - jnp.max(s, axis=-1, keepdims=True))
        den = jnp.sum(p, axis=-1, keepdims=True)
        o = jnp.dot(p.astype(BF16), v_ref[:, hs], preferred_element_type=F32)
        o_ref[:, hs] = (o / den).astype(o_ref.dtype)


def _context_attention(proj_ctx, batch):
    ctx_len = proj_ctx.shape[0] // batch
    spec = lambda col: pl.BlockSpec((ctx_len, NA_WIDTH), lambda b: (b, col))
    return pl.pallas_call(
        _ctx_attn_kernel,
        grid=(batch,),
        in_specs=[spec(0), spec(1), spec(2)],
        out_specs=spec(0),
        out_shape=jax.ShapeDtypeStruct((proj_ctx.shape[0], NA_WIDTH), BF16),
        compiler_params=_params(1),
        name="context_attention",
    )(proj_ctx, proj_ctx, proj_ctx)


def _dft_cos_sin(n_out, n_in, period):
    ij = (np.arange(n_out, dtype=np.int64)[:, None] * np.arange(n_in, dtype=np.int64)[None, :]) % period
    ang = 2.0 * np.pi * ij.astype(np.float64) / period
    return np.cos(ang), np.sin(ang)


def _ft_channel_kernel(u_ref, m_ref, o_ref):
    m = m_ref[...]
    for g in range(FT_GROUPS):
        z = jnp.dot(u_ref[:, g * GROUP_DIM:(g + 1) * GROUP_DIM], m, preferred_element_type=F32)
        o_ref[:, g * 2 * GROUP_DIM:(g + 1) * 2 * GROUP_DIM] = z.astype(o_ref.dtype)


def _ft_stage_a_kernel(x_ref, m_ref, ta_ref, tb_ref, o_ref):
    y = jnp.dot(x_ref[...], m_ref[...], preferred_element_type=F32)
    half = y.shape[1] // 2
    for s in range(y.shape[0] // GROUP_DIM):
        ys = y[s * GROUP_DIM:(s + 1) * GROUP_DIM, :]
        sw = jnp.concatenate([ys[:, half:], ys[:, :half]], axis=1)
        o_ref[s * GROUP_DIM:(s + 1) * GROUP_DIM, :] = (ys * ta_ref[s:s + 1, :] + sw * tb_ref[s:s + 1, :]).astype(o_ref.dtype)


def _ft_stage_b_kernel(x_ref, m_ref, o_ref):
    o_ref[...] = jnp.dot(x_ref[...], m_ref[...], preferred_element_type=F32).astype(o_ref.dtype)


def _fourier_mix(proj, batch, seq):
    t = batch * seq
    n_b = GROUP_DIM
    n_a = seq // n_b
    assert (2 * n_a) % 128 == 0
    g, c = FT_GROUPS, GROUP_DIM

    cc, sc = _dft_cos_sin(c, c, c)
    m_c = jnp.asarray(np.concatenate([cc, -sc], axis=1), BF16)
    ca, sa = _dft_cos_sin(n_a, n_a, n_a)
    m_a = jnp.asarray(np.block([[ca, -sa], [sa, ca]]), BF16)
    tc, ts = _dft_cos_sin(n_b, n_a, seq)
    t_a = jnp.asarray(np.concatenate([tc, tc], axis=1), F32)
    t_b = jnp.asarray(np.concatenate([ts, -ts], axis=1), F32)
    cb, sb = _dft_cos_sin(n_b, n_b, n_b)
    m_b = jnp.asarray(np.concatenate([cb, sb], axis=0) / math.sqrt(seq * c), BF16)

    tm = 2048
    z = pl.pallas_call(
        _ft_channel_kernel,
        grid=(t // tm,),
        in_specs=[pl.BlockSpec((tm, MIX_WIDTH), lambda i: (i, COL_FT)),
                  pl.BlockSpec((c, 2 * c), lambda i: (0, 0))],
        out_specs=pl.BlockSpec((tm, 2 * g * c), lambda i: (i, 0)),
        out_shape=jax.ShapeDtypeStruct((t, 2 * g * c), BF16),
        compiler_params=_params(1),
        name="fourier_channels",
    )(proj, m_c)

    z = z.reshape(batch, n_a, n_b, g, 2, c).transpose(0, 3, 2, 5, 4, 1).reshape(batch * g * n_b * c, 2 * n_a)
    rows = z.shape[0]
    sub = tm // c
    y = pl.pallas_call(
        _ft_stage_a_kernel,
        grid=(rows // tm,),
        in_specs=[pl.BlockSpec((tm, 2 * n_a), lambda i: (i, 0)),
                  pl.BlockSpec((2 * n_a, 2 * n_a), lambda i: (0, 0)),
                  pl.BlockSpec((sub, 2 * n_a), lambda i: (i % (n_b // sub), 0)),
                  pl.BlockSpec((sub, 2 * n_a), lambda i: (i % (n_b // sub), 0))],
        out_specs=pl.BlockSpec((tm, 2 * n_a), lambda i: (i, 0)),
        out_shape=jax.ShapeDtypeStruct((rows, 2 * n_a), BF16),
        compiler_params=_params(1),
        name="fourier_stage_a",
    )(z, m_a, t_a, t_b)

    y = y.reshape(batch, g, n_b, c, 2, n_a).transpose(0, 1, 3, 5, 4, 2).reshape(batch * g * c * n_a, 2 * n_b)
    rows = y.shape[0]
    o = pl.pallas_call(
        _ft_stage_b_kernel,
        grid=(rows // tm,),
        in_specs=[pl.BlockSpec((tm, 2 * n_b), lambda i: (i, 0)),
                  pl.BlockSpec((2 * n_b, n_b), lambda i: (0, 0))],
        out_specs=pl.BlockSpec((tm, n_b), lambda i: (i, 0)),
        out_shape=jax.ShapeDtypeStruct((rows, n_b), BF16),
        compiler_params=_params(1),
        name="fourier_stage_b",
    )(y, m_b)
    return o.reshape(batch, g, c, n_a, n_b).transpose(0, 4, 3, 1, 2).reshape(t, g * c)


def _ctx_fourier_kernel(u_ref, mc_ref, ml_ref, o_ref):
    mc = mc_ref[...]
    ml = ml_ref[...]
    for g in range(FT_GROUPS):
        z = jnp.dot(u_ref[:, g * GROUP_DIM:(g + 1) * GROUP_DIM], mc, preferred_element_type=F32)
        stack = jnp.concatenate([z[:, :GROUP_DIM], z[:, GROUP_DIM:]], axis=0).astype(BF16)
        o_ref[:, g * GROUP_DIM:(g + 1) * GROUP_DIM] = jnp.dot(ml, stack, preferred_element_type=F32).astype(o_ref.dtype)


def _context_fourier(proj_ctx, batch):
    ctx_len = proj_ctx.shape[0] // batch
    c = GROUP_DIM
    cc, sc = _dft_cos_sin(c, c, c)
    m_c = jnp.asarray(np.concatenate([cc, -sc], axis=1), BF16)
    cl, sl = _dft_cos_sin(ctx_len, ctx_len, ctx_len)
    m_l = jnp.asarray(np.concatenate([cl, sl], axis=1) / math.sqrt(ctx_len * c), BF16)
    return pl.pallas_call(
        _ctx_fourier_kernel,
        grid=(batch,),
        in_specs=[pl.BlockSpec((ctx_len, MIX_WIDTH), lambda b: (b, COL_FT)),
                  pl.BlockSpec((c, 2 * c), lambda b: (0, 0)),
                  pl.BlockSpec((ctx_len, 2 * ctx_len), lambda b: (0, 0))],
        out_specs=pl.BlockSpec((ctx_len, MIX_WIDTH), lambda b: (b, 0)),
        out_shape=jax.ShapeDtypeStruct((proj_ctx.shape[0], MIX_WIDTH), BF16),
        compiler_params=_params(1),
        name="context_fourier",
    )(proj_ctx, m_c, m_l)


def _merge_kernel(x_ref, sh_ref, sc_ref, nw_ref,
                  pool_ref, pool_p_ref, pool_n_ref, sv_ref, sv_p_ref, sv_n_ref, scv_ref, scv_p_ref, scv_n_ref,
                  sb_ref, attn_ref, four_ref, wg_ref, wpa_ref, wpb_ref, wpc_ref, wpd_ref,
                  wpool_ref, pscale_ref, convw_ref, o_ref,
                  h_scr, brc_scr, brd_scr, ext_scr, *, tiles_per_seq, seq_len):
    i = pl.program_id(0)
    j = pl.program_id(1)
    tm = x_ref.shape[0]

    @pl.when(j == 0)
    def _():
        _norm_mod_store(h_scr, x_ref, nw_ref, sh_ref, sc_ref)
        first = (i % tiles_per_seq) == 0
        last = (i % tiles_per_seq) == tiles_per_seq - 1
        pos = (i % tiles_per_seq) * tm + lax.broadcasted_iota(jnp.int32, (tm, 1), 0)

        def fill_ext(prev, main, nxt):
            ext_scr[0:HALO, :] = jnp.where(first, 0.0, prev)
            ext_scr[HALO:HALO + tm, :] = main
            ext_scr[HALO + tm:HALO + tm + HALO, :] = jnp.where(last, 0.0, nxt)

        fill_ext(pool_p_ref[...].astype(F32), pool_ref[...].astype(F32), pool_n_ref[...].astype(F32))
        for g, w in enumerate(POOL_WINDOWS):
            cs = slice(g * GROUP_DIM, (g + 1) * GROUP_DIM)
            win = ext_scr[HALO - w // 2:HALO - w // 2 + tm, cs]
            for dlt in range(-w // 2 + 1, w // 2):
                win = win + ext_scr[HALO + dlt:HALO + dlt + tm, cs]
            cnt = (jnp.minimum(pos + w // 2, seq_len) - jnp.maximum(pos - w // 2, 0)).astype(F32)
            p = win / cnt - ext_scr[HALO:HALO + tm, cs]
            yg = jnp.dot(p.astype(BF16), wpool_ref[g], preferred_element_type=F32) * pscale_ref[:, cs]
            brc_scr[:, cs] = yg.astype(BF16)

        fill_ext(sv_p_ref[...].astype(F32) * scv_p_ref[...].astype(F32),
                 sv_ref[...].astype(F32) * scv_ref[...].astype(F32),
                 sv_n_ref[...].astype(F32) * scv_n_ref[...].astype(F32))
        y = (ext_scr[HALO - 1:HALO - 1 + tm, :] * convw_ref[0:1, :]
             + ext_scr[HALO:HALO + tm, :] * convw_ref[1:2, :]
             + ext_scr[HALO + 1:HALO + 1 + tm, :] * convw_ref[2:3, :])
        brd_scr[...] = (sb_ref[...].astype(F32) * y).astype(BF16)

    h = h_scr[...]
    branches = ((attn_ref, wpa_ref), (four_ref, wpb_ref), (brc_scr, wpc_ref), (brd_scr, wpd_ref))
    y = None
    for k, (br_ref, wp_ref) in enumerate(branches):
        gate = jax.nn.sigmoid(jnp.dot(h, wg_ref[k], preferred_element_type=F32))
        term = gate * jnp.dot(br_ref[...], wp_ref[...], preferred_element_type=F32)
        y = term if y is None else y + term
    o_ref[...] = y.astype(o_ref.dtype)


def _merge(x2, modt, modrow, nw, proj, attn, four, wg, wpa, wpb, wpc, wpd, wpool, pscale, convw, seq_len, tm, tn):
    t, d = x2.shape
    row_fn = _row_fn(modrow, tm)
    tiles_per_seq = seq_len // tm
    n_halo = t // HALO

    def halo_specs(col):
        return [pl.BlockSpec((tm, MIX_WIDTH), lambda i, j: (i, col)),
                pl.BlockSpec((HALO, MIX_WIDTH), lambda i, j: (jnp.maximum(i * (tm // HALO) - 1, 0), col)),
                pl.BlockSpec((HALO, MIX_WIDTH), lambda i, j: (jnp.minimum((i + 1) * (tm // HALO), n_halo - 1), col))]

    in_specs = [pl.BlockSpec((tm, d), lambda i, j: (i, 0)),
                _mod_spec(d, 0, row_fn), _mod_spec(d, 1, row_fn),
                pl.BlockSpec((1, d), lambda i, j: (0, 0))]
    in_specs += halo_specs(COL_POOL) + halo_specs(COL_SV) + halo_specs(COL_SC)
    in_specs += [pl.BlockSpec((tm, MIX_WIDTH), lambda i, j: (i, COL_SB)),
                 pl.BlockSpec((tm, NA_WIDTH), lambda i, j: (i, 0)),
                 pl.BlockSpec((tm, MIX_WIDTH), lambda i, j: (i, 0)),
                 pl.BlockSpec((4, d, tn), lambda i, j: (0, 0, j)),
                 pl.BlockSpec((NA_WIDTH, tn), lambda i, j: (0, j)),
                 pl.BlockSpec((MIX_WIDTH, tn), lambda i, j: (0, j)),
                 pl.BlockSpec((MIX_WIDTH, tn), lambda i, j: (0, j)),
                 pl.BlockSpec((MIX_WIDTH, tn), lambda i, j: (0, j)),
                 pl.BlockSpec((len(POOL_WINDOWS), GROUP_DIM, GROUP_DIM), lambda i, j: (0, 0, 0)),
                 pl.BlockSpec((1, MIX_WIDTH), lambda i, j: (0, 0)),
                 pl.BlockSpec((3, MIX_WIDTH), lambda i, j: (0, 0))]
    return pl.pallas_call(
        functools.partial(_merge_kernel, tiles_per_seq=tiles_per_seq, seq_len=seq_len),
        grid=(t // tm, d // tn),
        in_specs=in_specs,
        out_specs=pl.BlockSpec((tm, tn), lambda i, j: (i, j)),
        out_shape=jax.ShapeDtypeStruct((t, d), BF16),
        scratch_shapes=[pltpu.VMEM((tm, d), BF16), pltpu.VMEM((tm, MIX_WIDTH), BF16),
                        pltpu.VMEM((tm, MIX_WIDTH), BF16), pltpu.VMEM((tm + 2 * HALO, MIX_WIDTH), F32)],
        compiler_params=_params(2),
        name="gated_merge",
    )(x2, modt, modt, nw, *([proj] * 10), attn, four, wg, wpa, wpb, wpc, wpd, wpool, pscale, convw)


def _resid_proj_kernel(a_ref, w_ref, x_ref, g_ref, o_ref):
    o_ref[...] = x_ref[...] + g_ref[...] * jnp.dot(a_ref[...], w_ref[...], preferred_element_type=F32)


def _resid_proj(a, w, x2, modt, chunk, modrow, tm, tn):
    t, k = a.shape
    row_fn = _row_fn(modrow, tm)
    d = w.shape[1]
    return pl.pallas_call(
        _resid_proj_kernel,
        grid=(d // tn, t // tm),
        in_specs=[pl.BlockSpec((tm, k), lambda j, i: (i, 0)),
                  pl.BlockSpec((k, tn), lambda j, i: (0, j)),
                  pl.BlockSpec((tm, tn), lambda j, i: (i, j)),
                  pl.BlockSpec((None, 1, tn), lambda j, i: (row_fn(i) * N_MOD + chunk, 0, j))],
        out_specs=pl.BlockSpec((tm, tn), lambda j, i: (i, j)),
        out_shape=jax.ShapeDtypeStruct((t, d), F32),
        compiler_params=_params(2),
        name="residual_projection",
    )(a, w, x2, modt)


def _ffn_up_kernel(x_ref, sh_ref, sc_ref, nw_ref, wg_ref, wu_ref, o_ref, h_scr):
    @pl.when(pl.program_id(1) == 0)
    def _():
        _norm_mod_store(h_scr, x_ref, nw_ref, sh_ref, sc_ref)

    h = h_scr[...]
    gt = jnp.dot(h, wg_ref[...], preferred_element_type=F32)
    up = jnp.dot(h, wu_ref[...], preferred_element_type=F32)
    o_ref[...] = (gt * jax.nn.sigmoid(gt) * up).astype(o_ref.dtype)


def _ffn_up(x2, modt, modrow, nw, w_g, w_u, tm, tn):
    t, d = x2.shape
    row_fn = _row_fn(modrow, tm)
    n = w_g.shape[1]
    return pl.pallas_call(
        _ffn_up_kernel,
        grid=(t // tm, n // tn),
        in_specs=[pl.BlockSpec((tm, d), lambda i, j: (i, 0)),
                  _mod_spec(d, 3, row_fn), _mod_spec(d, 4, row_fn),
                  pl.BlockSpec((1, d), lambda i, j: (0, 0)),
                  pl.BlockSpec((d, tn), lambda i, j: (0, j)),
                  pl.BlockSpec((d, tn), lambda i, j: (0, j))],
        out_specs=pl.BlockSpec((tm, tn), lambda i, j: (i, j)),
        out_shape=jax.ShapeDtypeStruct((t, n), BF16),
        scratch_shapes=[pltpu.VMEM((tm, d), BF16)],
        compiler_params=_params(2),
        name="ffn_up",
    )(x2, modt, modt, nw, w_g, w_u)


def _stream_layer(x2, modt, modrow, seq_len, tm, lw, mixers):
    proj = _in_proj(x2, modt, modrow, lw["norm1"], lw["w_in"], lw["qw"], lw["kw"], tm)
    attn, four = mixers(proj)
    y = _merge(x2, modt, modrow, lw["norm1"], proj, attn, four, lw["w_gate"], lw["w_pa"], lw["w_pb"], lw["w_pc"],
               lw["w_pd"], lw["w_pool"], lw["pool_scale"], lw["conv_w"], seq_len, min(tm, 512), 512)
    x2 = _resid_proj(y, lw["w_o"], x2, modt, 2, modrow, min(tm, 512), x2.shape[1])
    a = _ffn_up(x2, modt, modrow, lw["norm2"], lw["w_ffn_gate"], lw["w_ffn_up"], tm, 512)
    return _resid_proj(a, lw["w_ffn_down"], x2, modt, 5, modrow, min(tm, 512), min(x2.shape[1], 1024))


def kernel(x, c, ctx, c_ctx, w_mod, b_mod, norm1_w, w_in, q_norm_w, k_norm_w, rpb, w_pool, pool_scale, conv_w,
           w_gate, w_pa, w_pb, w_pc, w_pd, w_o, norm2_w, w_ffn_gate, w_ffn_up, w_ffn_down):
    batch, seq, d = x.shape
    ctx_len = ctx.shape[1]
    depth = w_mod.shape[0]
    rows = seq // GRID_W
    assert seq % (NA_QROWS * GRID_W) == 0 and rows >= NA_KROWS and batch + 1 <= MOD_ROWS

    cvec = jnp.zeros((MOD_ROWS, d), F32).at[:batch].set(c).at[batch].set(c_ctx)
    mod = _modulation(cvec, w_mod, b_mod)

    tm_lat = 1024
    lat_row = (0, seq)
    ctx_row = (batch, batch * ctx_len)

    x2 = x.reshape(batch * seq, d)
    xc = ctx.reshape(batch * ctx_len, d)
    scale = HEAD_DIM ** -0.5
    for l in range(depth):
        lw = {
            "norm1": norm1_w[l].reshape(1, d), "norm2": norm2_w[l].reshape(1, d),
            "w_in": w_in[l].astype(BF16),
            "qw": (q_norm_w[l] * scale).reshape(1, HEAD_DIM), "kw": k_norm_w[l].reshape(1, HEAD_DIM),
            "w_gate": w_gate[l].astype(BF16), "w_pa": w_pa[l].astype(BF16), "w_pb": w_pb[l].astype(BF16),
            "w_pc": w_pc[l].astype(BF16), "w_pd": w_pd[l].astype(BF16), "w_o": w_o[l].astype(BF16),
            "w_pool": w_pool[l].astype(BF16), "pool_scale": pool_scale[l].reshape(1, MIX_WIDTH),
            "conv_w": conv_w[l],
            "w_ffn_gate": w_ffn_gate[l].astype(BF16), "w_ffn_up": w_ffn_up[l].astype(BF16),
            "w_ffn_down": w_ffn_down[l].astype(BF16),
        }
        modt = mod[l].reshape(MOD_ROWS * N_MOD, 1, d)
        bias = _na_bias_tables(rpb[l], rows)

        if l == depth - 1:
            proj_c = _in_proj(xc, modt, ctx_row, lw["norm1"], lw["w_in"], lw["qw"], lw["kw"], ctx_len)
        else:
            holder = {}

            def ctx_mixers(proj):
                holder["proj"] = proj
                return _context_attention(proj, batch), _context_fourier(proj, batch)

            xc = _stream_layer(xc, modt, ctx_row, ctx_len, ctx_len, lw, ctx_mixers)
            proj_c = holder["proj"]

        def lat_mixers(proj):
            return (_neighbourhood_attention(proj, proj_c, bias, batch, seq),
                    _fourier_mix(proj, batch, seq))

        x2 = _stream_layer(x2, modt, lat_row, seq, tm_lat, lw, lat_mixers)
    return x2.reshape(batch, seq, d)
```

```python
import functools
import math

import numpy as np
import jax
import jax.numpy as jnp
from jax import lax
from jax.experimental import pallas as pl
from jax.experimental.pallas import tpu as pltpu

F32 = jnp.float32
BF16 = jnp.bfloat16

GRID_W = 64
NA_HEADS = 8
HEAD_DIM = 128
NA_WIDTH = NA_HEADS * HEAD_DIM
NA_KH = 8
NA_KW = 16
NA_QROWS = 4
NA_KROWS = 12
GROUP_DIM = 128
FT_GROUPS = 4
POOL_WINDOWS = (2, 4, 8, 16)
MIX_WIDTH = 512
N_MOD = 6
RMS_EPS = 1e-6
BF16_ROWS = 16
HALO = BF16_ROWS
NORM_ROWS = 2 * BF16_ROWS
MOD_ROWS = 8
NEG_BIG = -1e30

COL_FT, COL_POOL, COL_SV, COL_SB, COL_SC = 6, 7, 8, 9, 10

VMEM_LIMIT = 56 * 1024 * 1024


def _params(n_axes):
    return pltpu.CompilerParams(dimension_semantics=("arbitrary",) * n_axes,
                                vmem_limit_bytes=VMEM_LIMIT)


def _mod_kernel(c_ref, w_ref, b_ref, o_ref):
    cv = c_ref[...]
    s = cv * jax.nn.sigmoid(cv)
    o_ref[...] = jnp.dot(s.astype(BF16), w_ref[...].astype(BF16),
                         preferred_element_type=F32) + b_ref[...]


def _modulation(cvec, w_mod, b_mod):
    depth, d, n = w_mod.shape
    tn = 1024
    return pl.pallas_call(
        _mod_kernel,
        grid=(depth, n // tn),
        in_specs=[
            pl.BlockSpec((MOD_ROWS, d), lambda l, j: (0, 0)),
            pl.BlockSpec((None, d, tn), lambda l, j: (l, 0, j)),
            pl.BlockSpec((None, 1, tn), lambda l, j: (l, 0, j)),
        ],
        out_specs=pl.BlockSpec((None, MOD_ROWS, tn), lambda l, j: (l, 0, j)),
        out_shape=jax.ShapeDtypeStruct((depth, MOD_ROWS, n), F32),
        compiler_params=_params(2),
        name="modulation",
    )(cvec, w_mod, b_mod.reshape(depth, 1, n))


def _norm_mod_store(h_scr, x_ref, nw_ref, sh_ref, sc_ref):
    tm = x_ref.shape[0]
    w = nw_ref[...] * (1.0 + sc_ref[...])
    sh = sh_ref[...]

    def body(r, carry):
        rows = pl.ds(pl.multiple_of(r * NORM_ROWS, NORM_ROWS), NORM_ROWS)
        xf = x_ref[rows, :]
        ms = jnp.mean(xf * xf, axis=-1, keepdims=True)
        h_scr[rows, :] = (xf * lax.rsqrt(ms + RMS_EPS) * w + sh).astype(BF16)
        return carry

    lax.fori_loop(0, tm // NORM_ROWS, body, 0, unroll=4)


def _row_fn(modrow, tm):
    base, tokens_per_row = modrow
    return lambda i: base + (i * tm) // tokens_per_row


def _mod_spec(d_block, chunk, row_fn):
    return pl.BlockSpec((None, 1, d_block), lambda i, j: (row_fn(i) * N_MOD + chunk, 0, 0))


def _in_proj_kernel(x_ref, sh_ref, sc_ref, nw_ref, w_ref, qw_ref, kw_ref, o_ref, h_scr, *, n_head_blocks):
    j = pl.program_id(1)

    @pl.when(j == 0)
    def _():
        _norm_mod_store(h_scr, x_ref, nw_ref, sh_ref, sc_ref)

    acc = jnp.dot(h_scr[...], w_ref[...], preferred_element_type=F32)

    def head_norm(w):
        for hh in range(acc.shape[1] // HEAD_DIM):
            a = acc[:, hh * HEAD_DIM:(hh + 1) * HEAD_DIM]
            ms = jnp.mean(a * a, axis=-1, keepdims=True)
            o_ref[:, hh * HEAD_DIM:(hh + 1) * HEAD_DIM] = (a * lax.rsqrt(ms + RMS_EPS) * w).astype(o_ref.dtype)

    @pl.when(j < n_head_blocks)
    def _():
        head_norm(qw_ref[...])

    @pl.when((j >= n_head_blocks) & (j < 2 * n_head_blocks))
    def _():
        head_norm(kw_ref[...])

    @pl.when(j >= 2 * n_head_blocks)
    def _():
        o_ref[...] = acc.astype(o_ref.dtype)


def _in_proj(x2, modt, modrow, nw, w_in, qw, kw, tm):
    t, d = x2.shape
    row_fn = _row_fn(modrow, tm)
    n = w_in.shape[1]
    tn = MIX_WIDTH
    return pl.pallas_call(
        functools.partial(_in_proj_kernel, n_head_blocks=NA_WIDTH // tn),
        grid=(t // tm, n // tn),
        in_specs=[
            pl.BlockSpec((tm, d), lambda i, j: (i, 0)),
            _mod_spec(d, 0, row_fn),
            _mod_spec(d, 1, row_fn),
            pl.BlockSpec((1, d), lambda i, j: (0, 0)),
            pl.BlockSpec((d, tn), lambda i, j: (0, j)),
            pl.BlockSpec((1, HEAD_DIM), lambda i, j: (0, 0)),
            pl.BlockSpec((1, HEAD_DIM), lambda i, j: (0, 0)),
        ],
        out_specs=pl.BlockSpec((tm, tn), lambda i, j: (i, j)),
        out_shape=jax.ShapeDtypeStruct((t, n), BF16),
        scratch_shapes=[pltpu.VMEM((tm, d), BF16)],
        compiler_params=_params(2),
        name="in_proj",
    )(x2, modt, modt, nw, w_in, qw, kw)


def _na_bias_tables(rpb, rows):
    heads = rpb.shape[0]
    cols = np.arange(GRID_W)
    cs = np.clip(cols - NA_KW // 2, 0, GRID_W - NA_KW)
    valid_c = (cols[None, :] >= cs[:, None]) & (cols[None, :] < cs[:, None] + NA_KW)
    pad = GRID_W - NA_KW
    rp = jnp.pad(rpb.astype(F32), ((0, 0), (0, 0), (pad, pad)))
    col_t = jnp.stack([rp[:, :, GRID_W - 1 - qc:2 * GRID_W - 1 - qc] for qc in range(GRID_W)], axis=2)
    tables = []
    for r0, kr0 in ((0, 0), (2 * NA_QROWS, 2 * NA_QROWS - NA_KH // 2), (rows - NA_QROWS, rows - NA_KROWS)):
        r = r0 + np.arange(NA_QROWS)
        rs = np.clip(r - NA_KH // 2, 0, rows - NA_KH)
        key_row = kr0 + np.arange(NA_KROWS)
        valid_r = (key_row[None, :] >= rs[:, None]) & (key_row[None, :] < rs[:, None] + NA_KH)
        row_off = np.clip(key_row[None, :] - r[:, None] + (NA_KH - 1), 0, 2 * NA_KH - 2)
        tab = jnp.stack([jnp.stack([col_t[:, int(row_off[qr, kr])] for kr in range(NA_KROWS)], axis=2)
                         for qr in range(NA_QROWS)], axis=1)
        valid = valid_r[:, None, :, None] & valid_c[None, :, None, :]
        tab = jnp.where(valid[None], tab, NEG_BIG)
        tables.append(tab.reshape(heads, NA_QROWS * GRID_W, NA_KROWS * GRID_W))
    return jnp.stack(tables)


def _qk(q, k):
    return lax.dot_general(q, k, (((1,), (1,)), ((), ())), preferred_element_type=F32)


def _lane_tiles(a, width=128):
    return [a[:, k:k + width] for k in range(0, a.shape[1], width)]


def _na_kernel(q_ref, k0_ref, k1_ref, k2_ref, v0_ref, v1_ref, v2_ref, kc_ref, vc_ref, bias_ref, o_ref):
    blk = k0_ref.shape[0]
    for h in range(NA_HEADS):
        hs = slice(h * HEAD_DIM, (h + 1) * HEAD_DIM)
        q = q_ref[:, hs]
        s_loc = [_qk(q, kr[:, hs]) + bias_ref[h, :, m * blk:(m + 1) * blk]
                 for m, kr in enumerate((k0_ref, k1_ref, k2_ref))]
        scores = [_qk(q, kc_ref[:, hs])] + s_loc
        values = (vc_ref, v0_ref, v1_ref, v2_ref)
        mx = jnp.max(functools.reduce(jnp.maximum, [t for s in scores for t in _lane_tiles(s)]),
                     axis=-1, keepdims=True)
        probs = [jnp.exp(s - mx) for s in scores]
        den = jnp.sum(functools.reduce(jnp.add, [t for p in probs for t in _lane_tiles(p)]),
                      axis=-1, keepdims=True)
        o = None
        for p, vr in zip(probs, values):
            pv = jnp.dot(p.astype(BF16), vr[:, hs], preferred_element_type=F32)
            o = pv if o is None else o + pv
        o_ref[:, hs] = (o / den).astype(o_ref.dtype)


def _neighbourhood_attention(proj, proj_ctx, bias, batch, seq):
    blk = NA_QROWS * GRID_W
    nrb = seq // blk
    n_kblk = NA_KROWS // NA_QROWS
    ctx_len = proj_ctx.shape[0] // batch

    def kblock(rb):
        return jnp.clip(rb - 1, 0, nrb - n_kblk)

    def kv_spec(m, col):
        return pl.BlockSpec((blk, NA_WIDTH), lambda b, rb: (b * nrb + kblock(rb) + m, col))

    def variant(rb):
        return jnp.where(rb == 0, 0, jnp.where(rb == nrb - 1, 2, 1))

    in_specs = [pl.BlockSpec((blk, NA_WIDTH), lambda b, rb: (b * nrb + rb, 0))]
    in_specs += [kv_spec(m, 1) for m in range(n_kblk)] + [kv_spec(m, 2) for m in range(n_kblk)]
    in_specs += [pl.BlockSpec((ctx_len, NA_WIDTH), lambda b, rb: (b, 1)),
                 pl.BlockSpec((ctx_len, NA_WIDTH), lambda b, rb: (b, 2)),
                 pl.BlockSpec((None, NA_HEADS, blk, NA_KROWS * GRID_W), lambda b, rb: (variant(rb), 0, 0, 0))]
    return pl.pallas_call(
        _na_kernel,
        grid=(batch, nrb),
        in_specs=in_specs,
        out_specs=pl.BlockSpec((blk, NA_WIDTH), lambda b, rb: (b * nrb + rb, 0)),
        out_shape=jax.ShapeDtypeStruct((batch * seq, NA_WIDTH), BF16),
        compiler_params=_params(2),
        name="neighbourhood_attention",
    )(*([proj] * (1 + 2 * n_kblk)), proj_ctx, proj_ctx, bias)


def _ctx_attn_kernel(q_ref, k_ref, v_ref, o_ref):
    for h in range(NA_HEADS):
        hs = slice(h * HEAD_DIM, (h + 1) * HEAD_DIM)
        s = _qk(q_ref[:, hs], k_ref[:, hs])
        p = jnp.exp(s - jnp.max(s, axis=-1, keepdims=True))
        den = jnp.sum(p, axis=-1, keepdims=True)
        o = jnp.dot(p.astype(BF16), v_ref[:, hs], preferred_element_type=F32)
        o_ref[:, hs] = (o / den).astype(o_ref.dtype)


def _context_attention(proj_ctx, batch):
    ctx_len = proj_ctx.shape[0] // batch
    spec = lambda col: pl.BlockSpec((ctx_len, NA_WIDTH), lambda b: (b, col))
    return pl.pallas_call(
        _ctx_attn_kernel,
        grid=(batch,),
        in_specs=[spec(0), spec(1), spec(2)],
        out_specs=spec(0),
        out_shape=jax.ShapeDtypeStruct((proj_ctx.shape[0], NA_WIDTH), BF16),
        compiler_params=_params(1),
        name="context_attention",
    )(proj_ctx, proj_ctx, proj_ctx)


def _dft_cos_sin(n_out, n_in, period):
    ij = (np.arange(n_out, dtype=np.int64)[:, None] * np.arange(n_in, dtype=np.int64)[None, :]) % period
    ang = 2.0 * np.pi * ij.astype(np.float64) / period
    return np.cos(ang), np.sin(ang)


def _ft_channel_kernel(u_ref, m_ref, o_ref):
    m = m_ref[...]
    for g in range(FT_GROUPS):
        z = jnp.dot(u_ref[:, g * GROUP_DIM:(g + 1) * GROUP_DIM], m, preferred_element_type=F32)
        o_ref[:, g * 2 * GROUP_DIM:(g + 1) * 2 * GROUP_DIM] = z.astype(o_ref.dtype)


def _ft_stage_a_kernel(x_ref, m_ref, ta_ref, tb_ref, o_ref):
    y = jnp.dot(x_ref[...], m_ref[...], preferred_element_type=F32)
    half = y.shape[1] // 2
    for s in range(y.shape[0] // GROUP_DIM):
        ys = y[s * GROUP_DIM:(s + 1) * GROUP_DIM, :]
        sw = jnp.concatenate([ys[:, half:], ys[:, :half]], axis=1)
        o_ref[s * GROUP_DIM:(s + 1) * GROUP_DIM, :] = (ys * ta_ref[s:s + 1, :] + sw * tb_ref[s:s + 1, :]).astype(o_ref.dtype)


def _ft_stage_b_kernel(x_ref, m_ref, o_ref):
    o_ref[...] = jnp.dot(x_ref[...], m_ref[...], preferred_element_type=F32).astype(o_ref.dtype)


def _fourier_mix(proj, batch, seq):
    t = batch * seq
    n_b = GROUP_DIM
    n_a = seq // n_b
    assert (2 * n_a) % 128 == 0
    g, c = FT_GROUPS, GROUP_DIM

    cc, sc = _dft_cos_sin(c, c, c)
    m_c = jnp.asarray(np.concatenate([cc, -sc], axis=1), BF16)
    ca, sa = _dft_cos_sin(n_a, n_a, n_a)
    m_a = jnp.asarray(np.block([[ca, -sa], [sa, ca]]), BF16)
    tc, ts = _dft_cos_sin(n_b, n_a, seq)
    t_a = jnp.asarray(np.concatenate([tc, tc], axis=1), F32)
    t_b = jnp.asarray(np.concatenate([ts, -ts], axis=1), F32)
    cb, sb = _dft_cos_sin(n_b, n_b, n_b)
    m_b = jnp.asarray(np.concatenate([cb, sb], axis=0) / math.sqrt(seq * c), BF16)

    tm = 2048
    z = pl.pallas_call(
        _ft_channel_kernel,
        grid=(t // tm,),
        in_specs=[pl.BlockSpec((tm, MIX_WIDTH), lambda i: (i, COL_FT)),
                  pl.BlockSpec((c, 2 * c), lambda i: (0, 0))],
        out_specs=pl.BlockSpec((tm, 2 * g * c), lambda i: (i, 0)),
        out_shape=jax.ShapeDtypeStruct((t, 2 * g * c), BF16),
        compiler_params=_params(1),
        name="fourier_channels",
    )(proj, m_c)

    z = z.reshape(batch, n_a, n_b, g, 2, c).transpose(0, 3, 2, 5, 4, 1).reshape(batch * g * n_b * c, 2 * n_a)
    rows = z.shape[0]
    sub = tm // c
    y = pl.pallas_call(
        _ft_stage_a_kernel,
        grid=(rows // tm,),
        in_specs=[pl.BlockSpec((tm, 2 * n_a), lambda i: (i, 0)),
                  pl.BlockSpec((2 * n_a, 2 * n_a), lambda i: (0, 0)),
                  pl.BlockSpec((sub, 2 * n_a), lambda i: (i % (n_b // sub), 0)),
                  pl.BlockSpec((sub, 2 * n_a), lambda i: (i % (n_b // sub), 0))],
        out_specs=pl.BlockSpec((tm, 2 * n_a), lambda i: (i, 0)),
        out_shape=jax.ShapeDtypeStruct((rows, 2 * n_a), BF16),
        compiler_params=_params(1),
        name="fourier_stage_a",
    )(z, m_a, t_a, t_b)

    y = y.reshape(batch, g, n_b, c, 2, n_a).transpose(0, 1, 3, 5, 4, 2).reshape(batch * g * c * n_a, 2 * n_b)
    rows = y.shape[0]
    o = pl.pallas_call(
        _ft_stage_b_kernel,
        grid=(rows // tm,),
        in_specs=[pl.BlockSpec((tm, 2 * n_b), lambda i: (i, 0)),
                  pl.BlockSpec((2 * n_b, n_b), lambda i: (0, 0))],
        out_specs=pl.BlockSpec((tm, n_b), lambda i: (i, 0)),
        out_shape=jax.ShapeDtypeStruct((rows, n_b), BF16),
        compiler_params=_params(1),
        name="fourier_stage_b",
    )(y, m_b)
    return o.reshape(batch, g, c, n_a, n_b).transpose(0, 4, 3, 1, 2).reshape(t, g * c)


def _ctx_fourier_kernel(u_ref, mc_ref, ml_ref, o_ref):
    mc = mc_ref[...]
    ml = ml_ref[...]
    for g in range(FT_GROUPS):
        z = jnp.dot(u_ref[:, g * GROUP_DIM:(g + 1) * GROUP_DIM], mc, preferred_element_type=F32)
        stack = jnp.concatenate([z[:, :GROUP_DIM], z[:, GROUP_DIM:]], axis=0).astype(BF16)
        o_ref[:, g * GROUP_DIM:(g + 1) * GROUP_DIM] = jnp.dot(ml, stack, preferred_element_type=F32).astype(o_ref.dtype)


def _context_fourier(proj_ctx, batch):
    ctx_len = proj_ctx.shape[0] // batch
    c = GROUP_DIM
    cc, sc = _dft_cos_sin(c, c, c)
    m_c = jnp.asarray(np.concatenate([cc, -sc], axis=1), BF16)
    cl, sl = _dft_cos_sin(ctx_len, ctx_len, ctx_len)
    m_l = jnp.asarray(np.concatenate([cl, sl], axis=1) / math.sqrt(ctx_len * c), BF16)
    return pl.pallas_call(
        _ctx_fourier_kernel,
        grid=(batch,),
        in_specs=[pl.BlockSpec((ctx_len, MIX_WIDTH), lambda b: (b, COL_FT)),
                  pl.BlockSpec((c, 2 * c), lambda b: (0, 0)),
                  pl.BlockSpec((ctx_len, 2 * ctx_len), lambda b: (0, 0))],
        out_specs=pl.BlockSpec((ctx_len, MIX_WIDTH), lambda b: (b, 0)),
        out_shape=jax.ShapeDtypeStruct((proj_ctx.shape[0], MIX_WIDTH), BF16),
        compiler_params=_params(1),
        name="context_fourier",
    )(proj_ctx, m_c, m_l)


def _merge_kernel(x_ref, sh_ref, sc_ref, nw_ref,
                  pool_ref, pool_p_ref, pool_n_ref, sv_ref, sv_p_ref, sv_n_ref, scv_ref, scv_p_ref, scv_n_ref,
                  sb_ref, attn_ref, four_ref, wg_ref, wpa_ref, wpb_ref, wpc_ref, wpd_ref,
                  wpool_ref, pscale_ref, convw_ref, o_ref,
                  h_scr, brc_scr, brd_scr, ext_scr, *, tiles_per_seq, seq_len):
    i = pl.program_id(0)
    j = pl.program_id(1)
    tm = x_ref.shape[0]

    @pl.when(j == 0)
    def _():
        _norm_mod_store(h_scr, x_ref, nw_ref, sh_ref, sc_ref)
        first = (i % tiles_per_seq) == 0
        last = (i % tiles_per_seq) == tiles_per_seq - 1
        pos = (i % tiles_per_seq) * tm + lax.broadcasted_iota(jnp.int32, (tm, 1), 0)

        def fill_ext(prev, main, nxt):
            ext_scr[0:HALO, :] = jnp.where(first, 0.0, prev)
            ext_scr[HALO:HALO + tm, :] = main
            ext_scr[HALO + tm:HALO + tm + HALO, :] = jnp.where(last, 0.0, nxt)

        fill_ext(pool_p_ref[...].astype(F32), pool_ref[...].astype(F32), pool_n_ref[...].astype(F32))
        for g, w in enumerate(POOL_WINDOWS):
            cs = slice(g * GROUP_DIM, (g + 1) * GROUP_DIM)
            win = ext_scr[HALO - w // 2:HALO - w // 2 + tm, cs]
            for dlt in range(-w // 2 + 1, w // 2):
                win = win + ext_scr[HALO + dlt:HALO + dlt + tm, cs]
            cnt = (jnp.minimum(pos + w // 2, seq_len) - jnp.maximum(pos - w // 2, 0)).astype(F32)
            p = win / cnt - ext_scr[HALO:HALO + tm, cs]
            yg = jnp.dot(p.astype(BF16), wpool_ref[g], preferred_element_type=F32) * pscale_ref[:, cs]
            brc_scr[:, cs] = yg.astype(BF16)

        fill_ext(sv_p_ref[...].astype(F32) * scv_p_ref[...].astype(F32),
                 sv_ref[...].astype(F32) * scv_ref[...].astype(F32),
                 sv_n_ref[...].astype(F32) * scv_n_ref[...].astype(F32))
        y = (ext_scr[HALO - 1:HALO - 1 + tm, :] * convw_ref[0:1, :]
             + ext_scr[HALO:HALO + tm, :] * convw_ref[1:2, :]
             + ext_scr[HALO + 1:HALO + 1 + tm, :] * convw_ref[2:3, :])
        brd_scr[...] = (sb_ref[...].astype(F32) * y).astype(BF16)

    h = h_scr[...]
    branches = ((attn_ref, wpa_ref), (four_ref, wpb_ref), (brc_scr, wpc_ref), (brd_scr, wpd_ref))
    y = None
    for k, (br_ref, wp_ref) in enumerate(branches):
        gate = jax.nn.sigmoid(jnp.dot(h, wg_ref[k], preferred_element_type=F32))
        term = gate * jnp.dot(br_ref[...], wp_ref[...], preferred_element_type=F32)
        y = term if y is None else y + term
    o_ref[...] = y.astype(o_ref.dtype)


def _merge(x2, modt, modrow, nw, proj, attn, four, wg, wpa, wpb, wpc, wpd, wpool, pscale, convw, seq_len, tm, tn):
    t, d = x2.shape
    row_fn = _row_fn(modrow, tm)
    tiles_per_seq = seq_len // tm
    n_halo = t // HALO

    def halo_specs(col):
        return [pl.BlockSpec((tm, MIX_WIDTH), lambda i, j: (i, col)),
                pl.BlockSpec((HALO, MIX_WIDTH), lambda i, j: (jnp.maximum(i * (tm // HALO) - 1, 0), col)),
                pl.BlockSpec((HALO, MIX_WIDTH), lambda i, j: (jnp.minimum((i + 1) * (tm // HALO), n_halo - 1), col))]

    in_specs = [pl.BlockSpec((tm, d), lambda i, j: (i, 0)),
                _mod_spec(d, 0, row_fn), _mod_spec(d, 1, row_fn),
                pl.BlockSpec((1, d), lambda i, j: (0, 0))]
    in_specs += halo_specs(COL_POOL) + halo_specs(COL_SV) + halo_specs(COL_SC)
    in_specs += [pl.BlockSpec((tm, MIX_WIDTH), lambda i, j: (i, COL_SB)),
                 pl.BlockSpec((tm, NA_WIDTH), lambda i, j: (i, 0)),
                 pl.BlockSpec((tm, MIX_WIDTH), lambda i, j: (i, 0)),
                 pl.BlockSpec((4, d, tn), lambda i, j: (0, 0, j)),
                 pl.BlockSpec((NA_WIDTH, tn), lambda i, j: (0, j)),
                 pl.BlockSpec((MIX_WIDTH, tn), lambda i, j: (0, j)),
                 pl.BlockSpec((MIX_WIDTH, tn), lambda i, j: (0, j)),
                 pl.BlockSpec((MIX_WIDTH, tn), lambda i, j: (0, j)),
                 pl.BlockSpec((len(POOL_WINDOWS), GROUP_DIM, GROUP_DIM), lambda i, j: (0, 0, 0)),
                 pl.BlockSpec((1, MIX_WIDTH), lambda i, j: (0, 0)),
                 pl.BlockSpec((3, MIX_WIDTH), lambda i, j: (0, 0))]
    return pl.pallas_call(
        functools.partial(_merge_kernel, tiles_per_seq=tiles_per_seq, seq_len=seq_len),
        grid=(t // tm, d // tn),
        in_specs=in_specs,
        out_specs=pl.BlockSpec((tm, tn), lambda i, j: (i, j)),
        out_shape=jax.ShapeDtypeStruct((t, d), BF16),
        scratch_shapes=[pltpu.VMEM((tm, d), BF16), pltpu.VMEM((tm, MIX_WIDTH), BF16),
                        pltpu.VMEM((tm, MIX_WIDTH), BF16), pltpu.VMEM((tm + 2 * HALO, MIX_WIDTH), F32)],
        compiler_params=_params(2),
        name="gated_merge",
    )(x2, modt, modt, nw, *([proj] * 10), attn, four, wg, wpa, wpb, wpc, wpd, wpool, pscale, convw)


def _resid_proj_kernel(a_ref, w_ref, x_ref, g_ref, o_ref):
    o_ref[...] = x_ref[...] + g_ref[...] * jnp.dot(a_ref[...], w_ref[...], preferred_element_type=F32)


def _resid_proj(a, w, x2, modt, chunk, modrow, tm, tn):
    t, k = a.shape
    row_fn = _row_fn(modrow, tm)
    d = w.shape[1]
    return pl.pallas_call(
        _resid_proj_kernel,
        grid=(d // tn, t // tm),
        in_specs=[pl.BlockSpec((tm, k), lambda j, i: (i, 0)),
                  pl.BlockSpec((k, tn), lambda j, i: (0, j)),
                  pl.BlockSpec((tm, tn), lambda j, i: (i, j)),
                  pl.BlockSpec((None, 1, tn), lambda j, i: (row_fn(i) * N_MOD + chunk, 0, j))],
        out_specs=pl.BlockSpec((tm, tn), lambda j, i: (i, j)),
        out_shape=jax.ShapeDtypeStruct((t, d), F32),
        compiler_params=_params(2),
        name="residual_projection",
    )(a, w, x2, modt)


def _ffn_up_kernel(x_ref, sh_ref, sc_ref, nw_ref, wg_ref, wu_ref, o_ref, h_scr):
    @pl.when(pl.program_id(1) == 0)
    def _():
        _norm_mod_store(h_scr, x_ref, nw_ref, sh_ref, sc_ref)

    h = h_scr[...]
    gt = jnp.dot(h, wg_ref[...], preferred_element_type=F32)
    up = jnp.dot(h, wu_ref[...], preferred_element_type=F32)
    o_ref[...] = (gt * jax.nn.sigmoid(gt) * up).astype(o_ref.dtype)


def _ffn_up(x2, modt, modrow, nw, w_g, w_u, tm, tn):
    t, d = x2.shape
    row_fn = _row_fn(modrow, tm)
    n = w_g.shape[1]
    return pl.pallas_call(
        _ffn_up_kernel,
        grid=(t // tm, n // tn),
        in_specs=[pl.BlockSpec((tm, d), lambda i, j: (i, 0)),
                  _mod_spec(d, 3, row_fn), _mod_spec(d, 4, row_fn),
                  pl.BlockSpec((1, d), lambda i, j: (0, 0)),
                  pl.BlockSpec((d, tn), lambda i, j: (0, j)),
                  pl.BlockSpec((d, tn), lambda i, j: (0, j))],
        out_specs=pl.BlockSpec((tm, tn), lambda i, j: (i, j)),
        out_shape=jax.ShapeDtypeStruct((t, n), BF16),
        scratch_shapes=[pltpu.VMEM((tm, d), BF16)],
        compiler_params=_params(2),
        name="ffn_up",
    )(x2, modt, modt, nw, w_g, w_u)


def _stream_layer(x2, modt, modrow, seq_len, tm, lw, mixers):
    proj = _in_proj(x2, modt, modrow, lw["norm1"], lw["w_in"], lw["qw"], lw["kw"], tm)
    attn, four = mixers(proj)
    y = _merge(x2, modt, modrow, lw["norm1"], proj, attn, four, lw["w_gate"], lw["w_pa"], lw["w_pb"], lw["w_pc"],
               lw["w_pd"], lw["w_pool"], lw["pool_scale"], lw["conv_w"], seq_len, min(tm, 512), 512)
    x2 = _resid_proj(y, lw["w_o"], x2, modt, 2, modrow, min(tm, 512), x2.shape[1])
    a = _ffn_up(x2, modt, modrow, lw["norm2"], lw["w_ffn_gate"], lw["w_ffn_up"], tm, 512)
    return _resid_proj(a, lw["w_ffn_down"], x2, modt, 5, modrow, min(tm, 512), min(x2.shape[1], 1024))


def kernel(x, c, ctx, c_ctx, w_mod, b_mod, norm1_w, w_in, q_norm_w, k_norm_w, rpb, w_pool, pool_scale, conv_w,
           w_gate, w_pa, w_pb, w_pc, w_pd, w_o, norm2_w, w_ffn_gate, w_ffn_up, w_ffn_down):
    batch, seq, d = x.shape
    ctx_len = ctx.shape[1]
    depth = w_mod.shape[0]
    rows = seq // GRID_W
    assert seq % (NA_QROWS * GRID_W) == 0 and rows >= NA_KROWS and batch + 1 <= MOD_ROWS

    cvec = jnp.zeros((MOD_ROWS, d), F32).at[:batch].set(c).at[batch].set(c_ctx)
    mod = _modulation(cvec, w_mod, b_mod)

    tm_lat = 1024
    lat_row = (0, seq)
    ctx_row = (batch, batch * ctx_len)

    x2 = x.reshape(batch * seq, d)
    xc = ctx.reshape(batch * ctx_len, d)
    scale = HEAD_DIM ** -0.5
    for l in range(depth):
        lw = {
            "norm1": norm1_w[l].reshape(1, d), "norm2": norm2_w[l].reshape(1, d),
            "w_in": w_in[l].astype(BF16),
            "qw": (q_norm_w[l] * scale).reshape(1, HEAD_DIM), "kw": k_norm_w[l].reshape(1, HEAD_DIM),
            "w_gate": w_gate[l].astype(BF16), "w_pa": w_pa[l].astype(BF16), "w_pb": w_pb[l].astype(BF16),
            "w_pc": w_pc[l].astype(BF16), "w_pd": w_pd[l].astype(BF16), "w_o": w_o[l].astype(BF16),
            "w_pool": w_pool[l].astype(BF16), "pool_scale": pool_scale[l].reshape(1, MIX_WIDTH),
            "conv_w": conv_w[l],
            "w_ffn_gate": w_ffn_gate[l].astype(BF16), "w_ffn_up": w_ffn_up[l].astype(BF16),
            "w_ffn_down": w_ffn_down[l].astype(BF16),
        }
        modt = mod[l].reshape(MOD_ROWS * N_MOD, 1, d)
        bias = _na_bias_tables(rpb[l], rows)

        if l == depth - 1:
            proj_c = _in_proj(xc, modt, ctx_row, lw["norm1"], lw["w_in"], lw["qw"], lw["kw"], ctx_len)
        else:
            holder = {}

            def ctx_mixers(proj):
                holder["proj"] = proj
                return _context_attention(proj, batch), _context_fourier(proj, batch)

            xc = _stream_layer(xc, modt, ctx_row, ctx_len, ctx_len, lw, ctx_mixers)
            proj_c = holder["proj"]

        def lat_mixers(proj):
            return (_neighbourhood_attention(proj, proj_c, bias, batch, seq),
                    _fourier_mix(proj, batch, seq))

        x2 = _stream_layer(x2, modt, lat_row, seq, tm_lat, lw, lat_mixers)
    return x2.reshape(batch, seq, d)
```

```python
import functools
import math

import numpy as np
import jax
import jax.numpy as jnp
from jax import lax
from jax.experimental import pallas as pl
from jax.experimental.pallas import tpu as pltpu

F32 = jnp.float32
BF16 = jnp.bfloat16

GRID_W = 64
NA_HEADS = 8
HEAD_DIM = 128
NA_WIDTH = NA_HEADS * HEAD_DIM
NA_KH = 8
NA_KW = 16
NA_QROWS = 4
NA_KROWS = 12
GROUP_DIM = 128
FT_GROUPS = 4
POOL_WINDOWS = (2, 4, 8, 16)
MIX_WIDTH = 512
N_MOD = 6
RMS_EPS = 1e-6
BF16_ROWS = 16
HALO = BF16_ROWS
NORM_ROWS = 2 * BF16_ROWS
MOD_ROWS = 8
NEG_BIG = -1e30

COL_FT, COL_POOL, COL_SV, COL_SB, COL_SC = 6, 7, 8, 9, 10

VMEM_LIMIT = 56 * 1024 * 1024


def _params(n_axes):
    return pltpu.CompilerParams(dimension_semantics=("arbitrary",) * n_axes,
                                vmem_limit_bytes=VMEM_LIMIT)


def _mod_kernel(c_ref, w_ref, b_ref, o_ref):
    cv = c_ref[...]
    s = cv * jax.nn.sigmoid(cv)
    o_ref[...] = jnp.dot(s.astype(BF16), w_ref[...].astype(BF16),
                         preferred_element_type=F32) + b_ref[...]


def _modulation(cvec, w_mod, b_mod):
    depth, d, n = w_mod.shape
    tn = 1024
    return pl.pallas_call(
        _mod_kernel,
        grid=(depth, n // tn),
        in_specs=[
            pl.BlockSpec((MOD_ROWS, d), lambda l, j: (0, 0)),
            pl.BlockSpec((None, d, tn), lambda l, j: (l, 0, j)),
            pl.BlockSpec((None, 1, tn), lambda l, j: (l, 0, j)),
        ],
        out_specs=pl.BlockSpec((None, MOD_ROWS, tn), lambda l, j: (l, 0, j)),
        out_shape=jax.ShapeDtypeStruct((depth, MOD_ROWS, n), F32),
        compiler_params=_params(2),
        name="modulation",
    )(cvec, w_mod, b_mod.reshape(depth, 1, n))


def _norm_mod_store(h_scr, x_ref, nw_ref, sh_ref, sc_ref):
    tm = x_ref.shape[0]
    w = nw_ref[...] * (1.0 + sc_ref[...])
    sh = sh_ref[...]

    def body(r, carry):
        rows = pl.ds(pl.multiple_of(r * NORM_ROWS, NORM_ROWS), NORM_ROWS)
        xf = x_ref[rows, :]
        ms = jnp.mean(xf * xf, axis=-1, keepdims=True)
        h_scr[rows, :] = (xf * lax.rsqrt(ms + RMS_EPS) * w + sh).astype(BF16)
        return carry

    lax.fori_loop(0, tm // NORM_ROWS, body, 0, unroll=4)


def _row_fn(modrow, tm):
    base, tokens_per_row = modrow
    return lambda i: base + (i * tm) // tokens_per_row


def _mod_spec(d_block, chunk, row_fn):
    return pl.BlockSpec((None, 1, d_block), lambda i, j: (row_fn(i) * N_MOD + chunk, 0, 0))


def _norm_mod_rows(h_ref, x_ref, w, sh, chunk=NORM_ROWS):
    for r in range(0, x_ref.shape[0], chunk):
        xf = x_ref[r:r + chunk, :]
        ms = jnp.mean(xf * xf, axis=-1, keepdims=True)
        h_ref[r:r + chunk, :] = (xf * lax.rsqrt(ms + RMS_EPS) * w + sh).astype(BF16)


def _in_proj_kernel(x_ref, sh_ref, sc_ref, nw_ref, w_ref, qw_ref, kw_ref, o_ref, h_scr):
    i = pl.program_id(0)

    @pl.when(i == 0)
    def _():
        h_scr[1] = jnp.zeros(h_scr.shape[1:], h_scr.dtype)

    h = h_scr[(i + 1) % 2]
    for c0 in range(0, w_ref.shape[1], MIX_WIDTH):
        acc = jnp.dot(h, w_ref[:, c0:c0 + MIX_WIDTH], preferred_element_type=F32)
        if c0 < 2 * NA_WIDTH:
            w = qw_ref[...] if c0 < NA_WIDTH else kw_ref[...]
            for hh in range(0, MIX_WIDTH, HEAD_DIM):
                a = acc[:, hh:hh + HEAD_DIM]
                ms = jnp.mean(a * a, axis=-1, keepdims=True)
                o_ref[:, c0 + hh:c0 + hh + HEAD_DIM] = (a * lax.rsqrt(ms + RMS_EPS) * w).astype(o_ref.dtype)
        else:
            o_ref[:, c0:c0 + MIX_WIDTH] = acc.astype(o_ref.dtype)
    _norm_mod_rows(h_scr.at[i % 2], x_ref, nw_ref[...] * (1.0 + sc_ref[...]), sh_ref[...])


def _in_proj(x2, modt, modrow, nw, w_in, qw, kw, tm):
    t, d = x2.shape
    n = w_in.shape[1]
    nt = t // tm
    row_fn = _row_fn(modrow, tm)
    cur = lambda i: jnp.minimum(i, nt - 1)
    mod_spec = lambda chunk: pl.BlockSpec((None, 1, d), lambda i: (row_fn(cur(i)) * N_MOD + chunk, 0, 0))
    return pl.pallas_call(
        _in_proj_kernel,
        grid=(nt + 1,),
        in_specs=[
            pl.BlockSpec((tm, d), lambda i: (cur(i), 0)),
            mod_spec(0),
            mod_spec(1),
            pl.BlockSpec((1, d), lambda i: (0, 0)),
            pl.BlockSpec((d, n), lambda i: (0, 0), pipeline_mode=pl.Buffered(1)),
            pl.BlockSpec((1, HEAD_DIM), lambda i: (0, 0)),
            pl.BlockSpec((1, HEAD_DIM), lambda i: (0, 0)),
        ],
        out_specs=pl.BlockSpec((tm, n), lambda i: (jnp.maximum(i - 1, 0), 0)),
        out_shape=jax.ShapeDtypeStruct((t, n), BF16),
        scratch_shapes=[pltpu.VMEM((2, tm, d), BF16)],
        compiler_params=_params(1),
        name="in_proj",
    )(x2, modt, modt, nw, w_in, qw, kw)


def _na_bias_tables(rpb, rows):
    heads = rpb.shape[0]
    cols = np.arange(GRID_W)
    cs = np.clip(cols - NA_KW // 2, 0, GRID_W - NA_KW)
    valid_c = (cols[None, :] >= cs[:, None]) & (cols[None, :] < cs[:, None] + NA_KW)
    pad = GRID_W - NA_KW
    rp = jnp.pad(rpb.astype(F32), ((0, 0), (0, 0), (pad, pad)))
    col_t = jnp.stack([rp[:, :, GRID_W - 1 - qc:2 * GRID_W - 1 - qc] for qc in range(GRID_W)], axis=2)
    tables = []
    for r0, kr0 in ((0, 0), (2 * NA_QROWS, 2 * NA_QROWS - NA_KH // 2), (rows - NA_QROWS, rows - NA_KROWS)):
        r = r0 + np.arange(NA_QROWS)
        rs = np.clip(r - NA_KH // 2, 0, rows - NA_KH)
        key_row = kr0 + np.arange(NA_KROWS)
        valid_r = (key_row[None, :] >= rs[:, None]) & (key_row[None, :] < rs[:, None] + NA_KH)
        row_off = np.clip(key_row[None, :] - r[:, None] + (NA_KH - 1), 0, 2 * NA_KH - 2)
        tab = jnp.stack([jnp.stack([col_t[:, int(row_off[qr, kr])] for kr in range(NA_KROWS)], axis=2)
                         for qr in range(NA_QROWS)], axis=1)
        valid = valid_r[:, None, :, None] & valid_c[None, :, None, :]
        tab = jnp.where(valid[None], tab, NEG_BIG)
        tables.append(tab.reshape(heads, NA_QROWS * GRID_W, NA_KROWS * GRID_W))
    return jnp.stack(tables)


def _qk(q, k):
    return lax.dot_general(q, k, (((1,), (1,)), ((), ())), preferred_element_type=F32)


def _lane_tiles(a, width=128):
    return [a[:, k:k + width] for k in range(0, a.shape[1], width)]


def _na_kernel(q_ref, k0_ref, k1_ref, k2_ref, v0_ref, v1_ref, v2_ref, kc_ref, vc_ref, bias_ref, o_ref):
    blk = k0_ref.shape[0]
    for h in range(NA_HEADS):
        hs = slice(h * HEAD_DIM, (h + 1) * HEAD_DIM)
        q = q_ref[:, hs]
        s_loc = [_qk(q, kr[:, hs]) + bias_ref[h, :, m * blk:(m + 1) * blk]
                 for m, kr in enumerate((k0_ref, k1_ref, k2_ref))]
        scores = [_qk(q, kc_ref[:, hs])] + s_loc
        values = (vc_ref, v0_ref, v1_ref, v2_ref)
        mx = jnp.max(functools.reduce(jnp.maximum, [t for s in scores for t in _lane_tiles(s)]),
                     axis=-1, keepdims=True)
        probs = [jnp.exp(s - mx) for s in scores]
        den = jnp.sum(functools.reduce(jnp.add, [t for p in probs for t in _lane_tiles(p)]),
                      axis=-1, keepdims=True)
        o = None
        for p, vr in zip(probs, values):
            pv = jnp.dot(p.astype(BF16), vr[:, hs], preferred_element_type=F32)
            o = pv if o is None else o + pv
        o_ref[:, hs] = (o / den).astype(o_ref.dtype)


def _neighbourhood_attention(proj, proj_ctx, bias, batch, seq):
    blk = NA_QROWS * GRID_W
    nrb = seq // blk
    n_kblk = NA_KROWS // NA_QROWS
    ctx_len = proj_ctx.shape[0] // batch

    def kblock(rb):
        return jnp.clip(rb - 1, 0, nrb - n_kblk)

    def kv_spec(m, col):
        return pl.BlockSpec((blk, NA_WIDTH), lambda b, rb: (b * nrb + kblock(rb) + m, col))

    def variant(rb):
        return jnp.where(rb == 0, 0, jnp.where(rb == nrb - 1, 2, 1))

    in_specs = [pl.BlockSpec((blk, NA_WIDTH), lambda b, rb: (b * nrb + rb, 0))]
    in_specs += [kv_spec(m, 1) for m in range(n_kblk)] + [kv_spec(m, 2) for m in range(n_kblk)]
    in_specs += [pl.BlockSpec((ctx_len, NA_WIDTH), lambda b, rb: (b, 1)),
                 pl.BlockSpec((ctx_len, NA_WIDTH), lambda b, rb: (b, 2)),
                 pl.BlockSpec((None, NA_HEADS, blk, NA_KROWS * GRID_W), lambda b, rb: (variant(rb), 0, 0, 0))]
    return pl.pallas_call(
        _na_kernel,
        grid=(batch, nrb),
        in_specs=in_specs,
        out_specs=pl.BlockSpec((blk, NA_WIDTH), lambda b, rb: (b * nrb + rb, 0)),
        out_shape=jax.ShapeDtypeStruct((batch * seq, NA_WIDTH), BF16),
        compiler_params=_params(2),
        name="neighbourhood_attention",
    )(*([proj] * (1 + 2 * n_kblk)), proj_ctx, proj_ctx, bias)


def _ctx_attn_kernel(q_ref, k_ref, v_ref, o_ref):
    for h in range(NA_HEADS):
        hs = slice(h * HEAD_DIM, (h + 1) * HEAD_DIM)
        s = _qk(q_ref[:, hs], k_ref[:, hs])
        p = jnp.exp(s - jnp.max(s, axis=-1, keepdims=True))
        den = jnp.sum(p, axis=-1, keepdims=True)
        o = jnp.dot(p.astype(BF16), v_ref[:, hs], preferred_element_type=F32)
        o_ref[:, hs] = (o / den).astype(o_ref.dtype)


def _context_attention(proj_ctx, batch):
    ctx_len = proj_ctx.shape[0] // batch
    spec = lambda col: pl.BlockSpec((ctx_len, NA_WIDTH), lambda b: (b, col))
    return pl.pallas_call(
        _ctx_attn_kernel,
        grid=(batch,),
        in_specs=[spec(0), spec(1), spec(2)],
        out_specs=spec(0),
        out_shape=jax.ShapeDtypeStruct((proj_ctx.shape[0], NA_WIDTH), BF16),
        compiler_params=_params(1),
        name="context_attention",
    )(proj_ctx, proj_ctx, proj_ctx)


def _dft_cos_sin(n_out, n_in, period):
    ij = (np.arange(n_out, dtype=np.int64)[:, None] * np.arange(n_in, dtype=np.int64)[None, :]) % period
    ang = 2.0 * np.pi * ij.astype(np.float64) / period
    return np.cos(ang), np.sin(ang)


def _ft_channel_kernel(u_ref, m_ref, o_ref):
    m = m_ref[...]
    for g in range(FT_GROUPS):
        z = jnp.dot(u_ref[:, g * GROUP_DIM:(g + 1) * GROUP_DIM], m, preferred_element_type=F32)
        o_ref[:, g * 2 * GROUP_DIM:(g + 1) * 2 * GROUP_DIM] = z.astype(o_ref.dtype)


def _ft_stage_a_kernel(x_ref, m_ref, o_ref):
    c = GROUP_DIM
    for k in range(x_ref.shape[1]):
        xb = x_ref[:, k, :]
        xr, xi = xb[:, :c], xb[:, c:]
        rhs = jnp.concatenate([jnp.concatenate([xr, xi], axis=1), jnp.concatenate([-xi, xr], axis=1)], axis=0)
        o_ref[:, k, :] = jnp.dot(m_ref[k], rhs.astype(BF16), preferred_element_type=F32)


def _ft_stage_b_kernel(x_ref, m_ref, o_ref):
    c = GROUP_DIM
    m = m_ref[...]
    for k in range(x_ref.shape[0]):
        y = x_ref[k]
        rhs = jnp.concatenate([y[:, :c], y[:, c:]], axis=0).astype(BF16)
        o_ref[:, k, :] = jnp.dot(m, rhs, preferred_element_type=F32)


def _fourier_mix(proj, batch, seq):
    t = batch * seq
    n_b = GROUP_DIM
    n_a = seq // n_b
    g, c = FT_GROUPS, GROUP_DIM
    sub = 8

    cc, sc = _dft_cos_sin(c, c, c)
    m_c = jnp.asarray(np.concatenate([cc, -sc], axis=1), BF16)
    ca, sa = _dft_cos_sin(n_a, seq, seq)
    ca, sa = ca.reshape(n_a, n_a, n_b), sa.reshape(n_a, n_a, n_b)
    m_a = jnp.asarray(np.concatenate([ca, -sa], axis=1).transpose(2, 0, 1), BF16)
    cb, sb = _dft_cos_sin(n_b, n_b, n_b)
    m_b = jnp.asarray(np.concatenate([cb, sb], axis=1) / math.sqrt(seq * c), BF16)

    tm = 2048
    z = pl.pallas_call(
        _ft_channel_kernel,
        grid=(t // tm,),
        in_specs=[pl.BlockSpec((tm, MIX_WIDTH), lambda i: (i, COL_FT)),
                  pl.BlockSpec((c, 2 * c), lambda i: (0, 0))],
        out_specs=pl.BlockSpec((tm, 2 * g * c), lambda i: (i, 0)),
        out_shape=jax.ShapeDtypeStruct((t, 2 * g * c), F32),
        compiler_params=_params(1),
        name="fourier_channels",
    )(proj, m_c)

    y = pl.pallas_call(
        _ft_stage_a_kernel,
        grid=(n_b // sub, batch, g),
        in_specs=[pl.BlockSpec((None, n_a, sub, 2 * c), lambda j, b, k: (b, 0, j, k)),
                  pl.BlockSpec((sub, n_a, 2 * n_a), lambda j, b, k: (j, 0, 0))],
        out_specs=pl.BlockSpec((None, n_a, sub, 2 * c), lambda j, b, k: (b, 0, j, k)),
        out_shape=jax.ShapeDtypeStruct((batch, n_a, n_b, 2 * g * c), F32),
        compiler_params=_params(3),
        name="fourier_stage_a",
    )(z.reshape(batch, n_a, n_b, 2 * g * c), m_a)

    o = pl.pallas_call(
        _ft_stage_b_kernel,
        grid=(batch, g, n_a // sub),
        in_specs=[pl.BlockSpec((None, sub, n_b, 2 * c), lambda b, k, j: (b, j, 0, k)),
                  pl.BlockSpec((n_b, 2 * n_b), lambda b, k, j: (0, 0))],
        out_specs=pl.BlockSpec((None, n_b, sub, c), lambda b, k, j: (b, 0, j, k)),
        out_shape=jax.ShapeDtypeStruct((batch, n_b, n_a, g * c), F32),
        compiler_params=_params(3),
        name="fourier_stage_b",
    )(y, m_b)
    return o.reshape(t, g * c)


def _ctx_fourier_kernel(u_ref, mc_ref, ml_ref, o_ref):
    mc = mc_ref[...]
    ml = ml_ref[...]
    for g in range(FT_GROUPS):
        z = jnp.dot(u_ref[:, g * GROUP_DIM:(g + 1) * GROUP_DIM], mc, preferred_element_type=F32)
        stack = jnp.concatenate([z[:, :GROUP_DIM], z[:, GROUP_DIM:]], axis=0).astype(BF16)
        o_ref[:, g * GROUP_DIM:(g + 1) * GROUP_DIM] = jnp.dot(ml, stack, preferred_element_type=F32).astype(o_ref.dtype)


def _context_fourier(proj_ctx, batch):
    ctx_len = proj_ctx.shape[0] // batch
    c = GROUP_DIM
    cc, sc = _dft_cos_sin(c, c, c)
    m_c = jnp.asarray(np.concatenate([cc, -sc], axis=1), BF16)
    cl, sl = _dft_cos_sin(ctx_len, ctx_len, ctx_len)
    m_l = jnp.asarray(np.concatenate([cl, sl], axis=1) / math.sqrt(ctx_len * c), BF16)
    return pl.pallas_call(
        _ctx_fourier_kernel,
        grid=(batch,),
        in_specs=[pl.BlockSpec((ctx_len, MIX_WIDTH), lambda b: (b, COL_FT)),
                  pl.BlockSpec((c, 2 * c), lambda b: (0, 0)),
                  pl.BlockSpec((ctx_len, 2 * ctx_len), lambda b: (0, 0))],
        out_specs=pl.BlockSpec((ctx_len, MIX_WIDTH), lambda b: (b, 0)),
        out_shape=jax.ShapeDtypeStruct((proj_ctx.shape[0], MIX_WIDTH), BF16),
        compiler_params=_params(1),
        name="context_fourier",
    )(proj_ctx, m_c, m_l)


def _merge_kernel(x_ref, sh_ref, sc_ref, nw_ref,
                  pool_ref, pool_p_ref, pool_n_ref, sv_ref, sv_p_ref, sv_n_ref, scv_ref, scv_p_ref, scv_n_ref,
                  sb_ref, attn_ref, four_ref, wg_ref, wpa_ref, wpb_ref, wpc_ref, wpd_ref,
                  wpool_ref, pscale_ref, convw_ref, o_ref,
                  h_scr, brc_scr, brd_scr, ext_scr, *, tiles_per_seq, seq_len):
    i = pl.program_id(0)
    j = pl.program_id(1)
    tm = x_ref.shape[0]

    @pl.when(j == 0)
    def _():
        _norm_mod_store(h_scr, x_ref, nw_ref, sh_ref, sc_ref)
        first = (i % tiles_per_seq) == 0
        last = (i % tiles_per_seq) == tiles_per_seq - 1
        pos = (i % tiles_per_seq) * tm + lax.broadcasted_iota(jnp.int32, (tm, 1), 0)

        def fill_ext(prev, main, nxt):
            ext_scr[0:HALO, :] = jnp.where(first, 0.0, prev)
            ext_scr[HALO:HALO + tm, :] = main
            ext_scr[HALO + tm:HALO + tm + HALO, :] = jnp.where(last, 0.0, nxt)

        fill_ext(pool_p_ref[...].astype(F32), pool_ref[...].astype(F32), pool_n_ref[...].astype(F32))
        for g, w in enumerate(POOL_WINDOWS):
            cs = slice(g * GROUP_DIM, (g + 1) * GROUP_DIM)
            win = ext_scr[HALO - w // 2:HALO - w // 2 + tm, cs]
            for dlt in range(-w // 2 + 1, w // 2):
                win = win + ext_scr[HALO + dlt:HALO + dlt + tm, cs]
            cnt = (jnp.minimum(pos + w // 2, seq_len) - jnp.maximum(pos - w // 2, 0)).astype(F32)
            p = win / cnt - ext_scr[HALO:HALO + tm, cs]
            yg = jnp.dot(p.astype(BF16), wpool_ref[g], preferred_element_type=F32) * pscale_ref[:, cs]
            brc_scr[:, cs] = yg.astype(BF16)

        fill_ext(sv_p_ref[...].astype(F32) * scv_p_ref[...].astype(F32),
                 sv_ref[...].astype(F32) * scv_ref[...].astype(F32),
                 sv_n_ref[...].astype(F32) * scv_n_ref[...].astype(F32))
        y = (ext_scr[HALO - 1:HALO - 1 + tm, :] * convw_ref[0:1, :]
             + ext_scr[HALO:HALO + tm, :] * convw_ref[1:2, :]
             + ext_scr[HALO + 1:HALO + 1 + tm, :] * convw_ref[2:3, :])
        brd_scr[...] = (sb_ref[...].astype(F32) * y).astype(BF16)

    h = h_scr[...]
    branches = ((attn_ref, wpa_ref), (four_ref, wpb_ref), (brc_scr, wpc_ref), (brd_scr, wpd_ref))
    y = None
    for k, (br_ref, wp_ref) in enumerate(branches):
        gate = jax.nn.sigmoid(jnp.dot(h, wg_ref[k], preferred_element_type=F32))
        term = gate * jnp.dot(br_ref[...].astype(BF16), wp_ref[...], preferred_element_type=F32)
        y = term if y is None else y + term
    o_ref[...] = y.astype(o_ref.dtype)


def _merge(x2, modt, modrow, nw, proj, attn, four, wg, wpa, wpb, wpc, wpd, wpool, pscale, convw, seq_len, tm, tn):
    t, d = x2.shape
    row_fn = _row_fn(modrow, tm)
    tiles_per_seq = seq_len // tm
    n_halo = t // HALO

    def halo_specs(col):
        return [pl.BlockSpec((tm, MIX_WIDTH), lambda i, j: (i, col)),
                pl.BlockSpec((HALO, MIX_WIDTH), lambda i, j: (jnp.maximum(i * (tm // HALO) - 1, 0), col)),
                pl.BlockSpec((HALO, MIX_WIDTH), lambda i, j: (jnp.minimum((i + 1) * (tm // HALO), n_halo - 1), col))]

    in_specs = [pl.BlockSpec((tm, d), lambda i, j: (i, 0)),
                _mod_spec(d, 0, row_fn), _mod_spec(d, 1, row_fn),
                pl.BlockSpec((1, d), lambda i, j: (0, 0))]
    in_specs += halo_specs(COL_POOL) + halo_specs(COL_SV) + halo_specs(COL_SC)
    in_specs += [pl.BlockSpec((tm, MIX_WIDTH), lambda i, j: (i, COL_SB)),
                 pl.BlockSpec((tm, NA_WIDTH), lambda i, j: (i, 0)),
                 pl.BlockSpec((tm, MIX_WIDTH), lambda i, j: (i, 0)),
                 pl.BlockSpec((4, d, tn), lambda i, j: (0, 0, j)),
                 pl.BlockSpec((NA_WIDTH, tn), lambda i, j: (0, j)),
                 pl.BlockSpec((MIX_WIDTH, tn), lambda i, j: (0, j)),
                 pl.BlockSpec((MIX_WIDTH, tn), lambda i, j: (0, j)),
                 pl.BlockSpec((MIX_WIDTH, tn), lambda i, j: (0, j)),
                 pl.BlockSpec((len(POOL_WINDOWS), GROUP_DIM, GROUP_DIM), lambda i, j: (0, 0, 0)),
                 pl.BlockSpec((1, MIX_WIDTH), lambda i, j: (0, 0)),
                 pl.BlockSpec((3, MIX_WIDTH), lambda i, j: (0, 0))]
    return pl.pallas_call(
        functools.partial(_merge_kernel, tiles_per_seq=tiles_per_seq, seq_len=seq_len),
        grid=(t // tm, d // tn),
        in_specs=in_specs,
        out_specs=pl.BlockSpec((tm, tn), lambda i, j: (i, j)),
        out_shape=jax.ShapeDtypeStruct((t, d), BF16),
        scratch_shapes=[pltpu.VMEM((tm, d), BF16), pltpu.VMEM((tm, MIX_WIDTH), BF16),
                        pltpu.VMEM((tm, MIX_WIDTH), BF16), pltpu.VMEM((tm + 2 * HALO, MIX_WIDTH), F32)],
        compiler_params=_params(2),
        name="gated_merge",
    )(x2, modt, modt, nw, *([proj] * 10), attn, four, wg, wpa, wpb, wpc, wpd, wpool, pscale, convw)


def _resid_proj_kernel(a_ref, w_ref, x_ref, g_ref, o_ref):
    o_ref[...] = x_ref[...] + g_ref[...] * jnp.dot(a_ref[...], w_ref[...], preferred_element_type=F32)


def _resid_proj(a, w, x2, modt, chunk, modrow, tm, tn):
    t, k = a.shape
    row_fn = _row_fn(modrow, tm)
    d = w.shape[1]
    return pl.pallas_call(
        _resid_proj_kernel,
        grid=(d // tn, t // tm),
        in_specs=[pl.BlockSpec((tm, k), lambda j, i: (i, 0)),
                  pl.BlockSpec((k, tn), lambda j, i: (0, j)),
                  pl.BlockSpec((tm, tn), lambda j, i: (i, j)),
                  pl.BlockSpec((None, 1, tn), lambda j, i: (row_fn(i) * N_MOD + chunk, 0, j))],
        out_specs=pl.BlockSpec((tm, tn), lambda j, i: (i, j)),
        out_shape=jax.ShapeDtypeStruct((t, d), F32),
        compiler_params=_params(2),
        name="residual_projection",
    )(a, w, x2, modt)


def _ffn_up_kernel(x_ref, sh_ref, sc_ref, nw_ref, wg_ref, wu_ref, o_ref, h_scr):
    @pl.when(pl.program_id(1) == 0)
    def _():
        _norm_mod_store(h_scr, x_ref, nw_ref, sh_ref, sc_ref)

    h = h_scr[...]
    gt = jnp.dot(h, wg_ref[...], preferred_element_type=F32)
    up = jnp.dot(h, wu_ref[...], preferred_element_type=F32)
    o_ref[...] = (gt * jax.nn.sigmoid(gt) * up).astype(o_ref.dtype)


def _ffn_up(x2, modt, modrow, nw, w_g, w_u, tm, tn):
    t, d = x2.shape
    row_fn = _row_fn(modrow, tm)
    n = w_g.shape[1]
    return pl.pallas_call(
        _ffn_up_kernel,
        grid=(t // tm, n // tn),
        in_specs=[pl.BlockSpec((tm, d), lambda i, j: (i, 0)),
                  _mod_spec(d, 3, row_fn), _mod_spec(d, 4, row_fn),
                  pl.BlockSpec((1, d), lambda i, j: (0, 0)),
                  pl.BlockSpec((d, tn), lambda i, j: (0, j)),
                  pl.BlockSpec((d, tn), lambda i, j: (0, j))],
        out_specs=pl.BlockSpec((tm, tn), lambda i, j: (i, j)),
        out_shape=jax.ShapeDtypeStruct((t, n), BF16),
        scratch_shapes=[pltpu.VMEM((tm, d), BF16)],
        compiler_params=_params(2),
        name="ffn_up",
    )(x2, modt, modt, nw, w_g, w_u)


def _stream_layer(x2, modt, modrow, seq_len, tm, lw, mixers):
    proj = _in_proj(x2, modt, modrow, lw["norm1"], lw["w_in"], lw["qw"], lw["kw"], min(tm, 512))
    attn, four = mixers(proj)
    y = _merge(x2, modt, modrow, lw["norm1"], proj, attn, four, lw["w_gate"], lw["w_pa"], lw["w_pb"], lw["w_pc"],
               lw["w_pd"], lw["w_pool"], lw["pool_scale"], lw["conv_w"], seq_len, min(tm, 512), 512)
    x2 = _resid_proj(y, lw["w_o"], x2, modt, 2, modrow, min(tm, 512), x2.shape[1])
    a = _ffn_up(x2, modt, modrow, lw["norm2"], lw["w_ffn_gate"], lw["w_ffn_up"], tm, 512)
    return _resid_proj(a, lw["w_ffn_down"], x2, modt, 5, modrow, min(tm, 512), min(x2.shape[1], 1024))


def kernel(x, c, ctx, c_ctx, w_mod, b_mod, norm1_w, w_in, q_norm_w, k_norm_w, rpb, w_pool, pool_scale, conv_w,
           w_gate, w_pa, w_pb, w_pc, w_pd, w_o, norm2_w, w_ffn_gate, w_ffn_up, w_ffn_down):
    batch, seq, d = x.shape
    ctx_len = ctx.shape[1]
    depth = w_mod.shape[0]
    rows = seq // GRID_W
    assert seq % (NA_QROWS * GRID_W) == 0 and rows >= NA_KROWS and batch + 1 <= MOD_ROWS

    cvec = jnp.zeros((MOD_ROWS, d), F32).at[:batch].set(c).at[batch].set(c_ctx)
    mod = _modulation(cvec, w_mod, b_mod)

    tm_lat = 1024
    lat_row = (0, seq)
    ctx_row = (batch, batch * ctx_len)

    x2 = x.reshape(batch * seq, d)
    xc = ctx.reshape(batch * ctx_len, d)
    scale = HEAD_DIM ** -0.5
    for l in range(depth):
        lw = {
            "norm1": norm1_w[l].reshape(1, d), "norm2": norm2_w[l].reshape(1, d),
            "w_in": w_in[l].astype(BF16),
            "qw": (q_norm_w[l] * scale).reshape(1, HEAD_DIM), "kw": k_norm_w[l].reshape(1, HEAD_DIM),
            "w_gate": w_gate[l].astype(BF16), "w_pa": w_pa[l].astype(BF16), "w_pb": w_pb[l].astype(BF16),
            "w_pc": w_pc[l].astype(BF16), "w_pd": w_pd[l].astype(BF16), "w_o": w_o[l].astype(BF16),
            "w_pool": w_pool[l].astype(BF16), "pool_scale": pool_scale[l].reshape(1, MIX_WIDTH),
            "conv_w": conv_w[l],
            "w_ffn_gate": w_ffn_gate[l].astype(BF16), "w_ffn_up": w_ffn_up[l].astype(BF16),
            "w_ffn_down": w_ffn_down[l].astype(BF16),
        }
        modt = mod[l].reshape(MOD_ROWS * N_MOD, 1, d)
        bias = _na_bias_tables(rpb[l], rows)

        if l == depth - 1:
            proj_c = _in_proj(xc, modt, ctx_row, lw["norm1"], lw["w_in"], lw["qw"], lw["kw"], ctx_len)
        else:
            holder = {}

            def ctx_mixers(proj):
                holder["proj"] = proj
                return _context_attention(proj, batch), _context_fourier(proj, batch)

            xc = _stream_layer(xc, modt, ctx_row, ctx_len, ctx_len, lw, ctx_mixers)
            proj_c = holder["proj"]

        def lat_mixers(proj):
            return (_neighbourhood_attention(proj, proj_c, bias, batch, seq),
                    _fourier_mix(proj, batch, seq))

        x2 = _stream_layer(x2, modt, lat_row, seq, tm_lat, lw, lat_mixers)
    return x2.reshape(batch, seq, d)
```

```python
import functools
import math

import numpy as np
import jax
import jax.numpy as jnp
from jax import lax
from jax.experimental import pallas as pl
from jax.experimental.pallas import tpu as pltpu

F32 = jnp.float32
BF16 = jnp.bfloat16

GRID_W = 64
NA_HEADS = 8
HEAD_DIM = 128
NA_WIDTH = NA_HEADS * HEAD_DIM
NA_KH = 8
NA_KW = 16
NA_QROWS = 4
NA_KROWS = 12
GROUP_DIM = 128
FT_GROUPS = 4
POOL_WINDOWS = (2, 4, 8, 16)
MIX_WIDTH = 512
N_MOD = 6
RMS_EPS = 1e-6
BF16_ROWS = 16
HALO = 64
NORM_ROWS = 2 * BF16_ROWS
MOD_ROWS = 8
NEG_BIG = -1e30

COL_FT, COL_POOL, COL_SV, COL_SB, COL_SC = 6, 7, 8, 9, 10

VMEM_LIMIT = 56 * 1024 * 1024


def _params(n_axes):
    return pltpu.CompilerParams(dimension_semantics=("arbitrary",) * n_axes,
                                vmem_limit_bytes=VMEM_LIMIT)


def _mod_kernel(c_ref, w_ref, b_ref, o_ref):
    cv = c_ref[...]
    s = cv * jax.nn.sigmoid(cv)
    o_ref[...] = jnp.dot(s.astype(BF16), w_ref[...].astype(BF16),
                         preferred_element_type=F32) + b_ref[...]


def _modulation(cvec, w_mod, b_mod):
    depth, d, n = w_mod.shape
    tn = 1024
    return pl.pallas_call(
        _mod_kernel,
        grid=(depth, n // tn),
        in_specs=[
            pl.BlockSpec((MOD_ROWS, d), lambda l, j: (0, 0)),
            pl.BlockSpec((None, d, tn), lambda l, j: (l, 0, j)),
            pl.BlockSpec((None, 1, tn), lambda l, j: (l, 0, j)),
        ],
        out_specs=pl.BlockSpec((None, MOD_ROWS, tn), lambda l, j: (l, 0, j)),
        out_shape=jax.ShapeDtypeStruct((depth, MOD_ROWS, n), F32),
        compiler_params=_params(2),
        name="modulation",
    )(cvec, w_mod, b_mod.reshape(depth, 1, n))


def _row_fn(modrow, tm):
    base, tokens_per_row = modrow
    return lambda i: base + (i * tm) // tokens_per_row


def _norm_mod_rows(h_ref, x_ref, w, sh, chunk=NORM_ROWS):
    for r in range(0, x_ref.shape[0], chunk):
        xf = x_ref[r:r + chunk, :]
        ms = jnp.mean(xf * xf, axis=-1, keepdims=True)
        h_ref[r:r + chunk, :] = (xf * lax.rsqrt(ms + RMS_EPS) * w + sh).astype(BF16)


def _in_proj_kernel(x_ref, sh_ref, sc_ref, nw_ref, w_ref, qw_ref, kw_ref, o_ref, ho_ref, h_scr):
    i = pl.program_id(0)

    @pl.when(i == 0)
    def _():
        h_scr[1] = jnp.zeros(h_scr.shape[1:], h_scr.dtype)

    h = h_scr[(i + 1) % 2]
    ho_ref[...] = h
    for c0 in range(0, w_ref.shape[1], MIX_WIDTH):
        acc = jnp.dot(h, w_ref[:, c0:c0 + MIX_WIDTH], preferred_element_type=F32)
        if c0 < 2 * NA_WIDTH:
            w = qw_ref[...] if c0 < NA_WIDTH else kw_ref[...]
            for hh in range(0, MIX_WIDTH, HEAD_DIM):
                a = acc[:, hh:hh + HEAD_DIM]
                ms = jnp.mean(a * a, axis=-1, keepdims=True)
                o_ref[:, c0 + hh:c0 + hh + HEAD_DIM] = (a * lax.rsqrt(ms + RMS_EPS) * w).astype(o_ref.dtype)
        else:
            o_ref[:, c0:c0 + MIX_WIDTH] = acc.astype(o_ref.dtype)
    _norm_mod_rows(h_scr.at[i % 2], x_ref, nw_ref[...] * (1.0 + sc_ref[...]), sh_ref[...])


def _in_proj(x2, modt, modrow, nw, w_in, qw, kw, tm):
    t, d = x2.shape
    n = w_in.shape[1]
    nt = t // tm
    row_fn = _row_fn(modrow, tm)
    cur = lambda i: jnp.minimum(i, nt - 1)
    mod_spec = lambda chunk: pl.BlockSpec((None, 1, d), lambda i: (row_fn(cur(i)) * N_MOD + chunk, 0, 0))
    return pl.pallas_call(
        _in_proj_kernel,
        grid=(nt + 1,),
        in_specs=[
            pl.BlockSpec((tm, d), lambda i: (cur(i), 0)),
            mod_spec(0),
            mod_spec(1),
            pl.BlockSpec((1, d), lambda i: (0, 0)),
            pl.BlockSpec((d, n), lambda i: (0, 0), pipeline_mode=pl.Buffered(1)),
            pl.BlockSpec((1, HEAD_DIM), lambda i: (0, 0)),
            pl.BlockSpec((1, HEAD_DIM), lambda i: (0, 0)),
        ],
        out_specs=[pl.BlockSpec((tm, n), lambda i: (jnp.maximum(i - 1, 0), 0)),
                   pl.BlockSpec((tm, d), lambda i: (jnp.maximum(i - 1, 0), 0))],
        out_shape=[jax.ShapeDtypeStruct((t, n), BF16), jax.ShapeDtypeStruct((t, d), BF16)],
        scratch_shapes=[pltpu.VMEM((2, tm, d), BF16)],
        compiler_params=_params(1),
        name="in_proj",
    )(x2, modt, modt, nw, w_in, qw, kw)


def _na_bias_tables(rpb, rows):
    heads = rpb.shape[0]
    cols = np.arange(GRID_W)
    cs = np.clip(cols - NA_KW // 2, 0, GRID_W - NA_KW)
    valid_c = (cols[None, :] >= cs[:, None]) & (cols[None, :] < cs[:, None] + NA_KW)
    pad = GRID_W - NA_KW
    rp = jnp.pad(rpb.astype(F32), ((0, 0), (0, 0), (pad, pad)))
    col_t = jnp.stack([rp[:, :, GRID_W - 1 - qc:2 * GRID_W - 1 - qc] for qc in range(GRID_W)], axis=2)
    tables = []
    for r0, kr0 in ((0, 0), (2 * NA_QROWS, 2 * NA_QROWS - NA_KH // 2), (rows - NA_QROWS, rows - NA_KROWS)):
        r = r0 + np.arange(NA_QROWS)
        rs = np.clip(r - NA_KH // 2, 0, rows - NA_KH)
        key_row = kr0 + np.arange(NA_KROWS)
        valid_r = (key_row[None, :] >= rs[:, None]) & (key_row[None, :] < rs[:, None] + NA_KH)
        row_off = np.clip(key_row[None, :] - r[:, None] + (NA_KH - 1), 0, 2 * NA_KH - 2)
        tab = jnp.stack([jnp.stack([col_t[:, int(row_off[qr, kr])] for kr in range(NA_KROWS)], axis=2)
                         for qr in range(NA_QROWS)], axis=1)
        valid = valid_r[:, None, :, None] & valid_c[None, :, None, :]
        tab = jnp.where(valid[None], tab, NEG_BIG)
        tables.append(tab.reshape(heads, NA_QROWS * GRID_W, NA_KROWS * GRID_W))
    return jnp.stack(tables)


def _qk(q, k):
    return lax.dot_general(q, k, (((1,), (1,)), ((), ())), preferred_element_type=F32)


def _lane_tiles(a, width=128):
    return [a[:, k:k + width] for k in range(0, a.shape[1], width)]


def _na_kernel(q_ref, k0_ref, k1_ref, k2_ref, v0_ref, v1_ref, v2_ref, kc_ref, vc_ref, bias_ref, o_ref):
    blk = k0_ref.shape[0]
    for h in range(NA_HEADS):
        hs = slice(h * HEAD_DIM, (h + 1) * HEAD_DIM)
        q = q_ref[:, hs]
        s_loc = [_qk(q, kr[:, hs]) + bias_ref[h, :, m * blk:(m + 1) * blk]
                 for m, kr in enumerate((k0_ref, k1_ref, k2_ref))]
        scores = [_qk(q, kc_ref[:, hs])] + s_loc
        values = (vc_ref, v0_ref, v1_ref, v2_ref)
        mx = jnp.max(functools.reduce(jnp.maximum, [t for s in scores for t in _lane_tiles(s)]),
                     axis=-1, keepdims=True)
        probs = [jnp.exp(s - mx) for s in scores]
        den = jnp.sum(functools.reduce(jnp.add, [t for p in probs for t in _lane_tiles(p)]),
                      axis=-1, keepdims=True)
        o = None
        for p, vr in zip(probs, values):
            pv = jnp.dot(p.astype(BF16), vr[:, hs], preferred_element_type=F32)
            o = pv if o is None else o + pv
        o_ref[:, hs] = (o / den).astype(o_ref.dtype)


def _neighbourhood_attention(proj, proj_ctx, bias, batch, seq):
    blk = NA_QROWS * GRID_W
    nrb = seq // blk
    n_kblk = NA_KROWS // NA_QROWS
    ctx_len = proj_ctx.shape[0] // batch

    def kblock(rb):
        return jnp.clip(rb - 1, 0, nrb - n_kblk)

    def kv_spec(m, col):
        return pl.BlockSpec((blk, NA_WIDTH), lambda b, rb: (b * nrb + kblock(rb) + m, col))

    def variant(rb):
        return jnp.where(rb == 0, 0, jnp.where(rb == nrb - 1, 2, 1))

    in_specs = [pl.BlockSpec((blk, NA_WIDTH), lambda b, rb: (b * nrb + rb, 0))]
    in_specs += [kv_spec(m, 1) for m in range(n_kblk)] + [kv_spec(m, 2) for m in range(n_kblk)]
    in_specs += [pl.BlockSpec((ctx_len, NA_WIDTH), lambda b, rb: (b, 1)),
                 pl.BlockSpec((ctx_len, NA_WIDTH), lambda b, rb: (b, 2)),
                 pl.BlockSpec((None, NA_HEADS, blk, NA_KROWS * GRID_W), lambda b, rb: (variant(rb), 0, 0, 0))]
    return pl.pallas_call(
        _na_kernel,
        grid=(batch, nrb),
        in_specs=in_specs,
        out_specs=pl.BlockSpec((blk, NA_WIDTH), lambda b, rb: (b * nrb + rb, 0)),
        out_shape=jax.ShapeDtypeStruct((batch * seq, NA_WIDTH), BF16),
        compiler_params=_params(2),
        name="neighbourhood_attention",
    )(*([proj] * (1 + 2 * n_kblk)), proj_ctx, proj_ctx, bias)


def _ctx_attn_kernel(q_ref, k_ref, v_ref, o_ref):
    for h in range(NA_HEADS):
        hs = slice(h * HEAD_DIM, (h + 1) * HEAD_DIM)
        s = _qk(q_ref[:, hs], k_ref[:, hs])
        p = jnp.exp(s - jnp.max(s, axis=-1, keepdims=True))
        den = jnp.sum(p, axis=-1, keepdims=True)
        o = jnp.dot(p.astype(BF16), v_ref[:, hs], preferred_element_type=F32)
        o_ref[:, hs] = (o / den).astype(o_ref.dtype)


def _context_attention(proj_ctx, batch):
    ctx_len = proj_ctx.shape[0] // batch
    spec = lambda col: pl.BlockSpec((ctx_len, NA_WIDTH), lambda b: (b, col))
    return pl.pallas_call(
        _ctx_attn_kernel,
        grid=(batch,),
        in_specs=[spec(0), spec(1), spec(2)],
        out_specs=spec(0),
        out_shape=jax.ShapeDtypeStruct((proj_ctx.shape[0], NA_WIDTH), BF16),
        compiler_params=_params(1),
        name="context_attention",
    )(proj_ctx, proj_ctx, proj_ctx)


def _dft_cos_sin(n_out, n_in, period):
    ij = (np.arange(n_out, dtype=np.int64)[:, None] * np.arange(n_in, dtype=np.int64)[None, :]) % period
    ang = 2.0 * np.pi * ij.astype(np.float64) / period
    return np.cos(ang), np.sin(ang)


def _ft_channel_kernel(u_ref, m_ref, o_ref):
    m = m_ref[...]
    for g in range(FT_GROUPS):
        z = jnp.dot(u_ref[:, g * GROUP_DIM:(g + 1) * GROUP_DIM], m, preferred_element_type=F32)
        o_ref[:, g * 2 * GROUP_DIM:(g + 1) * 2 * GROUP_DIM] = z.astype(o_ref.dtype)


def _ft_stage_a_kernel(x_ref, m_ref, o_ref):
    c = GROUP_DIM
    for k in range(x_ref.shape[1]):
        xb = x_ref[:, k, :]
        xr, xi = xb[:, :c], xb[:, c:]
        rhs = jnp.concatenate([jnp.concatenate([xr, xi], axis=1), jnp.concatenate([-xi, xr], axis=1)], axis=0)
        o_ref[:, k, :] = jnp.dot(m_ref[k], rhs.astype(BF16), preferred_element_type=F32)


def _ft_stage_b_kernel(x_ref, m_ref, o_ref):
    c = GROUP_DIM
    m = m_ref[...]
    for k in range(x_ref.shape[0]):
        y = x_ref[k]
        rhs = jnp.concatenate([y[:, :c], y[:, c:]], axis=0).astype(BF16)
        o_ref[:, k, :] = jnp.dot(m, rhs, preferred_element_type=F32)


def _fourier_mix(proj, batch, seq):
    t = batch * seq
    n_b = GROUP_DIM
    n_a = seq // n_b
    g, c = FT_GROUPS, GROUP_DIM
    sub = 8

    cc, sc = _dft_cos_sin(c, c, c)
    m_c = jnp.asarray(np.concatenate([cc, -sc], axis=1), BF16)
    ca, sa = _dft_cos_sin(n_a, seq, seq)
    ca, sa = ca.reshape(n_a, n_a, n_b), sa.reshape(n_a, n_a, n_b)
    m_a = jnp.asarray(np.concatenate([ca, -sa], axis=1).transpose(2, 0, 1), BF16)
    cb, sb = _dft_cos_sin(n_b, n_b, n_b)
    m_b = jnp.asarray(np.concatenate([cb, sb], axis=1) / math.sqrt(seq * c), BF16)

    tm = 2048
    z = pl.pallas_call(
        _ft_channel_kernel,
        grid=(t // tm,),
        in_specs=[pl.BlockSpec((tm, MIX_WIDTH), lambda i: (i, COL_FT)),
                  pl.BlockSpec((c, 2 * c), lambda i: (0, 0))],
        out_specs=pl.BlockSpec((tm, 2 * g * c), lambda i: (i, 0)),
        out_shape=jax.ShapeDtypeStruct((t, 2 * g * c), F32),
        compiler_params=_params(1),
        name="fourier_channels",
    )(proj, m_c)

    y = pl.pallas_call(
        _ft_stage_a_kernel,
        grid=(n_b // sub, batch, g),
        in_specs=[pl.BlockSpec((None, n_a, sub, 2 * c), lambda j, b, k: (b, 0, j, k)),
                  pl.BlockSpec((sub, n_a, 2 * n_a), lambda j, b, k: (j, 0, 0))],
        out_specs=pl.BlockSpec((None, n_a, sub, 2 * c), lambda j, b, k: (b, 0, j, k)),
        out_shape=jax.ShapeDtypeStruct((batch, n_a, n_b, 2 * g * c), F32),
        compiler_params=_params(3),
        name="fourier_stage_a",
    )(z.reshape(batch, n_a, n_b, 2 * g * c), m_a)

    o = pl.pallas_call(
        _ft_stage_b_kernel,
        grid=(batch, g, n_a // sub),
        in_specs=[pl.BlockSpec((None, sub, n_b, 2 * c), lambda b, k, j: (b, j, 0, k)),
                  pl.BlockSpec((n_b, 2 * n_b), lambda b, k, j: (0, 0))],
        out_specs=pl.BlockSpec((None, n_b, sub, c), lambda b, k, j: (b, 0, j, k)),
        out_shape=jax.ShapeDtypeStruct((batch, n_b, n_a, g * c), F32),
        compiler_params=_params(3),
        name="fourier_stage_b",
    )(y, m_b)
    return o.reshape(t, g * c)


def _ctx_fourier_kernel(u_ref, mc_ref, ml_ref, o_ref):
    mc = mc_ref[...]
    ml = ml_ref[...]
    for g in range(FT_GROUPS):
        z = jnp.dot(u_ref[:, g * GROUP_DIM:(g + 1) * GROUP_DIM], mc, preferred_element_type=F32)
        stack = jnp.concatenate([z[:, :GROUP_DIM], z[:, GROUP_DIM:]], axis=0).astype(BF16)
        o_ref[:, g * GROUP_DIM:(g + 1) * GROUP_DIM] = jnp.dot(ml, stack, preferred_element_type=F32).astype(o_ref.dtype)


def _context_fourier(proj_ctx, batch):
    ctx_len = proj_ctx.shape[0] // batch
    c = GROUP_DIM
    cc, sc = _dft_cos_sin(c, c, c)
    m_c = jnp.asarray(np.concatenate([cc, -sc], axis=1), BF16)
    cl, sl = _dft_cos_sin(ctx_len, ctx_len, ctx_len)
    m_l = jnp.asarray(np.concatenate([cl, sl], axis=1) / math.sqrt(ctx_len * c), BF16)
    return pl.pallas_call(
        _ctx_fourier_kernel,
        grid=(batch,),
        in_specs=[pl.BlockSpec((ctx_len, MIX_WIDTH), lambda b: (b, COL_FT)),
                  pl.BlockSpec((c, 2 * c), lambda b: (0, 0)),
                  pl.BlockSpec((ctx_len, 2 * ctx_len), lambda b: (0, 0))],
        out_specs=pl.BlockSpec((ctx_len, MIX_WIDTH), lambda b: (b, 0)),
        out_shape=jax.ShapeDtypeStruct((proj_ctx.shape[0], MIX_WIDTH), BF16),
        compiler_params=_params(1),
        name="context_fourier",
    )(proj_ctx, m_c, m_l)


def _merge_kernel(h_ref, pool_ref, pool_p_ref, pool_n_ref, sv_ref, sv_p_ref, sv_n_ref, scv_ref, scv_p_ref, scv_n_ref,
                  sb_ref, attn_ref, four_ref, wg_ref, wpa_ref, wpb_ref, wpc_ref, wpd_ref,
                  band_ref, wpool_ref, pscale_ref, convw_ref, o_ref,
                  brc_scr, brd_scr, ext_scr, *, tiles_per_seq, seq_len):
    i = pl.program_id(0)
    j = pl.program_id(1)
    tm = h_ref.shape[0]

    @pl.when(j == 0)
    def _():
        first = (i % tiles_per_seq) == 0
        last = (i % tiles_per_seq) == tiles_per_seq - 1
        pos = (i % tiles_per_seq) * tm + lax.broadcasted_iota(jnp.int32, (tm, 1), 0)

        zero = jnp.zeros(pool_p_ref.shape, pool_p_ref.dtype)
        ext = jnp.concatenate([jnp.where(first, zero, pool_p_ref[...]), pool_ref[...],
                               jnp.where(last, zero, pool_n_ref[...])], axis=0)
        for g, w in enumerate(POOL_WINDOWS):
            cs = slice(g * GROUP_DIM, (g + 1) * GROUP_DIM)
            win = jnp.dot(band_ref[g], ext[:, cs], preferred_element_type=F32)
            cnt = (jnp.minimum(pos + w // 2, seq_len) - jnp.maximum(pos - w // 2, 0)).astype(F32)
            p = win * (1.0 / cnt) - pool_ref[:, cs].astype(F32)
            yg = jnp.dot(p.astype(BF16), wpool_ref[g], preferred_element_type=F32) * pscale_ref[:, cs]
            brc_scr[:, cs] = yg.astype(BF16)

        ext_scr[0:HALO, :] = jnp.where(first, 0.0, sv_p_ref[...].astype(F32) * scv_p_ref[...].astype(F32))
        ext_scr[HALO:HALO + tm, :] = sv_ref[...].astype(F32) * scv_ref[...].astype(F32)
        ext_scr[HALO + tm:HALO + tm + HALO, :] = jnp.where(last, 0.0,
                                                           sv_n_ref[...].astype(F32) * scv_n_ref[...].astype(F32))
        y = (ext_scr[HALO - 1:HALO - 1 + tm, :] * convw_ref[0:1, :]
             + ext_scr[HALO:HALO + tm, :] * convw_ref[1:2, :]
             + ext_scr[HALO + 1:HALO + 1 + tm, :] * convw_ref[2:3, :])
        brd_scr[...] = (sb_ref[...].astype(F32) * y).astype(BF16)

    h = h_ref[...]
    branches = ((attn_ref, wpa_ref), (four_ref, wpb_ref), (brc_scr, wpc_ref), (brd_scr, wpd_ref))
    y = None
    for k, (br_ref, wp_ref) in enumerate(branches):
        gate = jax.nn.sigmoid(jnp.dot(h, wg_ref[k], preferred_element_type=F32))
        term = gate * jnp.dot(br_ref[...].astype(BF16), wp_ref[...], preferred_element_type=F32)
        y = term if y is None else y + term
    o_ref[...] = y.astype(o_ref.dtype)


def _merge(h, proj, attn, four, wg, wpa, wpb, wpc, wpd, wpool, pscale, convw, seq_len, tm, tn):
    t, d = h.shape
    tiles_per_seq = seq_len // tm
    n_halo = t // HALO

    rel = np.arange(tm + 2 * HALO)[None, :] - HALO - np.arange(tm)[:, None]
    band = jnp.asarray(np.stack([(rel >= -(w // 2)) & (rel < w // 2) for w in POOL_WINDOWS]), BF16)

    def halo_specs(col):
        return [pl.BlockSpec((tm, MIX_WIDTH), lambda i, j: (i, col)),
                pl.BlockSpec((HALO, MIX_WIDTH), lambda i, j: (jnp.maximum(i * (tm // HALO) - 1, 0), col)),
                pl.BlockSpec((HALO, MIX_WIDTH), lambda i, j: (jnp.minimum((i + 1) * (tm // HALO), n_halo - 1), col))]

    in_specs = [pl.BlockSpec((tm, d), lambda i, j: (i, 0))]
    in_specs += halo_specs(COL_POOL) + halo_specs(COL_SV) + halo_specs(COL_SC)
    in_specs += [pl.BlockSpec((tm, MIX_WIDTH), lambda i, j: (i, COL_SB)),
                 pl.BlockSpec((tm, NA_WIDTH), lambda i, j: (i, 0)),
                 pl.BlockSpec((tm, MIX_WIDTH), lambda i, j: (i, 0)),
                 pl.BlockSpec((4, d, tn), lambda i, j: (0, 0, j)),
                 pl.BlockSpec((NA_WIDTH, tn), lambda i, j: (0, j)),
                 pl.BlockSpec((MIX_WIDTH, tn), lambda i, j: (0, j)),
                 pl.BlockSpec((MIX_WIDTH, tn), lambda i, j: (0, j)),
                 pl.BlockSpec((MIX_WIDTH, tn), lambda i, j: (0, j)),
                 pl.BlockSpec(band.shape, lambda i, j: (0, 0, 0)),
                 pl.BlockSpec((len(POOL_WINDOWS), GROUP_DIM, GROUP_DIM), lambda i, j: (0, 0, 0)),
                 pl.BlockSpec((1, MIX_WIDTH), lambda i, j: (0, 0)),
                 pl.BlockSpec((3, MIX_WIDTH), lambda i, j: (0, 0))]
    return pl.pallas_call(
        functools.partial(_merge_kernel, tiles_per_seq=tiles_per_seq, seq_len=seq_len),
        grid=(t // tm, d // tn),
        in_specs=in_specs,
        out_specs=pl.BlockSpec((tm, tn), lambda i, j: (i, j)),
        out_shape=jax.ShapeDtypeStruct((t, d), BF16),
        scratch_shapes=[pltpu.VMEM((tm, MIX_WIDTH), BF16), pltpu.VMEM((tm, MIX_WIDTH), BF16),
                        pltpu.VMEM((tm + 2 * HALO, MIX_WIDTH), F32)],
        compiler_params=_params(2),
        name="gated_merge",
    )(h, *([proj] * 10), attn, four, wg, wpa, wpb, wpc, wpd, band, wpool, pscale, convw)


def _resid_proj_kernel(a_ref, w_ref, x_ref, g_ref, o_ref):
    o_ref[...] = x_ref[...] + g_ref[...] * jnp.dot(a_ref[...], w_ref[...], preferred_element_type=F32)


def _resid_proj(a, w, x2, modt, chunk, modrow, tm, tn):
    t, k = a.shape
    row_fn = _row_fn(modrow, tm)
    d = w.shape[1]
    return pl.pallas_call(
        _resid_proj_kernel,
        grid=(d // tn, t // tm),
        in_specs=[pl.BlockSpec((tm, k), lambda j, i: (i, 0)),
                  pl.BlockSpec((k, tn), lambda j, i: (0, j)),
                  pl.BlockSpec((tm, tn), lambda j, i: (i, j)),
                  pl.BlockSpec((None, 1, tn), lambda j, i: (row_fn(i) * N_MOD + chunk, 0, j))],
        out_specs=pl.BlockSpec((tm, tn), lambda j, i: (i, j)),
        out_shape=jax.ShapeDtypeStruct((t, d), F32),
        compiler_params=_params(2),
        name="residual_projection",
    )(a, w, x2, modt)


def _resid_norm_kernel(a_ref, w_ref, x_ref, g_ref, sh_ref, sc_ref, nw_ref, o_ref, ho_ref, x_scr):
    i = pl.program_id(0)

    @pl.when(i == 0)
    def _():
        x_scr[1] = jnp.zeros(x_scr.shape[1:], x_scr.dtype)

    x1 = x_ref[...] + g_ref[...] * jnp.dot(a_ref[...], w_ref[...], preferred_element_type=F32)
    o_ref[...] = x1
    x_scr[i % 2] = x1
    _norm_mod_rows(ho_ref, x_scr.at[(i + 1) % 2], nw_ref[...] * (1.0 + sc_ref[...]), sh_ref[...])


def _resid_norm(a, w, x2, modt, modrow, nw, tm):
    t, k = a.shape
    d = w.shape[1]
    nt = t // tm
    row_fn = _row_fn(modrow, tm)
    cur = lambda i: jnp.minimum(i, nt - 1)
    prev = lambda i: jnp.maximum(i - 1, 0)
    mod_spec = lambda chunk, tile: pl.BlockSpec((None, 1, d), lambda i: (row_fn(tile(i)) * N_MOD + chunk, 0, 0))
    return pl.pallas_call(
        _resid_norm_kernel,
        grid=(nt + 1,),
        in_specs=[pl.BlockSpec((tm, k), lambda i: (cur(i), 0)),
                  pl.BlockSpec((k, d), lambda i: (0, 0), pipeline_mode=pl.Buffered(1)),
                  pl.BlockSpec((tm, d), lambda i: (cur(i), 0)),
                  mod_spec(2, cur), mod_spec(3, prev), mod_spec(4, prev),
                  pl.BlockSpec((1, d), lambda i: (0, 0))],
        out_specs=[pl.BlockSpec((tm, d), lambda i: (cur(i), 0)),
                   pl.BlockSpec((tm, d), lambda i: (prev(i), 0))],
        out_shape=[jax.ShapeDtypeStruct((t, d), F32), jax.ShapeDtypeStruct((t, d), BF16)],
        scratch_shapes=[pltpu.VMEM((2, tm, d), F32)],
        compiler_params=_params(1),
        name="residual_projection_norm",
    )(a, w, x2, modt, modt, modt, nw)


def _ffn_up_kernel(h_ref, wg_ref, wu_ref, o_ref):
    h = h_ref[...]
    gt = jnp.dot(h, wg_ref[...], preferred_element_type=F32)
    up = jnp.dot(h, wu_ref[...], preferred_element_type=F32)
    o_ref[...] = (gt * jax.nn.sigmoid(gt) * up).astype(o_ref.dtype)


def _ffn_up(h, w_g, w_u, tm, tn):
    t, d = h.shape
    n = w_g.shape[1]
    return pl.pallas_call(
        _ffn_up_kernel,
        grid=(t // tm, n // tn),
        in_specs=[pl.BlockSpec((tm, d), lambda i, j: (i, 0)),
                  pl.BlockSpec((d, tn), lambda i, j: (0, j)),
                  pl.BlockSpec((d, tn), lambda i, j: (0, j))],
        out_specs=pl.BlockSpec((tm, tn), lambda i, j: (i, j)),
        out_shape=jax.ShapeDtypeStruct((t, n), BF16),
        compiler_params=_params(2),
        name="ffn_up",
    )(h, w_g, w_u)


def _stream_layer(x2, modt, modrow, seq_len, tm, lw, mixers):
    proj, h = _in_proj(x2, modt, modrow, lw["norm1"], lw["w_in"], lw["qw"], lw["kw"], min(tm, 512))
    attn, four = mixers(proj)
    y = _merge(h, proj, attn, four, lw["w_gate"], lw["w_pa"], lw["w_pb"], lw["w_pc"],
               lw["w_pd"], lw["w_pool"], lw["pool_scale"], lw["conv_w"], seq_len, min(tm, 512), 512)
    x2, h2 = _resid_norm(y, lw["w_o"], x2, modt, modrow, lw["norm2"], min(tm, 512))
    a = _ffn_up(h2, lw["w_ffn_gate"], lw["w_ffn_up"], min(2 * tm, x2.shape[0]), 512)
    return _resid_proj(a, lw["w_ffn_down"], x2, modt, 5, modrow, min(tm, 512), min(x2.shape[1], 1024))


def kernel(x, c, ctx, c_ctx, w_mod, b_mod, norm1_w, w_in, q_norm_w, k_norm_w, rpb, w_pool, pool_scale, conv_w,
           w_gate, w_pa, w_pb, w_pc, w_pd, w_o, norm2_w, w_ffn_gate, w_ffn_up, w_ffn_down):
    batch, seq, d = x.shape
    ctx_len = ctx.shape[1]
    depth = w_mod.shape[0]
    rows = seq // GRID_W
    assert seq % (NA_QROWS * GRID_W) == 0 and rows >= NA_KROWS and batch + 1 <= MOD_ROWS

    cvec = jnp.zeros((MOD_ROWS, d), F32).at[:batch].set(c).at[batch].set(c_ctx)
    mod = _modulation(cvec, w_mod, b_mod)

    tm_lat = 1024
    lat_row = (0, seq)
    ctx_row = (batch, batch * ctx_len)

    x2 = x.reshape(batch * seq, d)
    xc = ctx.reshape(batch * ctx_len, d)
    scale = HEAD_DIM ** -0.5
    for l in range(depth):
        lw = {
            "norm1": norm1_w[l].reshape(1, d), "norm2": norm2_w[l].reshape(1, d),
            "w_in": w_in[l].astype(BF16),
            "qw": (q_norm_w[l] * scale).reshape(1, HEAD_DIM), "kw": k_norm_w[l].reshape(1, HEAD_DIM),
            "w_gate": w_gate[l].astype(BF16), "w_pa": w_pa[l].astype(BF16), "w_pb": w_pb[l].astype(BF16),
            "w_pc": w_pc[l].astype(BF16), "w_pd": w_pd[l].astype(BF16), "w_o": w_o[l].astype(BF16),
            "w_pool": w_pool[l].astype(BF16), "pool_scale": pool_scale[l].reshape(1, MIX_WIDTH),
            "conv_w": conv_w[l],
            "w_ffn_gate": w_ffn_gate[l].astype(BF16), "w_ffn_up": w_ffn_up[l].astype(BF16),
            "w_ffn_down": w_ffn_down[l].astype(BF16),
        }
        modt = mod[l].reshape(MOD_ROWS * N_MOD, 1, d)
        bias = _na_bias_tables(rpb[l], rows)

        if l == depth - 1:
            proj_c, _ = _in_proj(xc, modt, ctx_row, lw["norm1"], lw["w_in"], lw["qw"], lw["kw"], ctx_len)
        else:
            holder = {}

            def ctx_mixers(proj):
                holder["proj"] = proj
                return _context_attention(proj, batch), _context_fourier(proj, batch)

            xc = _stream_layer(xc, modt, ctx_row, ctx_len, ctx_len, lw, ctx_mixers)
            proj_c = holder["proj"]

        def lat_mixers(proj):
            return (_neighbourhood_attention(proj, proj_c, bias, batch, seq),
                    _fourier_mix(proj, batch, seq))

        x2 = _stream_layer(x2, modt, lat_row, seq, tm_lat, lw, lat_mixers)
    return x2.reshape(batch, seq, d)
```

```python
import functools
import math

import numpy as np
import jax
import jax.numpy as jnp
from jax import lax
from jax.experimental import pallas as pl
from jax.experimental.pallas import tpu as pltpu

F32 = jnp.float32
BF16 = jnp.bfloat16

GRID_W = 64
NA_HEADS = 8
HEAD_DIM = 128
NA_WIDTH = NA_HEADS * HEAD_DIM
NA_KH = 8
NA_KW = 16
NA_QROWS = 4
NA_KROWS = 12
GROUP_DIM = 128
FT_GROUPS = 4
POOL_WINDOWS = (2, 4, 8, 16)
MIX_WIDTH = 512
N_MOD = 6
RMS_EPS = 1e-6
BF16_ROWS = 16
HALO = 64
NORM_ROWS = 2 * BF16_ROWS
MOD_ROWS = 8
NEG_BIG = -1e30

COL_FT, COL_POOL, COL_SV, COL_SB, COL_SC = 6, 7, 8, 9, 10

VMEM_LIMIT = 56 * 1024 * 1024


def _params(n_axes):
    return pltpu.CompilerParams(dimension_semantics=("arbitrary",) * n_axes,
                                vmem_limit_bytes=VMEM_LIMIT)


def _mod_kernel(c_ref, w_ref, b_ref, o_ref):
    cv = c_ref[...]
    s = cv * jax.nn.sigmoid(cv)
    o_ref[...] = jnp.dot(s.astype(BF16), w_ref[...].astype(BF16),
                         preferred_element_type=F32) + b_ref[...]


def _modulation(cvec, w_mod, b_mod):
    depth, d, n = w_mod.shape
    tn = 1024
    return pl.pallas_call(
        _mod_kernel,
        grid=(depth, n // tn),
        in_specs=[
            pl.BlockSpec((MOD_ROWS, d), lambda l, j: (0, 0)),
            pl.BlockSpec((None, d, tn), lambda l, j: (l, 0, j)),
            pl.BlockSpec((None, 1, tn), lambda l, j: (l, 0, j)),
        ],
        out_specs=pl.BlockSpec((None, MOD_ROWS, tn), lambda l, j: (l, 0, j)),
        out_shape=jax.ShapeDtypeStruct((depth, MOD_ROWS, n), F32),
        compiler_params=_params(2),
        name="modulation",
    )(cvec, w_mod, b_mod.reshape(depth, 1, n))


def _row_fn(modrow, tm):
    base, tokens_per_row = modrow
    return lambda i: base + (i * tm) // tokens_per_row


def _norm_mod_rows(h_ref, x_ref, w, sh, chunk=NORM_ROWS):
    for r in range(0, x_ref.shape[0], chunk):
        xf = x_ref[r:r + chunk, :]
        ms = jnp.mean(xf * xf, axis=-1, keepdims=True)
        h_ref[r:r + chunk, :] = (xf * lax.rsqrt(ms + RMS_EPS) * w + sh).astype(BF16)


def _in_proj_kernel(x_ref, sh_ref, sc_ref, nw_ref, w_ref, qw_ref, kw_ref, o_ref, ho_ref, h_scr):
    i = pl.program_id(0)

    @pl.when(i == 0)
    def _():
        h_scr[1] = jnp.zeros(h_scr.shape[1:], h_scr.dtype)

    h = h_scr[(i + 1) % 2]
    ho_ref[...] = h
    for c0 in range(0, w_ref.shape[1], MIX_WIDTH):
        acc = jnp.dot(h, w_ref[:, c0:c0 + MIX_WIDTH], preferred_element_type=F32)
        if c0 < 2 * NA_WIDTH:
            w = qw_ref[...] if c0 < NA_WIDTH else kw_ref[...]
            for hh in range(0, MIX_WIDTH, HEAD_DIM):
                a = acc[:, hh:hh + HEAD_DIM]
                ms = jnp.mean(a * a, axis=-1, keepdims=True)
                o_ref[:, c0 + hh:c0 + hh + HEAD_DIM] = (a * lax.rsqrt(ms + RMS_EPS) * w).astype(o_ref.dtype)
        else:
            o_ref[:, c0:c0 + MIX_WIDTH] = acc.astype(o_ref.dtype)
    _norm_mod_rows(h_scr.at[i % 2], x_ref, nw_ref[...] * (1.0 + sc_ref[...]), sh_ref[...])


def _wspec(layer, block, index_map, **kwargs):
    return pl.BlockSpec((None,) + tuple(block), lambda *g: (layer,) + tuple(index_map(*g)), **kwargs)


def _in_proj(x2, modt, modrow, nw, w_in, layer, qw, kw, tm):
    t, d = x2.shape
    n = w_in.shape[2]
    nt = t // tm
    row_fn = _row_fn(modrow, tm)
    cur = lambda i: jnp.minimum(i, nt - 1)
    mod_spec = lambda chunk: pl.BlockSpec((None, 1, d), lambda i: (row_fn(cur(i)) * N_MOD + chunk, 0, 0))
    return pl.pallas_call(
        _in_proj_kernel,
        grid=(nt + 1,),
        in_specs=[
            pl.BlockSpec((tm, d), lambda i: (cur(i), 0)),
            mod_spec(0),
            mod_spec(1),
            pl.BlockSpec((1, d), lambda i: (0, 0)),
            _wspec(layer, (d, n), lambda i: (0, 0), pipeline_mode=pl.Buffered(1)),
            pl.BlockSpec((1, HEAD_DIM), lambda i: (0, 0)),
            pl.BlockSpec((1, HEAD_DIM), lambda i: (0, 0)),
        ],
        out_specs=[pl.BlockSpec((tm, n), lambda i: (jnp.maximum(i - 1, 0), 0)),
                   pl.BlockSpec((tm, d), lambda i: (jnp.maximum(i - 1, 0), 0))],
        out_shape=[jax.ShapeDtypeStruct((t, n), BF16), jax.ShapeDtypeStruct((t, d), BF16)],
        scratch_shapes=[pltpu.VMEM((2, tm, d), BF16)],
        compiler_params=_params(1),
        name="in_proj",
    )(x2, modt, modt, nw, w_in, qw, kw)


def _na_bias_tables(rpb, rows):
    heads = rpb.shape[0]
    cols = np.arange(GRID_W)
    cs = np.clip(cols - NA_KW // 2, 0, GRID_W - NA_KW)
    valid_c = (cols[None, :] >= cs[:, None]) & (cols[None, :] < cs[:, None] + NA_KW)
    pad = GRID_W - NA_KW
    rp = jnp.pad(rpb.astype(F32), ((0, 0), (0, 0), (pad, pad)))
    col_t = jnp.stack([rp[:, :, GRID_W - 1 - qc:2 * GRID_W - 1 - qc] for qc in range(GRID_W)], axis=2)
    tables = []
    for r0, kr0 in ((0, 0), (2 * NA_QROWS, 2 * NA_QROWS - NA_KH // 2), (rows - NA_QROWS, rows - NA_KROWS)):
        r = r0 + np.arange(NA_QROWS)
        rs = np.clip(r - NA_KH // 2, 0, rows - NA_KH)
        key_row = kr0 + np.arange(NA_KROWS)
        valid_r = (key_row[None, :] >= rs[:, None]) & (key_row[None, :] < rs[:, None] + NA_KH)
        row_off = np.clip(key_row[None, :] - r[:, None] + (NA_KH - 1), 0, 2 * NA_KH - 2)
        tab = jnp.stack([jnp.stack([col_t[:, int(row_off[qr, kr])] for kr in range(NA_KROWS)], axis=2)
                         for qr in range(NA_QROWS)], axis=1)
        valid = valid_r[:, None, :, None] & valid_c[None, :, None, :]
        tab = jnp.where(valid[None], tab, NEG_BIG)
        tables.append(tab.reshape(heads, NA_QROWS * GRID_W, NA_KROWS * GRID_W))
    return jnp.stack(tables)


def _qk(q, k):
    return lax.dot_general(q, k, (((1,), (1,)), ((), ())), preferred_element_type=F32)


def _lane_tiles(a, width=128):
    return [a[:, k:k + width] for k in range(0, a.shape[1], width)]


def _na_kernel(q_ref, k0_ref, k1_ref, k2_ref, v0_ref, v1_ref, v2_ref, kc_ref, vc_ref, bias_ref, o_ref):
    blk = k0_ref.shape[0]
    for h in range(NA_HEADS):
        hs = slice(h * HEAD_DIM, (h + 1) * HEAD_DIM)
        q = q_ref[:, hs]
        s_loc = [_qk(q, kr[:, hs]) + bias_ref[h, :, m * blk:(m + 1) * blk]
                 for m, kr in enumerate((k0_ref, k1_ref, k2_ref))]
        scores = [_qk(q, kc_ref[:, hs])] + s_loc
        values = (vc_ref, v0_ref, v1_ref, v2_ref)
        mx = jnp.max(functools.reduce(jnp.maximum, [t for s in scores for t in _lane_tiles(s)]),
                     axis=-1, keepdims=True)
        probs = [jnp.exp(s - mx) for s in scores]
        den = jnp.sum(functools.reduce(jnp.add, [t for p in probs for t in _lane_tiles(p)]),
                      axis=-1, keepdims=True)
        o = None
        for p, vr in zip(probs, values):
            pv = jnp.dot(p.astype(BF16), vr[:, hs], preferred_element_type=F32)
            o = pv if o is None else o + pv
        o_ref[:, hs] = (o / den).astype(o_ref.dtype)


def _neighbourhood_attention(proj, proj_ctx, bias, batch, seq):
    blk = NA_QROWS * GRID_W
    nrb = seq // blk
    n_kblk = NA_KROWS // NA_QROWS
    ctx_len = proj_ctx.shape[0] // batch

    def kblock(rb):
        return jnp.clip(rb - 1, 0, nrb - n_kblk)

    def kv_spec(m, col):
        return pl.BlockSpec((blk, NA_WIDTH), lambda b, rb: (b * nrb + kblock(rb) + m, col))

    def variant(rb):
        return jnp.where(rb == 0, 0, jnp.where(rb == nrb - 1, 2, 1))

    in_specs = [pl.BlockSpec((blk, NA_WIDTH), lambda b, rb: (b * nrb + rb, 0))]
    in_specs += [kv_spec(m, 1) for m in range(n_kblk)] + [kv_spec(m, 2) for m in range(n_kblk)]
    in_specs += [pl.BlockSpec((ctx_len, NA_WIDTH), lambda b, rb: (b, 1)),
                 pl.BlockSpec((ctx_len, NA_WIDTH), lambda b, rb: (b, 2)),
                 pl.BlockSpec((None, NA_HEADS, blk, NA_KROWS * GRID_W), lambda b, rb: (variant(rb), 0, 0, 0))]
    return pl.pallas_call(
        _na_kernel,
        grid=(batch, nrb),
        in_specs=in_specs,
        out_specs=pl.BlockSpec((blk, NA_WIDTH), lambda b, rb: (b * nrb + rb, 0)),
        out_shape=jax.ShapeDtypeStruct((batch * seq, NA_WIDTH), BF16),
        compiler_params=_params(2),
        name="neighbourhood_attention",
    )(*([proj] * (1 + 2 * n_kblk)), proj_ctx, proj_ctx, bias)


def _ctx_attn_kernel(q_ref, k_ref, v_ref, o_ref):
    for h in range(NA_HEADS):
        hs = slice(h * HEAD_DIM, (h + 1) * HEAD_DIM)
        s = _qk(q_ref[:, hs], k_ref[:, hs])
        p = jnp.exp(s - jnp.max(s, axis=-1, keepdims=True))
        den = jnp.sum(p, axis=-1, keepdims=True)
        o = jnp.dot(p.astype(BF16), v_ref[:, hs], preferred_element_type=F32)
        o_ref[:, hs] = (o / den).astype(o_ref.dtype)


def _context_attention(proj_ctx, batch):
    ctx_len = proj_ctx.shape[0] // batch
    spec = lambda col: pl.BlockSpec((ctx_len, NA_WIDTH), lambda b: (b, col))
    return pl.pallas_call(
        _ctx_attn_kernel,
        grid=(batch,),
        in_specs=[spec(0), spec(1), spec(2)],
        out_specs=spec(0),
        out_shape=jax.ShapeDtypeStruct((proj_ctx.shape[0], NA_WIDTH), BF16),
        compiler_params=_params(1),
        name="context_attention",
    )(proj_ctx, proj_ctx, proj_ctx)


def _dft_cos_sin(n_out, n_in, period):
    ij = (np.arange(n_out, dtype=np.int64)[:, None] * np.arange(n_in, dtype=np.int64)[None, :]) % period
    ang = 2.0 * np.pi * ij.astype(np.float64) / period
    return np.cos(ang), np.sin(ang)


def _ft_channel_kernel(u_ref, m_ref, o_ref):
    m = m_ref[...]
    for g in range(FT_GROUPS):
        z = jnp.dot(u_ref[:, g * GROUP_DIM:(g + 1) * GROUP_DIM], m, preferred_element_type=F32)
        o_ref[:, g * 2 * GROUP_DIM:(g + 1) * 2 * GROUP_DIM] = z.astype(o_ref.dtype)


def _ft_stage_a_kernel(x_ref, m_ref, o_ref, xr_s, xi_s, yr_s, yi_s):
    c = GROUP_DIM
    n_a, bb = x_ref.shape[0], x_ref.shape[1]
    for a in range(n_a):
        blk = x_ref[a]
        xr_s[a * bb:(a + 1) * bb, :] = blk[:, :c]
        xi_s[a * bb:(a + 1) * bb, :] = blk[:, c:]
    for k in range(bb):
        rows = pl.ds(k, n_a, stride=bb)
        xr, xi = xr_s[rows, :], xi_s[rows, :]
        rhs = jnp.concatenate([jnp.concatenate([xr, xi], axis=1), jnp.concatenate([-xi, xr], axis=1)], axis=0)
        y = jnp.dot(m_ref[k], rhs.astype(BF16), preferred_element_type=F32)
        yr_s[rows, :] = y[:, :c]
        yi_s[rows, :] = y[:, c:]
    for a in range(n_a):
        o_ref[a] = jnp.concatenate([yr_s[a * bb:(a + 1) * bb, :], yi_s[a * bb:(a + 1) * bb, :]], axis=1)


def _ft_stage_b_kernel(x_ref, m_ref, o_ref, o_s):
    c = GROUP_DIM
    aa, n_b = x_ref.shape[0], x_ref.shape[1]
    m = m_ref[...]
    for k in range(aa):
        y = x_ref[k]
        rhs = jnp.concatenate([y[:, :c], y[:, c:]], axis=0).astype(BF16)
        o_s[pl.ds(k, n_b, stride=aa), :] = jnp.dot(m, rhs, preferred_element_type=F32)
    for b in range(n_b):
        o_ref[b] = o_s[b * aa:(b + 1) * aa, :]


def _fourier_mix(proj, batch, seq):
    t = batch * seq
    n_b = GROUP_DIM
    n_a = seq // n_b
    g, c = FT_GROUPS, GROUP_DIM
    sub = 16

    cc, sc = _dft_cos_sin(c, c, c)
    m_c = jnp.asarray(np.concatenate([cc, -sc], axis=1), BF16)
    ca, sa = _dft_cos_sin(n_a, seq, seq)
    ca, sa = ca.reshape(n_a, n_a, n_b), sa.reshape(n_a, n_a, n_b)
    m_a = jnp.asarray(np.concatenate([ca, -sa], axis=1).transpose(2, 0, 1), BF16)
    cb, sb = _dft_cos_sin(n_b, n_b, n_b)
    m_b = jnp.asarray(np.concatenate([cb, sb], axis=1) / math.sqrt(seq * c), BF16)

    tm = 2048
    z = pl.pallas_call(
        _ft_channel_kernel,
        grid=(t // tm,),
        in_specs=[pl.BlockSpec((tm, MIX_WIDTH), lambda i: (i, COL_FT)),
                  pl.BlockSpec((c, 2 * c), lambda i: (0, 0))],
        out_specs=pl.BlockSpec((tm, 2 * g * c), lambda i: (i, 0)),
        out_shape=jax.ShapeDtypeStruct((t, 2 * g * c), F32),
        compiler_params=_params(1),
        name="fourier_channels",
    )(proj, m_c)

    y = pl.pallas_call(
        _ft_stage_a_kernel,
        grid=(n_b // sub, batch, g),
        in_specs=[pl.BlockSpec((None, n_a, sub, 2 * c), lambda j, b, k: (b, 0, j, k)),
                  pl.BlockSpec((sub, n_a, 2 * n_a), lambda j, b, k: (j, 0, 0))],
        out_specs=pl.BlockSpec((None, n_a, sub, 2 * c), lambda j, b, k: (b, 0, j, k)),
        out_shape=jax.ShapeDtypeStruct((batch, n_a, n_b, 2 * g * c), F32),
        scratch_shapes=[pltpu.VMEM((n_a * sub, c), F32)] * 4,
        compiler_params=_params(3),
        name="fourier_stage_a",
    )(z.reshape(batch, n_a, n_b, 2 * g * c), m_a)

    o = pl.pallas_call(
        _ft_stage_b_kernel,
        grid=(batch, g, n_a // sub),
        in_specs=[pl.BlockSpec((None, sub, n_b, 2 * c), lambda b, k, j: (b, j, 0, k)),
                  pl.BlockSpec((n_b, 2 * n_b), lambda b, k, j: (0, 0))],
        out_specs=pl.BlockSpec((None, n_b, sub, c), lambda b, k, j: (b, 0, j, k)),
        out_shape=jax.ShapeDtypeStruct((batch, n_b, n_a, g * c), F32),
        scratch_shapes=[pltpu.VMEM((n_b * sub, c), F32)],
        compiler_params=_params(3),
        name="fourier_stage_b",
    )(y, m_b)
    return o.reshape(t, g * c)


def _ctx_fourier_kernel(u_ref, mc_ref, ml_ref, o_ref):
    mc = mc_ref[...]
    ml = ml_ref[...]
    for g in range(FT_GROUPS):
        z = jnp.dot(u_ref[:, g * GROUP_DIM:(g + 1) * GROUP_DIM], mc, preferred_element_type=F32)
        stack = jnp.concatenate([z[:, :GROUP_DIM], z[:, GROUP_DIM:]], axis=0).astype(BF16)
        o_ref[:, g * GROUP_DIM:(g + 1) * GROUP_DIM] = jnp.dot(ml, stack, preferred_element_type=F32).astype(o_ref.dtype)


def _context_fourier(proj_ctx, batch):
    ctx_len = proj_ctx.shape[0] // batch
    c = GROUP_DIM
    cc, sc = _dft_cos_sin(c, c, c)
    m_c = jnp.asarray(np.concatenate([cc, -sc], axis=1), BF16)
    cl, sl = _dft_cos_sin(ctx_len, ctx_len, ctx_len)
    m_l = jnp.asarray(np.concatenate([cl, sl], axis=1) / math.sqrt(ctx_len * c), BF16)
    return pl.pallas_call(
        _ctx_fourier_kernel,
        grid=(batch,),
        in_specs=[pl.BlockSpec((ctx_len, MIX_WIDTH), lambda b: (b, COL_FT)),
                  pl.BlockSpec((c, 2 * c), lambda b: (0, 0)),
                  pl.BlockSpec((ctx_len, 2 * ctx_len), lambda b: (0, 0))],
        out_specs=pl.BlockSpec((ctx_len, MIX_WIDTH), lambda b: (b, 0)),
        out_shape=jax.ShapeDtypeStruct((proj_ctx.shape[0], MIX_WIDTH), BF16),
        compiler_params=_params(1),
        name="context_fourier",
    )(proj_ctx, m_c, m_l)


def _merge_kernel(h_ref, pool_ref, pool_p_ref, pool_n_ref, sv_ref, sv_p_ref, sv_n_ref, scv_ref, scv_p_ref, scv_n_ref,
                  sb_ref, attn_ref, four_ref, wg_ref, wpa_ref, wpb_ref, wpc_ref, wpd_ref,
                  band_ref, wpool_ref, pscale_ref, convw_ref, o_ref,
                  brc_scr, brd_scr, ext_scr, *, tiles_per_seq, seq_len):
    i = pl.program_id(0)
    j = pl.program_id(1)
    tm = h_ref.shape[0]

    @pl.when(j == 0)
    def _():
        first = (i % tiles_per_seq) == 0
        last = (i % tiles_per_seq) == tiles_per_seq - 1
        pos = (i % tiles_per_seq) * tm + lax.broadcasted_iota(jnp.int32, (tm, 1), 0)

        zero = jnp.zeros(pool_p_ref.shape, pool_p_ref.dtype)
        ext = jnp.concatenate([jnp.where(first, zero, pool_p_ref[...]), pool_ref[...],
                               jnp.where(last, zero, pool_n_ref[...])], axis=0)
        for g, w in enumerate(POOL_WINDOWS):
            cs = slice(g * GROUP_DIM, (g + 1) * GROUP_DIM)
            win = jnp.dot(band_ref[g], ext[:, cs], preferred_element_type=F32)
            cnt = (jnp.minimum(pos + w // 2, seq_len) - jnp.maximum(pos - w // 2, 0)).astype(F32)
            p = win * (1.0 / cnt) - pool_ref[:, cs].astype(F32)
            yg = jnp.dot(p.astype(BF16), wpool_ref[g], preferred_element_type=F32) * pscale_ref[:, cs]
            brc_scr[:, cs] = yg.astype(BF16)

        ext_scr[0:HALO, :] = jnp.where(first, 0.0, sv_p_ref[...].astype(F32) * scv_p_ref[...].astype(F32))
        ext_scr[HALO:HALO + tm, :] = sv_ref[...].astype(F32) * scv_ref[...].astype(F32)
        ext_scr[HALO + tm:HALO + tm + HALO, :] = jnp.where(last, 0.0,
                                                           sv_n_ref[...].astype(F32) * scv_n_ref[...].astype(F32))
        y = (ext_scr[HALO - 1:HALO - 1 + tm, :] * convw_ref[0:1, :]
             + ext_scr[HALO:HALO + tm, :] * convw_ref[1:2, :]
             + ext_scr[HALO + 1:HALO + 1 + tm, :] * convw_ref[2:3, :])
        brd_scr[...] = (sb_ref[...].astype(F32) * y).astype(BF16)

    h = h_ref[...]
    branches = ((attn_ref, wpa_ref), (four_ref, wpb_ref), (brc_scr, wpc_ref), (brd_scr, wpd_ref))
    y = None
    for k, (br_ref, wp_ref) in enumerate(branches):
        gate = jax.nn.sigmoid(jnp.dot(h, wg_ref[k], preferred_element_type=F32))
        term = gate * jnp.dot(br_ref[...].astype(BF16), wp_ref[...], preferred_element_type=F32)
        y = term if y is None else y + term
    o_ref[...] = y.astype(o_ref.dtype)


def _merge(h, proj, attn, four, wg, wpa, wpb, wpc, wpd, wpool, layer, pscale, convw, seq_len, tm, tn):
    t, d = h.shape
    tiles_per_seq = seq_len // tm
    n_halo = t // HALO

    rel = np.arange(tm + 2 * HALO)[None, :] - HALO - np.arange(tm)[:, None]
    band = jnp.asarray(np.stack([(rel >= -(w // 2)) & (rel < w // 2) for w in POOL_WINDOWS]), BF16)

    def halo_specs(col):
        return [pl.BlockSpec((tm, MIX_WIDTH), lambda i, j: (i, col)),
                pl.BlockSpec((HALO, MIX_WIDTH), lambda i, j: (jnp.maximum(i * (tm // HALO) - 1, 0), col)),
                pl.BlockSpec((HALO, MIX_WIDTH), lambda i, j: (jnp.minimum((i + 1) * (tm // HALO), n_halo - 1), col))]

    in_specs = [pl.BlockSpec((tm, d), lambda i, j: (i, 0))]
    in_specs += halo_specs(COL_POOL) + halo_specs(COL_SV) + halo_specs(COL_SC)
    in_specs += [pl.BlockSpec((tm, MIX_WIDTH), lambda i, j: (i, COL_SB)),
                 pl.BlockSpec((tm, NA_WIDTH), lambda i, j: (i, 0)),
                 pl.BlockSpec((tm, MIX_WIDTH), lambda i, j: (i, 0)),
                 _wspec(layer, (4, d, tn), lambda i, j: (0, 0, j)),
                 _wspec(layer, (NA_WIDTH, tn), lambda i, j: (0, j)),
                 _wspec(layer, (MIX_WIDTH, tn), lambda i, j: (0, j)),
                 _wspec(layer, (MIX_WIDTH, tn), lambda i, j: (0, j)),
                 _wspec(layer, (MIX_WIDTH, tn), lambda i, j: (0, j)),
                 pl.BlockSpec(band.shape, lambda i, j: (0, 0, 0)),
                 _wspec(layer, (len(POOL_WINDOWS), GROUP_DIM, GROUP_DIM), lambda i, j: (0, 0, 0)),
                 pl.BlockSpec((1, MIX_WIDTH), lambda i, j: (0, 0)),
                 pl.BlockSpec((3, MIX_WIDTH), lambda i, j: (0, 0))]
    return pl.pallas_call(
        functools.partial(_merge_kernel, tiles_per_seq=tiles_per_seq, seq_len=seq_len),
        grid=(t // tm, d // tn),
        in_specs=in_specs,
        out_specs=pl.BlockSpec((tm, tn), lambda i, j: (i, j)),
        out_shape=jax.ShapeDtypeStruct((t, d), BF16),
        scratch_shapes=[pltpu.VMEM((tm, MIX_WIDTH), BF16), pltpu.VMEM((tm, MIX_WIDTH), BF16),
                        pltpu.VMEM((tm + 2 * HALO, MIX_WIDTH), F32)],
        compiler_params=_params(2),
        name="gated_merge",
    )(h, *([proj] * 10), attn, four, wg, wpa, wpb, wpc, wpd, band, wpool, pscale, convw)


def _resid_proj_kernel(a_ref, w_ref, x_ref, g_ref, o_ref):
    o_ref[...] = x_ref[...] + g_ref[...] * jnp.dot(a_ref[...], w_ref[...], preferred_element_type=F32)


def _resid_proj(a, w, layer, x2, modt, chunk, modrow, tm, tn):
    t, k = a.shape
    row_fn = _row_fn(modrow, tm)
    d = w.shape[2]
    return pl.pallas_call(
        _resid_proj_kernel,
        grid=(d // tn, t // tm),
        in_specs=[pl.BlockSpec((tm, k), lambda j, i: (i, 0)),
                  _wspec(layer, (k, tn), lambda j, i: (0, j)),
                  pl.BlockSpec((tm, tn), lambda j, i: (i, j)),
                  pl.BlockSpec((None, 1, tn), lambda j, i: (row_fn(i) * N_MOD + chunk, 0, j))],
        out_specs=pl.BlockSpec((tm, tn), lambda j, i: (i, j)),
        out_shape=jax.ShapeDtypeStruct((t, d), F32),
        compiler_params=_params(2),
        name="residual_projection",
    )(a, w, x2, modt)


def _resid_norm_kernel(a_ref, w_ref, x_ref, g_ref, sh_ref, sc_ref, nw_ref, o_ref, ho_ref, x_scr):
    i = pl.program_id(0)

    @pl.when(i == 0)
    def _():
        x_scr[1] = jnp.zeros(x_scr.shape[1:], x_scr.dtype)

    x1 = x_ref[...] + g_ref[...] * jnp.dot(a_ref[...], w_ref[...], preferred_element_type=F32)
    o_ref[...] = x1
    x_scr[i % 2] = x1
    _norm_mod_rows(ho_ref, x_scr.at[(i + 1) % 2], nw_ref[...] * (1.0 + sc_ref[...]), sh_ref[...])


def _resid_norm(a, w, layer, x2, modt, modrow, nw, tm):
    t, k = a.shape
    d = w.shape[2]
    nt = t // tm
    row_fn = _row_fn(modrow, tm)
    cur = lambda i: jnp.minimum(i, nt - 1)
    prev = lambda i: jnp.maximum(i - 1, 0)
    mod_spec = lambda chunk, tile: pl.BlockSpec((None, 1, d), lambda i: (row_fn(tile(i)) * N_MOD + chunk, 0, 0))
    return pl.pallas_call(
        _resid_norm_kernel,
        grid=(nt + 1,),
        in_specs=[pl.BlockSpec((tm, k), lambda i: (cur(i), 0)),
                  _wspec(layer, (k, d), lambda i: (0, 0), pipeline_mode=pl.Buffered(1)),
                  pl.BlockSpec((tm, d), lambda i: (cur(i), 0)),
                  mod_spec(2, cur), mod_spec(3, prev), mod_spec(4, prev),
                  pl.BlockSpec((1, d), lambda i: (0, 0))],
        out_specs=[pl.BlockSpec((tm, d), lambda i: (cur(i), 0)),
                   pl.BlockSpec((tm, d), lambda i: (prev(i), 0))],
        out_shape=[jax.ShapeDtypeStruct((t, d), F32), jax.ShapeDtypeStruct((t, d), BF16)],
        scratch_shapes=[pltpu.VMEM((2, tm, d), F32)],
        compiler_params=_params(1),
        name="residual_projection_norm",
    )(a, w, x2, modt, modt, modt, nw)


def _ffn_up_kernel(h_ref, wg_ref, wu_ref, o_ref):
    h = h_ref[...]
    gt = jnp.dot(h, wg_ref[...], preferred_element_type=F32)
    up = jnp.dot(h, wu_ref[...], preferred_element_type=F32)
    o_ref[...] = (gt * jax.nn.sigmoid(gt) * up).astype(o_ref.dtype)


def _ffn_up(h, w_g, w_u, layer, tm, tn):
    t, d = h.shape
    n = w_g.shape[2]
    return pl.pallas_call(
        _ffn_up_kernel,
        grid=(t // tm, n // tn),
        in_specs=[pl.BlockSpec((tm, d), lambda i, j: (i, 0)),
                  _wspec(layer, (d, tn), lambda i, j: (0, j)),
                  _wspec(layer, (d, tn), lambda i, j: (0, j))],
        out_specs=pl.BlockSpec((tm, tn), lambda i, j: (i, j)),
        out_shape=jax.ShapeDtypeStruct((t, n), BF16),
        compiler_params=_params(2),
        name="ffn_up",
    )(h, w_g, w_u)


def _stream_layer(x2, modt, modrow, seq_len, tm, lw, mixers):
    l = lw["layer"]
    proj, h = _in_proj(x2, modt, modrow, lw["norm1"], lw["w_in"], l, lw["qw"], lw["kw"], min(tm, 512))
    attn, four = mixers(proj)
    y = _merge(h, proj, attn, four, lw["w_gate"], lw["w_pa"], lw["w_pb"], lw["w_pc"],
               lw["w_pd"], lw["w_pool"], l, lw["pool_scale"], lw["conv_w"], seq_len, min(tm, 512), 512)
    x2, h2 = _resid_norm(y, lw["w_o"], l, x2, modt, modrow, lw["norm2"], min(tm, 512))
    a = _ffn_up(h2, lw["w_ffn_gate"], lw["w_ffn_up"], l, min(2 * tm, x2.shape[0]), 512)
    return _resid_proj(a, lw["w_ffn_down"], l, x2, modt, 5, modrow, min(tm, 512), min(x2.shape[1], 1024))


def kernel(x, c, ctx, c_ctx, w_mod, b_mod, norm1_w, w_in, q_norm_w, k_norm_w, rpb, w_pool, pool_scale, conv_w,
           w_gate, w_pa, w_pb, w_pc, w_pd, w_o, norm2_w, w_ffn_gate, w_ffn_up, w_ffn_down):
    batch, seq, d = x.shape
    ctx_len = ctx.shape[1]
    depth = w_mod.shape[0]
    rows = seq // GRID_W
    assert seq % (NA_QROWS * GRID_W) == 0 and rows >= NA_KROWS and batch + 1 <= MOD_ROWS

    cvec = jnp.zeros((MOD_ROWS, d), F32).at[:batch].set(c).at[batch].set(c_ctx)
    mod = _modulation(cvec, w_mod, b_mod)

    tm_lat = 1024
    lat_row = (0, seq)
    ctx_row = (batch, batch * ctx_len)

    x2 = x.reshape(batch * seq, d)
    xc = ctx.reshape(batch * ctx_len, d)
    scale = HEAD_DIM ** -0.5
    stacked = {name: w.astype(BF16) for name, w in (
        ("w_in", w_in), ("w_gate", w_gate), ("w_pa", w_pa), ("w_pb", w_pb), ("w_pc", w_pc), ("w_pd", w_pd),
        ("w_o", w_o), ("w_pool", w_pool), ("w_ffn_gate", w_ffn_gate), ("w_ffn_up", w_ffn_up),
        ("w_ffn_down", w_ffn_down))}
    for l in range(depth):
        lw = dict(stacked)
        lw.update({
            "layer": l,
            "norm1": norm1_w[l].reshape(1, d), "norm2": norm2_w[l].reshape(1, d),
            "qw": (q_norm_w[l] * scale).reshape(1, HEAD_DIM), "kw": k_norm_w[l].reshape(1, HEAD_DIM),
            "pool_scale": pool_scale[l].reshape(1, MIX_WIDTH), "conv_w": conv_w[l],
        })
        modt = mod[l].reshape(MOD_ROWS * N_MOD, 1, d)
        bias = _na_bias_tables(rpb[l], rows)

        if l == depth - 1:
            proj_c, _ = _in_proj(xc, modt, ctx_row, lw["norm1"], lw["w_in"], l, lw["qw"], lw["kw"], ctx_len)
        else:
            holder = {}

            def ctx_mixers(proj):
                holder["proj"] = proj
                return _context_attention(proj, batch), _context_fourier(proj, batch)

            xc = _stream_layer(xc, modt, ctx_row, ctx_len, ctx_len, lw, ctx_mixers)
            proj_c = holder["proj"]

        def lat_mixers(proj):
            return (_neighbourhood_attention(proj, proj_c, bias, batch, seq),
                    _fourier_mix(proj, batch, seq))

        x2 = _stream_layer(x2, modt, lat_row, seq, tm_lat, lw, lat_mixers)
    return x2.reshape(batch, seq, d)
```

```python
import functools
import math

import numpy as np
import jax
import jax.numpy as jnp
from jax import lax
from jax.experimental import pallas as pl
from jax.experimental.pallas import tpu as pltpu

F32 = jnp.float32
BF16 = jnp.bfloat16

GRID_W = 64
NA_HEADS = 8
HEAD_DIM = 128
NA_WIDTH = NA_HEADS * HEAD_DIM
NA_KH = 8
NA_KW = 16
NA_QROWS = 4
NA_KROWS = 12
GROUP_DIM = 128
FT_GROUPS = 4
POOL_WINDOWS = (2, 4, 8, 16)
MIX_WIDTH = 512
N_MOD = 6
RMS_EPS = 1e-6
BF16_ROWS = 16
HALO = 64
NORM_ROWS = 2 * BF16_ROWS
MOD_ROWS = 8
NEG_BIG = -1e30

COL_FT, COL_POOL, COL_SV, COL_SB, COL_SC = 6, 7, 8, 9, 10

VMEM_LIMIT = 56 * 1024 * 1024


def _params(n_axes):
    return pltpu.CompilerParams(dimension_semantics=("arbitrary",) * n_axes,
                                vmem_limit_bytes=VMEM_LIMIT)


def _mod_kernel(c_ref, w_ref, b_ref, o_ref):
    cv = c_ref[...]
    s = cv * jax.nn.sigmoid(cv)
    o_ref[...] = jnp.dot(s.astype(BF16), w_ref[...].astype(BF16),
                         preferred_element_type=F32) + b_ref[...]


def _modulation(cvec, w_mod, b_mod):
    depth, d, n = w_mod.shape
    tn = 1024
    return pl.pallas_call(
        _mod_kernel,
        grid=(depth, n // tn),
        in_specs=[
            pl.BlockSpec((MOD_ROWS, d), lambda l, j: (0, 0)),
            pl.BlockSpec((None, d, tn), lambda l, j: (l, 0, j)),
            pl.BlockSpec((None, 1, tn), lambda l, j: (l, 0, j)),
        ],
        out_specs=pl.BlockSpec((None, MOD_ROWS, tn), lambda l, j: (l, 0, j)),
        out_shape=jax.ShapeDtypeStruct((depth, MOD_ROWS, n), F32),
        compiler_params=_params(2),
        name="modulation",
    )(cvec, w_mod, b_mod.reshape(depth, 1, n))


def _row_fn(modrow, tm):
    base, tokens_per_row = modrow
    return lambda i: base + (i * tm) // tokens_per_row


def _wspec(layer, block, index_map, **kwargs):
    return pl.BlockSpec((None,) + tuple(block), lambda *g: (layer,) + tuple(index_map(*g)), **kwargs)


def _norm_mod_rows(h_ref, x_ref, w, sh, chunk=NORM_ROWS):
    for r in range(0, x_ref.shape[0], chunk):
        xf = x_ref[r:r + chunk, :]
        ms = jnp.mean(xf * xf, axis=-1, keepdims=True)
        h_ref[r:r + chunk, :] = (xf * lax.rsqrt(ms + RMS_EPS) * w + sh).astype(BF16)


def _in_proj_kernel(x_ref, sh_ref, sc_ref, nw_ref, w_ref, qw_ref, kw_ref, o_ref, ho_ref, h_scr):
    i = pl.program_id(0)

    @pl.when(i == 0)
    def _():
        h_scr[1] = jnp.zeros(h_scr.shape[1:], h_scr.dtype)

    h = h_scr[(i + 1) % 2]
    ho_ref[...] = h
    for c0 in range(0, w_ref.shape[1], MIX_WIDTH):
        acc = jnp.dot(h, w_ref[:, c0:c0 + MIX_WIDTH], preferred_element_type=F32)
        if c0 < 2 * NA_WIDTH:
            w = qw_ref[...] if c0 < NA_WIDTH else kw_ref[...]
            for hh in range(0, MIX_WIDTH, HEAD_DIM):
                a = acc[:, hh:hh + HEAD_DIM]
                ms = jnp.mean(a * a, axis=-1, keepdims=True)
                o_ref[:, c0 + hh:c0 + hh + HEAD_DIM] = (a * lax.rsqrt(ms + RMS_EPS) * w).astype(o_ref.dtype)
        else:
            o_ref[:, c0:c0 + MIX_WIDTH] = acc.astype(o_ref.dtype)
    _norm_mod_rows(h_scr.at[i % 2], x_ref, nw_ref[...] * (1.0 + sc_ref[...]), sh_ref[...])


def _in_proj(x2, modt, modrow, nw, w_in, layer, qw, kw, tm):
    t, d = x2.shape
    n = w_in.shape[2]
    nt = t // tm
    row_fn = _row_fn(modrow, tm)
    cur = lambda i: jnp.minimum(i, nt - 1)
    mod_spec = lambda chunk: pl.BlockSpec((None, 1, d), lambda i: (row_fn(cur(i)) * N_MOD + chunk, 0, 0))
    return pl.pallas_call(
        _in_proj_kernel,
        grid=(nt + 1,),
        in_specs=[
            pl.BlockSpec((tm, d), lambda i: (cur(i), 0)),
            mod_spec(0),
            mod_spec(1),
            pl.BlockSpec((1, d), lambda i: (0, 0)),
            _wspec(layer, (d, n), lambda i: (0, 0), pipeline_mode=pl.Buffered(1)),
            pl.BlockSpec((1, HEAD_DIM), lambda i: (0, 0)),
            pl.BlockSpec((1, HEAD_DIM), lambda i: (0, 0)),
        ],
        out_specs=[pl.BlockSpec((tm, n), lambda i: (jnp.maximum(i - 1, 0), 0)),
                   pl.BlockSpec((tm, d), lambda i: (jnp.maximum(i - 1, 0), 0))],
        out_shape=[jax.ShapeDtypeStruct((t, n), BF16), jax.ShapeDtypeStruct((t, d), BF16)],
        scratch_shapes=[pltpu.VMEM((2, tm, d), BF16)],
        compiler_params=_params(1),
        name="in_proj",
    )(x2, modt, modt, nw, w_in, qw, kw)


def _na_bias_tables(rpb, rows):
    heads = rpb.shape[0]
    cols = np.arange(GRID_W)
    cs = np.clip(cols - NA_KW // 2, 0, GRID_W - NA_KW)
    valid_c = (cols[None, :] >= cs[:, None]) & (cols[None, :] < cs[:, None] + NA_KW)
    pad = GRID_W - NA_KW
    rp = jnp.pad(rpb.astype(F32), ((0, 0), (0, 0), (pad, pad)))
    col_t = jnp.stack([rp[:, :, GRID_W - 1 - qc:2 * GRID_W - 1 - qc] for qc in range(GRID_W)], axis=2)
    tables = []
    for r0, kr0 in ((0, 0), (2 * NA_QROWS, 2 * NA_QROWS - NA_KH // 2), (rows - NA_QROWS, rows - NA_KROWS)):
        r = r0 + np.arange(NA_QROWS)
        rs = np.clip(r - NA_KH // 2, 0, rows - NA_KH)
        key_row = kr0 + np.arange(NA_KROWS)
        valid_r = (key_row[None, :] >= rs[:, None]) & (key_row[None, :] < rs[:, None] + NA_KH)
        row_off = np.clip(key_row[None, :] - r[:, None] + (NA_KH - 1), 0, 2 * NA_KH - 2)
        tab = jnp.stack([jnp.stack([col_t[:, int(row_off[qr, kr])] for kr in range(NA_KROWS)], axis=2)
                         for qr in range(NA_QROWS)], axis=1)
        valid = valid_r[:, None, :, None] & valid_c[None, :, None, :]
        tab = jnp.where(valid[None], tab, NEG_BIG)
        tables.append(tab.reshape(heads, NA_QROWS * GRID_W, NA_KROWS * GRID_W))
    return jnp.stack(tables)


def _qk(q, k):
    return lax.dot_general(q, k, (((1,), (1,)), ((), ())), preferred_element_type=F32)


def _lane_tiles(a, width=128):
    return [a[:, k:k + width] for k in range(0, a.shape[1], width)]


def _na_kernel(q_ref, k0_ref, k1_ref, k2_ref, v0_ref, v1_ref, v2_ref, kc_ref, vc_ref, bias_ref, o_ref):
    blk = k0_ref.shape[0]
    values = (vc_ref, v0_ref, v1_ref, v2_ref)

    def scores_of(h):
        hs = slice(h * HEAD_DIM, (h + 1) * HEAD_DIM)
        q = q_ref[:, hs]
        s_loc = [_qk(q, kr[:, hs]) + bias_ref[h, :, m * blk:(m + 1) * blk]
                 for m, kr in enumerate((k0_ref, k1_ref, k2_ref))]
        return [_qk(q, kc_ref[:, hs])] + s_loc

    nxt = scores_of(0)
    for h in range(NA_HEADS):
        hs = slice(h * HEAD_DIM, (h + 1) * HEAD_DIM)
        scores = nxt
        if h + 1 < NA_HEADS:
            nxt = scores_of(h + 1)
        mx = jnp.max(functools.reduce(jnp.maximum, [t for s in scores for t in _lane_tiles(s)]),
                     axis=-1, keepdims=True)
        probs = [jnp.exp(s - mx) for s in scores]
        den = jnp.sum(functools.reduce(jnp.add, [t for p in probs for t in _lane_tiles(p)]),
                      axis=-1, keepdims=True)
        o = None
        for p, vr in zip(probs, values):
            pv = jnp.dot(p.astype(BF16), vr[:, hs], preferred_element_type=F32)
            o = pv if o is None else o + pv
        o_ref[:, hs] = (o / den).astype(o_ref.dtype)


def _neighbourhood_attention(proj, proj_ctx, bias, batch, seq):
    blk = NA_QROWS * GRID_W
    nrb = seq // blk
    n_kblk = NA_KROWS // NA_QROWS
    ctx_len = proj_ctx.shape[0] // batch

    def kblock(rb):
        return jnp.clip(rb - 1, 0, nrb - n_kblk)

    def kv_spec(m, col):
        return pl.BlockSpec((blk, NA_WIDTH), lambda b, rb: (b * nrb + kblock(rb) + m, col))

    def variant(rb):
        return jnp.where(rb == 0, 0, jnp.where(rb == nrb - 1, 2, 1))

    in_specs = [pl.BlockSpec((blk, NA_WIDTH), lambda b, rb: (b * nrb + rb, 0))]
    in_specs += [kv_spec(m, 1) for m in range(n_kblk)] + [kv_spec(m, 2) for m in range(n_kblk)]
    in_specs += [pl.BlockSpec((ctx_len, NA_WIDTH), lambda b, rb: (b, 1)),
                 pl.BlockSpec((ctx_len, NA_WIDTH), lambda b, rb: (b, 2)),
                 pl.BlockSpec((None, NA_HEADS, blk, NA_KROWS * GRID_W), lambda b, rb: (variant(rb), 0, 0, 0))]
    return pl.pallas_call(
        _na_kernel,
        grid=(batch, nrb),
        in_specs=in_specs,
        out_specs=pl.BlockSpec((blk, NA_WIDTH), lambda b, rb: (b * nrb + rb, 0)),
        out_shape=jax.ShapeDtypeStruct((batch * seq, NA_WIDTH), BF16),
        compiler_params=_params(2),
        name="neighbourhood_attention",
    )(*([proj] * (1 + 2 * n_kblk)), proj_ctx, proj_ctx, bias)


def _ctx_attn_kernel(q_ref, k_ref, v_ref, o_ref):
    for h in range(NA_HEADS):
        hs = slice(h * HEAD_DIM, (h + 1) * HEAD_DIM)
        s = _qk(q_ref[:, hs], k_ref[:, hs])
        p = jnp.exp(s - jnp.max(s, axis=-1, keepdims=True))
        den = jnp.sum(p, axis=-1, keepdims=True)
        o = jnp.dot(p.astype(BF16), v_ref[:, hs], preferred_element_type=F32)
        o_ref[:, hs] = (o / den).astype(o_ref.dtype)


def _context_attention(proj_ctx, batch):
    ctx_len = proj_ctx.shape[0] // batch
    spec = lambda col: pl.BlockSpec((ctx_len, NA_WIDTH), lambda b: (b, col))
    return pl.pallas_call(
        _ctx_attn_kernel,
        grid=(batch,),
        in_specs=[spec(0), spec(1), spec(2)],
        out_specs=spec(0),
        out_shape=jax.ShapeDtypeStruct((proj_ctx.shape[0], NA_WIDTH), BF16),
        compiler_params=_params(1),
        name="context_attention",
    )(proj_ctx, proj_ctx, proj_ctx)


def _dft_cos_sin(n_out, n_in, period):
    ij = (np.arange(n_out, dtype=np.int64)[:, None] * np.arange(n_in, dtype=np.int64)[None, :]) % period
    ang = 2.0 * np.pi * ij.astype(np.float64) / period
    return np.cos(ang), np.sin(ang)


def _ft_channel_kernel(u_ref, m_ref, o_ref):
    m = m_ref[...]
    for g in range(FT_GROUPS):
        z = jnp.dot(u_ref[:, g * GROUP_DIM:(g + 1) * GROUP_DIM], m, preferred_element_type=F32)
        o_ref[:, g * 2 * GROUP_DIM:(g + 1) * 2 * GROUP_DIM] = z.astype(o_ref.dtype)


def _ft_stage_a_kernel(x_ref, m_ref, o_ref, xr_s, xi_s, yr_s, yi_s):
    c = GROUP_DIM
    n_a, bb = x_ref.shape[0], x_ref.shape[1]
    for a in range(n_a):
        blk = x_ref[a]
        xr_s[a * bb:(a + 1) * bb, :] = blk[:, :c]
        xi_s[a * bb:(a + 1) * bb, :] = blk[:, c:]
    for k in range(bb):
        rows = pl.ds(k, n_a, stride=bb)
        xr, xi = xr_s[rows, :], xi_s[rows, :]
        rhs = jnp.concatenate([jnp.concatenate([xr, xi], axis=1), jnp.concatenate([-xi, xr], axis=1)], axis=0)
        y = jnp.dot(m_ref[k], rhs.astype(BF16), preferred_element_type=F32)
        yr_s[rows, :] = y[:, :c]
        yi_s[rows, :] = y[:, c:]
    for a in range(n_a):
        o_ref[a] = jnp.concatenate([yr_s[a * bb:(a + 1) * bb, :], yi_s[a * bb:(a + 1) * bb, :]], axis=1)


def _ft_stage_b_kernel(x_ref, m_ref, o_ref, o_s):
    c = GROUP_DIM
    aa, n_b = x_ref.shape[0], x_ref.shape[1]
    m = m_ref[...]
    for k in range(aa):
        y = x_ref[k]
        rhs = jnp.concatenate([y[:, :c], y[:, c:]], axis=0).astype(BF16)
        o_s[pl.ds(k, n_b, stride=aa), :] = jnp.dot(m, rhs, preferred_element_type=F32)
    for b in range(n_b):
        o_ref[b] = o_s[b * aa:(b + 1) * aa, :]


def _fourier_mix(proj, batch, seq):
    t = batch * seq
    n_b = GROUP_DIM
    n_a = seq // n_b
    g, c = FT_GROUPS, GROUP_DIM
    sub = 16

    cc, sc = _dft_cos_sin(c, c, c)
    m_c = jnp.asarray(np.concatenate([cc, -sc], axis=1), BF16)
    ca, sa = _dft_cos_sin(n_a, seq, seq)
    ca, sa = ca.reshape(n_a, n_a, n_b), sa.reshape(n_a, n_a, n_b)
    m_a = jnp.asarray(np.concatenate([ca, -sa], axis=1).transpose(2, 0, 1), BF16)
    cb, sb = _dft_cos_sin(n_b, n_b, n_b)
    m_b = jnp.asarray(np.concatenate([cb, sb], axis=1) / math.sqrt(seq * c), BF16)

    tm = 2048
    z = pl.pallas_call(
        _ft_channel_kernel,
        grid=(t // tm,),
        in_specs=[pl.BlockSpec((tm, MIX_WIDTH), lambda i: (i, COL_FT)),
                  pl.BlockSpec((c, 2 * c), lambda i: (0, 0))],
        out_specs=pl.BlockSpec((tm, 2 * g * c), lambda i: (i, 0)),
        out_shape=jax.ShapeDtypeStruct((t, 2 * g * c), F32),
        compiler_params=_params(1),
        name="fourier_channels",
    )(proj, m_c)

    y = pl.pallas_call(
        _ft_stage_a_kernel,
        grid=(n_b // sub, batch, g),
        in_specs=[pl.BlockSpec((None, n_a, sub, 2 * c), lambda j, b, k: (b, 0, j, k)),
                  pl.BlockSpec((sub, n_a, 2 * n_a), lambda j, b, k: (j, 0, 0))],
        out_specs=pl.BlockSpec((None, n_a, sub, 2 * c), lambda j, b, k: (b, 0, j, k)),
        out_shape=jax.ShapeDtypeStruct((batch, n_a, n_b, 2 * g * c), F32),
        scratch_shapes=[pltpu.VMEM((n_a * sub, c), F32)] * 4,
        compiler_params=_params(3),
        name="fourier_stage_a",
    )(z.reshape(batch, n_a, n_b, 2 * g * c), m_a)

    o = pl.pallas_call(
        _ft_stage_b_kernel,
        grid=(batch, g, n_a // sub),
        in_specs=[pl.BlockSpec((None, sub, n_b, 2 * c), lambda b, k, j: (b, j, 0, k)),
                  pl.BlockSpec((n_b, 2 * n_b), lambda b, k, j: (0, 0))],
        out_specs=pl.BlockSpec((None, n_b, sub, c), lambda b, k, j: (b, 0, j, k)),
        out_shape=jax.ShapeDtypeStruct((batch, n_b, n_a, g * c), F32),
        scratch_shapes=[pltpu.VMEM((n_b * sub, c), F32)],
        compiler_params=_params(3),
        name="fourier_stage_b",
    )(y, m_b)
    return o.reshape(t, g * c)


def _ctx_fourier_kernel(u_ref, mc_ref, ml_ref, o_ref):
    mc = mc_ref[...]
    ml = ml_ref[...]
    for g in range(FT_GROUPS):
        z = jnp.dot(u_ref[:, g * GROUP_DIM:(g + 1) * GROUP_DIM], mc, preferred_element_type=F32)
        stack = jnp.concatenate([z[:, :GROUP_DIM], z[:, GROUP_DIM:]], axis=0).astype(BF16)
        o_ref[:, g * GROUP_DIM:(g + 1) * GROUP_DIM] = jnp.dot(ml, stack, preferred_element_type=F32).astype(o_ref.dtype)


def _context_fourier(proj_ctx, batch):
    ctx_len = proj_ctx.shape[0] // batch
    c = GROUP_DIM
    cc, sc = _dft_cos_sin(c, c, c)
    m_c = jnp.asarray(np.concatenate([cc, -sc], axis=1), BF16)
    cl, sl = _dft_cos_sin(ctx_len, ctx_len, ctx_len)
    m_l = jnp.asarray(np.concatenate([cl, sl], axis=1) / math.sqrt(ctx_len * c), BF16)
    return pl.pallas_call(
        _ctx_fourier_kernel,
        grid=(batch,),
        in_specs=[pl.BlockSpec((ctx_len, MIX_WIDTH), lambda b: (b, COL_FT)),
                  pl.BlockSpec((c, 2 * c), lambda b: (0, 0)),
                  pl.BlockSpec((ctx_len, 2 * ctx_len), lambda b: (0, 0))],
        out_specs=pl.BlockSpec((ctx_len, MIX_WIDTH), lambda b: (b, 0)),
        out_shape=jax.ShapeDtypeStruct((proj_ctx.shape[0], MIX_WIDTH), BF16),
        compiler_params=_params(1),
        name="context_fourier",
    )(proj_ctx, m_c, m_l)


def _merge_kernel(h_ref, pool_ref, pool_p_ref, pool_n_ref, sv_ref, sv_p_ref, sv_n_ref, scv_ref, scv_p_ref, scv_n_ref,
                  sb_ref, attn_ref, four_ref, wg_ref, wpa_ref, wpb_ref, wpc_ref, wpd_ref,
                  band_ref, wpool_ref, pscale_ref, convw_ref, o_ref,
                  brc_scr, brd_scr, ext_scr, *, tiles_per_seq, seq_len):
    i = pl.program_id(0)
    j = pl.program_id(1)
    tm = h_ref.shape[0]

    @pl.when(j == 0)
    def _():
        first = (i % tiles_per_seq) == 0
        last = (i % tiles_per_seq) == tiles_per_seq - 1
        pos = (i % tiles_per_seq) * tm + lax.broadcasted_iota(jnp.int32, (tm, 1), 0)

        zero = jnp.zeros(pool_p_ref.shape, pool_p_ref.dtype)
        ext = jnp.concatenate([jnp.where(first, zero, pool_p_ref[...]), pool_ref[...],
                               jnp.where(last, zero, pool_n_ref[...])], axis=0)
        for g, w in enumerate(POOL_WINDOWS):
            cs = slice(g * GROUP_DIM, (g + 1) * GROUP_DIM)
            win = jnp.dot(band_ref[g], ext[:, cs], preferred_element_type=F32)
            cnt = (jnp.minimum(pos + w // 2, seq_len) - jnp.maximum(pos - w // 2, 0)).astype(F32)
            p = win * (1.0 / cnt) - pool_ref[:, cs].astype(F32)
            yg = jnp.dot(p.astype(BF16), wpool_ref[g], preferred_element_type=F32) * pscale_ref[:, cs]
            brc_scr[:, cs] = yg.astype(BF16)

        ext_scr[0:HALO, :] = jnp.where(first, 0.0, sv_p_ref[...].astype(F32) * scv_p_ref[...].astype(F32))
        ext_scr[HALO:HALO + tm, :] = sv_ref[...].astype(F32) * scv_ref[...].astype(F32)
        ext_scr[HALO + tm:HALO + tm + HALO, :] = jnp.where(last, 0.0,
                                                           sv_n_ref[...].astype(F32) * scv_n_ref[...].astype(F32))
        y = (ext_scr[HALO - 1:HALO - 1 + tm, :] * convw_ref[0:1, :]
             + ext_scr[HALO:HALO + tm, :] * convw_ref[1:2, :]
             + ext_scr[HALO + 1:HALO + 1 + tm, :] * convw_ref[2:3, :])
        brd_scr[...] = (sb_ref[...].astype(F32) * y).astype(BF16)

    h = h_ref[...]
    branches = ((attn_ref, wpa_ref), (four_ref, wpb_ref), (brc_scr, wpc_ref), (brd_scr, wpd_ref))
    y = None
    for k, (br_ref, wp_ref) in enumerate(branches):
        gate = jax.nn.sigmoid(jnp.dot(h, wg_ref[k], preferred_element_type=F32))
        term = gate * jnp.dot(br_ref[...].astype(BF16), wp_ref[...], preferred_element_type=F32)
        y = term if y is None else y + term
    o_ref[...] = y.astype(o_ref.dtype)


def _merge(h, proj, attn, four, wg, wpa, wpb, wpc, wpd, wpool, layer, pscale, convw, seq_len, tm, tn):
    t, d = h.shape
    tiles_per_seq = seq_len // tm
    n_halo = t // HALO

    rel = np.arange(tm + 2 * HALO)[None, :] - HALO - np.arange(tm)[:, None]
    band = jnp.asarray(np.stack([(rel >= -(w // 2)) & (rel < w // 2) for w in POOL_WINDOWS]), BF16)

    def halo_specs(col):
        return [pl.BlockSpec((tm, MIX_WIDTH), lambda i, j: (i, col)),
                pl.BlockSpec((HALO, MIX_WIDTH), lambda i, j: (jnp.maximum(i * (tm // HALO) - 1, 0), col)),
                pl.BlockSpec((HALO, MIX_WIDTH), lambda i, j: (jnp.minimum((i + 1) * (tm // HALO), n_halo - 1), col))]

    in_specs = [pl.BlockSpec((tm, d), lambda i, j: (i, 0))]
    in_specs += halo_specs(COL_POOL) + halo_specs(COL_SV) + halo_specs(COL_SC)
    in_specs += [pl.BlockSpec((tm, MIX_WIDTH), lambda i, j: (i, COL_SB)),
                 pl.BlockSpec((tm, NA_WIDTH), lambda i, j: (i, 0)),
                 pl.BlockSpec((tm, MIX_WIDTH), lambda i, j: (i, 0)),
                 _wspec(layer, (4, d, tn), lambda i, j: (0, 0, j)),
                 _wspec(layer, (NA_WIDTH, tn), lambda i, j: (0, j)),
                 _wspec(layer, (MIX_WIDTH, tn), lambda i, j: (0, j)),
                 _wspec(layer, (MIX_WIDTH, tn), lambda i, j: (0, j)),
                 _wspec(layer, (MIX_WIDTH, tn), lambda i, j: (0, j)),
                 pl.BlockSpec(band.shape, lambda i, j: (0, 0, 0)),
                 _wspec(layer, (len(POOL_WINDOWS), GROUP_DIM, GROUP_DIM), lambda i, j: (0, 0, 0)),
                 pl.BlockSpec((1, MIX_WIDTH), lambda i, j: (0, 0)),
                 pl.BlockSpec((3, MIX_WIDTH), lambda i, j: (0, 0))]
    return pl.pallas_call(
        functools.partial(_merge_kernel, tiles_per_seq=tiles_per_seq, seq_len=seq_len),
        grid=(t // tm, d // tn),
        in_specs=in_specs,
        out_specs=pl.BlockSpec((tm, tn), lambda i, j: (i, j)),
        out_shape=jax.ShapeDtypeStruct((t, d), BF16),
        scratch_shapes=[pltpu.VMEM((tm, MIX_WIDTH), BF16), pltpu.VMEM((tm, MIX_WIDTH), BF16),
                        pltpu.VMEM((tm + 2 * HALO, MIX_WIDTH), F32)],
        compiler_params=_params(2),
        name="gated_merge",
    )(h, *([proj] * 10), attn, four, wg, wpa, wpb, wpc, wpd, band, wpool, pscale, convw)


def _resid_proj_kernel(a_ref, w_ref, x_ref, g_ref, o_ref):
    o_ref[...] = x_ref[...] + g_ref[...] * jnp.dot(a_ref[...], w_ref[...], preferred_element_type=F32)


def _resid_proj(a, w, layer, x2, modt, chunk, modrow, tm, tn):
    t, k = a.shape
    row_fn = _row_fn(modrow, tm)
    d = w.shape[2]
    return pl.pallas_call(
        _resid_proj_kernel,
        grid=(d // tn, t // tm),
        in_specs=[pl.BlockSpec((tm, k), lambda j, i: (i, 0)),
                  _wspec(layer, (k, tn), lambda j, i: (0, j)),
                  pl.BlockSpec((tm, tn), lambda j, i: (i, j)),
                  pl.BlockSpec((None, 1, tn), lambda j, i: (row_fn(i) * N_MOD + chunk, 0, j))],
        out_specs=pl.BlockSpec((tm, tn), lambda j, i: (i, j)),
        out_shape=jax.ShapeDtypeStruct((t, d), F32),
        compiler_params=_params(2),
        name="residual_projection",
    )(a, w, x2, modt)


def _resid_norm_kernel(a_ref, w_ref, x_ref, g_ref, sh_ref, sc_ref, nw_ref, o_ref, ho_ref, xa_scr, xb_scr):
    i = pl.program_id(0)

    @pl.when(i == 0)
    def _():
        xb_scr[...] = jnp.zeros(xb_scr.shape, xb_scr.dtype)

    def step(cur_scr, prev_scr):
        _norm_mod_rows(ho_ref, prev_scr, nw_ref[...] * (1.0 + sc_ref[...]), sh_ref[...])
        x1 = x_ref[...] + g_ref[...] * jnp.dot(a_ref[...], w_ref[...], preferred_element_type=F32)
        o_ref[...] = x1
        cur_scr[...] = x1

    @pl.when(i % 2 == 0)
    def _():
        step(xa_scr, xb_scr)

    @pl.when(i % 2 == 1)
    def _():
        step(xb_scr, xa_scr)


def _resid_norm(a, w, layer, x2, modt, modrow, nw, tm):
    t, k = a.shape
    d = w.shape[2]
    nt = t // tm
    row_fn = _row_fn(modrow, tm)
    cur = lambda i: jnp.minimum(i, nt - 1)
    prev = lambda i: jnp.maximum(i - 1, 0)
    mod_spec = lambda chunk, tile: pl.BlockSpec((None, 1, d), lambda i: (row_fn(tile(i)) * N_MOD + chunk, 0, 0))
    return pl.pallas_call(
        _resid_norm_kernel,
        grid=(nt + 1,),
        in_specs=[pl.BlockSpec((tm, k), lambda i: (cur(i), 0)),
                  _wspec(layer, (k, d), lambda i: (0, 0), pipeline_mode=pl.Buffered(1)),
                  pl.BlockSpec((tm, d), lambda i: (cur(i), 0)),
                  mod_spec(2, cur), mod_spec(3, prev), mod_spec(4, prev),
                  pl.BlockSpec((1, d), lambda i: (0, 0))],
        out_specs=[pl.BlockSpec((tm, d), lambda i: (cur(i), 0)),
                   pl.BlockSpec((tm, d), lambda i: (prev(i), 0))],
        out_shape=[jax.ShapeDtypeStruct((t, d), F32), jax.ShapeDtypeStruct((t, d), BF16)],
        scratch_shapes=[pltpu.VMEM((tm, d), F32), pltpu.VMEM((tm, d), F32)],
        compiler_params=_params(1),
        name="residual_projection_norm",
    )(a, w, x2, modt, modt, modt, nw)


def _ffn_up_kernel(h_ref, wg_ref, wu_ref, o_ref):
    h = h_ref[...]
    gt = jnp.dot(h, wg_ref[...], preferred_element_type=F32)
    up = jnp.dot(h, wu_ref[...], preferred_element_type=F32)
    o_ref[...] = (gt * jax.nn.sigmoid(gt) * up).astype(o_ref.dtype)


def _ffn_up(h, w_g, w_u, layer, tm, tn):
    t, d = h.shape
    n = w_g.shape[2]
    return pl.pallas_call(
        _ffn_up_kernel,
        grid=(t // tm, n // tn),
        in_specs=[pl.BlockSpec((tm, d), lambda i, j: (i, 0)),
                  _wspec(layer, (d, tn), lambda i, j: (0, j)),
                  _wspec(layer, (d, tn), lambda i, j: (0, j))],
        out_specs=pl.BlockSpec((tm, tn), lambda i, j: (i, j)),
        out_shape=jax.ShapeDtypeStruct((t, n), BF16),
        compiler_params=_params(2),
        name="ffn_up",
    )(h, w_g, w_u)


def _stream_layer(x2, modt, modrow, seq_len, tm, lw, mixers):
    l = lw["layer"]
    proj, h = _in_proj(x2, modt, modrow, lw["norm1"], lw["w_in"], l, lw["qw"], lw["kw"], min(tm, 512))
    attn, four = mixers(proj)
    y = _merge(h, proj, attn, four, lw["w_gate"], lw["w_pa"], lw["w_pb"], lw["w_pc"],
               lw["w_pd"], lw["w_pool"], l, lw["pool_scale"], lw["conv_w"], seq_len, min(tm, 512), 512)
    x2, h2 = _resid_norm(y, lw["w_o"], l, x2, modt, modrow, lw["norm2"], min(tm, 512))
    a = _ffn_up(h2, lw["w_ffn_gate"], lw["w_ffn_up"], l, min(2 * tm, x2.shape[0]), 512)
    return _resid_proj(a, lw["w_ffn_down"], l, x2, modt, 5, modrow, min(tm, 512), min(x2.shape[1], 1024))


def kernel(x, c, ctx, c_ctx, w_mod, b_mod, norm1_w, w_in, q_norm_w, k_norm_w, rpb, w_pool, pool_scale, conv_w,
           w_gate, w_pa, w_pb, w_pc, w_pd, w_o, norm2_w, w_ffn_gate, w_ffn_up, w_ffn_down):
    batch, seq, d = x.shape
    ctx_len = ctx.shape[1]
    depth = w_mod.shape[0]
    rows = seq // GRID_W
    assert seq % (NA_QROWS * GRID_W) == 0 and rows >= NA_KROWS and batch + 1 <= MOD_ROWS

    cvec = jnp.zeros((MOD_ROWS, d), F32).at[:batch].set(c).at[batch].set(c_ctx)
    mod = _modulation(cvec, w_mod, b_mod)

    tm_lat = 1024
    lat_row = (0, seq)
    ctx_row = (batch, batch * ctx_len)

    x2 = x.reshape(batch * seq, d)
    xc = ctx.reshape(batch * ctx_len, d)
    scale = HEAD_DIM ** -0.5
    stacked = {name: w.astype(BF16) for name, w in (
        ("w_in", w_in), ("w_gate", w_gate), ("w_pa", w_pa), ("w_pb", w_pb), ("w_pc", w_pc), ("w_pd", w_pd),
        ("w_o", w_o), ("w_pool", w_pool), ("w_ffn_gate", w_ffn_gate), ("w_ffn_up", w_ffn_up),
        ("w_ffn_down", w_ffn_down))}
    for l in range(depth):
        lw = dict(stacked)
        lw.update({
            "layer": l,
            "norm1": norm1_w[l].reshape(1, d), "norm2": norm2_w[l].reshape(1, d),
            "qw": (q_norm_w[l] * scale).reshape(1, HEAD_DIM), "kw": k_norm_w[l].reshape(1, HEAD_DIM),
            "pool_scale": pool_scale[l].reshape(1, MIX_WIDTH), "conv_w": conv_w[l],
        })
        modt = mod[l].reshape(MOD_ROWS * N_MOD, 1, d)
        bias = _na_bias_tables(rpb[l], rows)

        if l == depth - 1:
            proj_c, _ = _in_proj(xc, modt, ctx_row, lw["norm1"], lw["w_in"], l, lw["qw"], lw["kw"], ctx_len)
        else:
            holder = {}

            def ctx_mixers(proj):
                holder["proj"] = proj
                return _context_attention(proj, batch), _context_fourier(proj, batch)

            xc = _stream_layer(xc, modt, ctx_row, ctx_len, ctx_len, lw, ctx_mixers)
            proj_c = holder["proj"]

        def lat_mixers(proj):
            return (_neighbourhood_attention(proj, proj_c, bias, batch, seq),
                    _fourier_mix(proj, batch, seq))

        x2 = _stream_layer(x2, modt, lat_row, seq, tm_lat, lw, lat_mixers)
    return x2.reshape(batch, seq, d)
```

```python
import functools
import math

import numpy as np
import jax
import jax.numpy as jnp
from jax import lax
from jax.experimental import pallas as pl
from jax.experimental.pallas import tpu as pltpu

F32 = jnp.float32
BF16 = jnp.bfloat16

GRID_W = 64
NA_HEADS = 8
HEAD_DIM = 128
NA_WIDTH = NA_HEADS * HEAD_DIM
NA_KH = 8
NA_KW = 16
NA_QROWS = 4
NA_KROWS = 12
GROUP_DIM = 128
FT_GROUPS = 4
POOL_WINDOWS = (2, 4, 8, 16)
MIX_WIDTH = 512
N_MOD = 6
RMS_EPS = 1e-6
BF16_ROWS = 16
HALO = 64
NORM_ROWS = 2 * BF16_ROWS
MOD_ROWS = 8
NEG_BIG = -1e30

COL_FT, COL_POOL, COL_SV, COL_SB, COL_SC = 6, 7, 8, 9, 10

VMEM_LIMIT = 56 * 1024 * 1024


def _params(n_axes):
    return pltpu.CompilerParams(dimension_semantics=("arbitrary",) * n_axes,
                                vmem_limit_bytes=VMEM_LIMIT)


def _mod_kernel(c_ref, w_ref, b_ref, o_ref):
    cv = c_ref[...]
    s = cv * jax.nn.sigmoid(cv)
    o_ref[...] = jnp.dot(s.astype(BF16), w_ref[...].astype(BF16),
                         preferred_element_type=F32) + b_ref[...]


def _modulation(cvec, w_mod, b_mod):
    depth, d, n = w_mod.shape
    tn = 1024
    return pl.pallas_call(
        _mod_kernel,
        grid=(depth, n // tn),
        in_specs=[
            pl.BlockSpec((MOD_ROWS, d), lambda l, j: (0, 0)),
            pl.BlockSpec((None, d, tn), lambda l, j: (l, 0, j)),
            pl.BlockSpec((None, 1, tn), lambda l, j: (l, 0, j)),
        ],
        out_specs=pl.BlockSpec((None, MOD_ROWS, tn), lambda l, j: (l, 0, j)),
        out_shape=jax.ShapeDtypeStruct((depth, MOD_ROWS, n), F32),
        compiler_params=_params(2),
        name="modulation",
    )(cvec, w_mod, b_mod.reshape(depth, 1, n))


def _row_fn(modrow, tm):
    base, tokens_per_row = modrow
    return lambda i: base + (i * tm) // tokens_per_row


def _wspec(layer, block, index_map, **kwargs):
    return pl.BlockSpec((None,) + tuple(block), lambda *g: (layer,) + tuple(index_map(*g)), **kwargs)


def _norm_mod_rows(h_ref, x_ref, w, sh, chunk=NORM_ROWS):
    for r in range(0, x_ref.shape[0], chunk):
        xf = x_ref[r:r + chunk, :]
        ms = jnp.mean(xf * xf, axis=-1, keepdims=True)
        h_ref[r:r + chunk, :] = (xf * lax.rsqrt(ms + RMS_EPS) * w + sh).astype(BF16)


def _in_proj_kernel(x_ref, sh_ref, sc_ref, nw_ref, w_ref, qw_ref, kw_ref, o_ref, ho_ref, h_scr):
    i = pl.program_id(0)

    @pl.when(i == 0)
    def _():
        h_scr[1] = jnp.zeros(h_scr.shape[1:], h_scr.dtype)

    h = h_scr[(i + 1) % 2]
    ho_ref[...] = h
    for c0 in range(0, w_ref.shape[1], MIX_WIDTH):
        acc = jnp.dot(h, w_ref[:, c0:c0 + MIX_WIDTH], preferred_element_type=F32)
        if c0 < 2 * NA_WIDTH:
            w = qw_ref[...] if c0 < NA_WIDTH else kw_ref[...]
            for hh in range(0, MIX_WIDTH, HEAD_DIM):
                a = acc[:, hh:hh + HEAD_DIM]
                ms = jnp.mean(a * a, axis=-1, keepdims=True)
                o_ref[:, c0 + hh:c0 + hh + HEAD_DIM] = (a * lax.rsqrt(ms + RMS_EPS) * w).astype(o_ref.dtype)
        else:
            o_ref[:, c0:c0 + MIX_WIDTH] = acc.astype(o_ref.dtype)
    _norm_mod_rows(h_scr.at[i % 2], x_ref, nw_ref[...] * (1.0 + sc_ref[...]), sh_ref[...])


def _in_proj(x2, modt, modrow, nw, w_in, layer, qw, kw, tm):
    t, d = x2.shape
    n = w_in.shape[2]
    nt = t // tm
    row_fn = _row_fn(modrow, tm)
    cur = lambda i: jnp.minimum(i, nt - 1)
    mod_spec = lambda chunk: pl.BlockSpec((None, 1, d), lambda i: (row_fn(cur(i)) * N_MOD + chunk, 0, 0))
    return pl.pallas_call(
        _in_proj_kernel,
        grid=(nt + 1,),
        in_specs=[
            pl.BlockSpec((tm, d), lambda i: (cur(i), 0)),
            mod_spec(0),
            mod_spec(1),
            pl.BlockSpec((1, d), lambda i: (0, 0)),
            _wspec(layer, (d, n), lambda i: (0, 0), pipeline_mode=pl.Buffered(1)),
            pl.BlockSpec((1, HEAD_DIM), lambda i: (0, 0)),
            pl.BlockSpec((1, HEAD_DIM), lambda i: (0, 0)),
        ],
        out_specs=[pl.BlockSpec((tm, n), lambda i: (jnp.maximum(i - 1, 0), 0)),
                   pl.BlockSpec((tm, d), lambda i: (jnp.maximum(i - 1, 0), 0))],
        out_shape=[jax.ShapeDtypeStruct((t, n), BF16), jax.ShapeDtypeStruct((t, d), BF16)],
        scratch_shapes=[pltpu.VMEM((2, tm, d), BF16)],
        compiler_params=_params(1),
        name="in_proj",
    )(x2, modt, modt, nw, w_in, qw, kw)


def _na_bias_tables(rpb, rows):
    cols = np.arange(GRID_W)
    cs = np.clip(cols - NA_KW // 2, 0, GRID_W - NA_KW)
    valid_c = (cols[None, :] >= cs[:, None]) & (cols[None, :] < cs[:, None] + NA_KW)
    pad = GRID_W - NA_KW
    rp = jnp.pad(rpb.astype(F32), ((0, 0), (0, 0), (pad, pad)))
    col_t = jnp.stack([rp[:, :, GRID_W - 1 - qc:2 * GRID_W - 1 - qc] for qc in range(GRID_W)], axis=2)
    tables = []
    for r0, kr0 in ((0, 0), (2 * NA_QROWS, 2 * NA_QROWS - NA_KH // 2), (rows - NA_QROWS, rows - NA_KROWS)):
        r = r0 + np.arange(NA_QROWS)
        rs = np.clip(r - NA_KH // 2, 0, rows - NA_KH)
        key_row = kr0 + np.arange(NA_KROWS)
        valid_r = (key_row[None, :] >= rs[:, None]) & (key_row[None, :] < rs[:, None] + NA_KH)
        row_off = np.clip(key_row[None, :] - r[:, None] + (NA_KH - 1), 0, 2 * NA_KH - 2)
        tab = jnp.concatenate([jnp.concatenate([col_t[:, int(row_off[qr, kr])] for kr in range(NA_KROWS)], axis=2)
                               for qr in range(NA_QROWS)], axis=1)
        valid = (valid_r[:, None, :, None] & valid_c[None, :, None, :]).reshape(tab.shape[1:])
        tables.append(jnp.where(valid[None], tab, NEG_BIG))
    return jnp.stack(tables)


def _qk(q, k):
    return lax.dot_general(q, k, (((1,), (1,)), ((), ())), preferred_element_type=F32)


def _lane_tiles(a, width=128):
    return [a[:, k:k + width] for k in range(0, a.shape[1], width)]


def _na_kernel(q_ref, k0_ref, k1_ref, k2_ref, v0_ref, v1_ref, v2_ref, kc_ref, vc_ref, bias_ref, o_ref):
    blk = k0_ref.shape[0]
    values = (vc_ref, v0_ref, v1_ref, v2_ref)

    def scores_of(h):
        hs = slice(h * HEAD_DIM, (h + 1) * HEAD_DIM)
        q = q_ref[:, hs]
        s_loc = [_qk(q, kr[:, hs]) + bias_ref[h, :, m * blk:(m + 1) * blk]
                 for m, kr in enumerate((k0_ref, k1_ref, k2_ref))]
        return [_qk(q, kc_ref[:, hs])] + s_loc

    nxt = scores_of(0)
    for h in range(NA_HEADS):
        hs = slice(h * HEAD_DIM, (h + 1) * HEAD_DIM)
        scores = nxt
        if h + 1 < NA_HEADS:
            nxt = scores_of(h + 1)
        mx = jnp.max(functools.reduce(jnp.maximum, [t for s in scores for t in _lane_tiles(s)]),
                     axis=-1, keepdims=True)
        probs = [jnp.exp(s - mx) for s in scores]
        den = jnp.sum(functools.reduce(jnp.add, [t for p in probs for t in _lane_tiles(p)]),
                      axis=-1, keepdims=True)
        o = None
        for p, vr in zip(probs, values):
            pv = jnp.dot(p.astype(BF16), vr[:, hs], preferred_element_type=F32)
            o = pv if o is None else o + pv
        o_ref[:, hs] = (o / den).astype(o_ref.dtype)


def _neighbourhood_attention(proj, proj_ctx, bias, batch, seq):
    blk = NA_QROWS * GRID_W
    nrb = seq // blk
    n_kblk = NA_KROWS // NA_QROWS
    ctx_len = proj_ctx.shape[0] // batch

    def kblock(rb):
        return jnp.clip(rb - 1, 0, nrb - n_kblk)

    def kv_spec(m, col):
        return pl.BlockSpec((blk, NA_WIDTH), lambda b, rb: (b * nrb + kblock(rb) + m, col))

    def variant(rb):
        return jnp.where(rb == 0, 0, jnp.where(rb == nrb - 1, 2, 1))

    in_specs = [pl.BlockSpec((blk, NA_WIDTH), lambda b, rb: (b * nrb + rb, 0))]
    in_specs += [kv_spec(m, 1) for m in range(n_kblk)] + [kv_spec(m, 2) for m in range(n_kblk)]
    in_specs += [pl.BlockSpec((ctx_len, NA_WIDTH), lambda b, rb: (b, 1)),
                 pl.BlockSpec((ctx_len, NA_WIDTH), lambda b, rb: (b, 2)),
                 pl.BlockSpec((None, NA_HEADS, blk, NA_KROWS * GRID_W), lambda b, rb: (variant(rb), 0, 0, 0))]
    return pl.pallas_call(
        _na_kernel,
        grid=(batch, nrb),
        in_specs=in_specs,
        out_specs=pl.BlockSpec((blk, NA_WIDTH), lambda b, rb: (b * nrb + rb, 0)),
        out_shape=jax.ShapeDtypeStruct((batch * seq, NA_WIDTH), BF16),
        compiler_params=_params(2),
        name="neighbourhood_attention",
    )(*([proj] * (1 + 2 * n_kblk)), proj_ctx, proj_ctx, bias)


def _ctx_attn_kernel(q_ref, k_ref, v_ref, o_ref):
    for h in range(NA_HEADS):
        hs = slice(h * HEAD_DIM, (h + 1) * HEAD_DIM)
        s = _qk(q_ref[:, hs], k_ref[:, hs])
        p = jnp.exp(s - jnp.max(s, axis=-1, keepdims=True))
        den = jnp.sum(p, axis=-1, keepdims=True)
        o = jnp.dot(p.astype(BF16), v_ref[:, hs], preferred_element_type=F32)
        o_ref[:, hs] = (o / den).astype(o_ref.dtype)


def _context_attention(proj_ctx, batch):
    ctx_len = proj_ctx.shape[0] // batch
    spec = lambda col: pl.BlockSpec((ctx_len, NA_WIDTH), lambda b: (b, col))
    return pl.pallas_call(
        _ctx_attn_kernel,
        grid=(batch,),
        in_specs=[spec(0), spec(1), spec(2)],
        out_specs=spec(0),
        out_shape=jax.ShapeDtypeStruct((proj_ctx.shape[0], NA_WIDTH), BF16),
        compiler_params=_params(1),
        name="context_attention",
    )(proj_ctx, proj_ctx, proj_ctx)


def _dft_cos_sin(n_out, n_in, period):
    ij = (np.arange(n_out, dtype=np.int64)[:, None] * np.arange(n_in, dtype=np.int64)[None, :]) % period
    ang = 2.0 * np.pi * ij.astype(np.float64) / period
    return np.cos(ang), np.sin(ang)


def _ft_stage_a_kernel(u_ref, mc_ref, m_ref, o_ref, xr_s, xi_s, yr_s, yi_s):
    c = GROUP_DIM
    n_a, bb = u_ref.shape[0], u_ref.shape[1]
    z = jnp.dot(u_ref[...].reshape(n_a * bb, c), mc_ref[...], preferred_element_type=F32)
    xr_s[...] = z[:, :c]
    xi_s[...] = z[:, c:]
    for k in range(bb):
        rows = pl.ds(k, n_a, stride=bb)
        xr, xi = xr_s[rows, :], xi_s[rows, :]
        rhs = jnp.concatenate([jnp.concatenate([xr, xi], axis=1), jnp.concatenate([-xi, xr], axis=1)], axis=0)
        y = jnp.dot(m_ref[k], rhs.astype(BF16), preferred_element_type=F32)
        yr_s[rows, :] = y[:, :c]
        yi_s[rows, :] = y[:, c:]
    for a in range(n_a):
        o_ref[a] = jnp.concatenate([yr_s[a * bb:(a + 1) * bb, :], yi_s[a * bb:(a + 1) * bb, :]], axis=1)


def _ft_stage_b_kernel(x_ref, m_ref, o_ref, o_s):
    c = GROUP_DIM
    aa, n_b = x_ref.shape[0], x_ref.shape[1]
    m = m_ref[...]
    for k in range(aa):
        y = x_ref[k]
        rhs = jnp.concatenate([y[:, :c], y[:, c:]], axis=0).astype(BF16)
        o_s[pl.ds(k, n_b, stride=aa), :] = jnp.dot(m, rhs, preferred_element_type=F32)
    for b in range(n_b):
        o_ref[b] = o_s[b * aa:(b + 1) * aa, :]


def _fourier_mix(proj, batch, seq):
    t = batch * seq
    n_b = GROUP_DIM
    n_a = seq // n_b
    g, c = FT_GROUPS, GROUP_DIM
    sub = 16

    cc, sc = _dft_cos_sin(c, c, c)
    m_c = jnp.asarray(np.concatenate([cc, -sc], axis=1), BF16)
    ca, sa = _dft_cos_sin(n_a, seq, seq)
    ca, sa = ca.reshape(n_a, n_a, n_b), sa.reshape(n_a, n_a, n_b)
    m_a = jnp.asarray(np.concatenate([ca, -sa], axis=1).transpose(2, 0, 1), BF16)
    cb, sb = _dft_cos_sin(n_b, n_b, n_b)
    m_b = jnp.asarray(np.concatenate([cb, sb], axis=1) / math.sqrt(seq * c), BF16)

    ft_col = COL_FT * MIX_WIDTH // c
    y = pl.pallas_call(
        _ft_stage_a_kernel,
        grid=(n_b // sub, batch, g),
        in_specs=[pl.BlockSpec((None, n_a, sub, c), lambda j, b, k: (b, 0, j, ft_col + k)),
                  pl.BlockSpec((c, 2 * c), lambda j, b, k: (0, 0)),
                  pl.BlockSpec((sub, n_a, 2 * n_a), lambda j, b, k: (j, 0, 0))],
        out_specs=pl.BlockSpec((None, n_a, sub, 2 * c), lambda j, b, k: (b, 0, j, k)),
        out_shape=jax.ShapeDtypeStruct((batch, n_a, n_b, 2 * g * c), F32),
        scratch_shapes=[pltpu.VMEM((n_a * sub, c), F32)] * 4,
        compiler_params=_params(3),
        name="fourier_stage_a",
    )(proj.reshape(batch, n_a, n_b, proj.shape[1]), m_c, m_a)

    o = pl.pallas_call(
        _ft_stage_b_kernel,
        grid=(batch, g, n_a // sub),
        in_specs=[pl.BlockSpec((None, sub, n_b, 2 * c), lambda b, k, j: (b, j, 0, k)),
                  pl.BlockSpec((n_b, 2 * n_b), lambda b, k, j: (0, 0))],
        out_specs=pl.BlockSpec((None, n_b, sub, c), lambda b, k, j: (b, 0, j, k)),
        out_shape=jax.ShapeDtypeStruct((batch, n_b, n_a, g * c), F32),
        scratch_shapes=[pltpu.VMEM((n_b * sub, c), F32)],
        compiler_params=_params(3),
        name="fourier_stage_b",
    )(y, m_b)
    return o.reshape(t, g * c)


def _ctx_fourier_kernel(u_ref, mc_ref, ml_ref, o_ref):
    mc = mc_ref[...]
    ml = ml_ref[...]
    for g in range(FT_GROUPS):
        z = jnp.dot(u_ref[:, g * GROUP_DIM:(g + 1) * GROUP_DIM], mc, preferred_element_type=F32)
        stack = jnp.concatenate([z[:, :GROUP_DIM], z[:, GROUP_DIM:]], axis=0).astype(BF16)
        o_ref[:, g * GROUP_DIM:(g + 1) * GROUP_DIM] = jnp.dot(ml, stack, preferred_element_type=F32).astype(o_ref.dtype)


def _context_fourier(proj_ctx, batch):
    ctx_len = proj_ctx.shape[0] // batch
    c = GROUP_DIM
    cc, sc = _dft_cos_sin(c, c, c)
    m_c = jnp.asarray(np.concatenate([cc, -sc], axis=1), BF16)
    cl, sl = _dft_cos_sin(ctx_len, ctx_len, ctx_len)
    m_l = jnp.asarray(np.concatenate([cl, sl], axis=1) / math.sqrt(ctx_len * c), BF16)
    return pl.pallas_call(
        _ctx_fourier_kernel,
        grid=(batch,),
        in_specs=[pl.BlockSpec((ctx_len, MIX_WIDTH), lambda b: (b, COL_FT)),
                  pl.BlockSpec((c, 2 * c), lambda b: (0, 0)),
                  pl.BlockSpec((ctx_len, 2 * ctx_len), lambda b: (0, 0))],
        out_specs=pl.BlockSpec((ctx_len, MIX_WIDTH), lambda b: (b, 0)),
        out_shape=jax.ShapeDtypeStruct((proj_ctx.shape[0], MIX_WIDTH), BF16),
        compiler_params=_params(1),
        name="context_fourier",
    )(proj_ctx, m_c, m_l)


def _merge_kernel(h_ref, pool_ref, pool_p_ref, pool_n_ref, sv_ref, sv_p_ref, sv_n_ref, scv_ref, scv_p_ref, scv_n_ref,
                  sb_ref, attn_ref, four_ref, wg_ref, wpa_ref, wpb_ref, wpc_ref, wpd_ref,
                  band_ref, wpool_ref, pscale_ref, convw_ref, o_ref,
                  brc_scr, brd_scr, ext_scr, *, tiles_per_seq, seq_len):
    i = pl.program_id(0)
    j = pl.program_id(1)
    tm = h_ref.shape[0]

    @pl.when(j == 0)
    def _():
        first = (i % tiles_per_seq) == 0
        last = (i % tiles_per_seq) == tiles_per_seq - 1
        pos = (i % tiles_per_seq) * tm + lax.broadcasted_iota(jnp.int32, (tm, 1), 0)

        zero = jnp.zeros(pool_p_ref.shape, pool_p_ref.dtype)
        ext = jnp.concatenate([jnp.where(first, zero, pool_p_ref[...]), pool_ref[...],
                               jnp.where(last, zero, pool_n_ref[...])], axis=0)
        for g, w in enumerate(POOL_WINDOWS):
            cs = slice(g * GROUP_DIM, (g + 1) * GROUP_DIM)
            win = jnp.dot(band_ref[g], ext[:, cs], preferred_element_type=F32)
            cnt = (jnp.minimum(pos + w // 2, seq_len) - jnp.maximum(pos - w // 2, 0)).astype(F32)
            p = win * (1.0 / cnt) - pool_ref[:, cs].astype(F32)
            yg = jnp.dot(p.astype(BF16), wpool_ref[g], preferred_element_type=F32) * pscale_ref[:, cs]
            brc_scr[:, cs] = yg.astype(BF16)

        ext_scr[0:HALO, :] = jnp.where(first, 0.0, sv_p_ref[...].astype(F32) * scv_p_ref[...].astype(F32))
        ext_scr[HALO:HALO + tm, :] = sv_ref[...].astype(F32) * scv_ref[...].astype(F32)
        ext_scr[HALO + tm:HALO + tm + HALO, :] = jnp.where(last, 0.0,
                                                           sv_n_ref[...].astype(F32) * scv_n_ref[...].astype(F32))
        y = (ext_scr[HALO - 1:HALO - 1 + tm, :] * convw_ref[0:1, :]
             + ext_scr[HALO:HALO + tm, :] * convw_ref[1:2, :]
             + ext_scr[HALO + 1:HALO + 1 + tm, :] * convw_ref[2:3, :])
        brd_scr[...] = (sb_ref[...].astype(F32) * y).astype(BF16)

    h = h_ref[...]
    branches = ((attn_ref, wpa_ref), (four_ref, wpb_ref), (brc_scr, wpc_ref), (brd_scr, wpd_ref))
    y = None
    for k, (br_ref, wp_ref) in enumerate(branches):
        gate = jax.nn.sigmoid(jnp.dot(h, wg_ref[k], preferred_element_type=F32))
        term = gate * jnp.dot(br_ref[...].astype(BF16), wp_ref[...], preferred_element_type=F32)
        y = term if y is None else y + term
    o_ref[...] = y.astype(o_ref.dtype)


def _merge(h, proj, attn, four, wg, wpa, wpb, wpc, wpd, wpool, layer, pscale, convw, seq_len, tm, tn):
    t, d = h.shape
    tiles_per_seq = seq_len // tm
    n_halo = t // HALO

    rel = np.arange(tm + 2 * HALO)[None, :] - HALO - np.arange(tm)[:, None]
    band = jnp.asarray(np.stack([(rel >= -(w // 2)) & (rel < w // 2) for w in POOL_WINDOWS]), BF16)

    def halo_specs(col):
        return [pl.BlockSpec((tm, MIX_WIDTH), lambda i, j: (i, col)),
                pl.BlockSpec((HALO, MIX_WIDTH), lambda i, j: (jnp.maximum(i * (tm // HALO) - 1, 0), col)),
                pl.BlockSpec((HALO, MIX_WIDTH), lambda i, j: (jnp.minimum((i + 1) * (tm // HALO), n_halo - 1), col))]

    in_specs = [pl.BlockSpec((tm, d), lambda i, j: (i, 0))]
    in_specs += halo_specs(COL_POOL) + halo_specs(COL_SV) + halo_specs(COL_SC)
    in_specs += [pl.BlockSpec((tm, MIX_WIDTH), lambda i, j: (i, COL_SB)),
                 pl.BlockSpec((tm, NA_WIDTH), lambda i, j: (i, 0)),
                 pl.BlockSpec((tm, MIX_WIDTH), lambda i, j: (i, 0)),
                 _wspec(layer, (4, d, tn), lambda i, j: (0, 0, j)),
                 _wspec(layer, (NA_WIDTH, tn), lambda i, j: (0, j)),
                 _wspec(layer, (MIX_WIDTH, tn), lambda i, j: (0, j)),
                 _wspec(layer, (MIX_WIDTH, tn), lambda i, j: (0, j)),
                 _wspec(layer, (MIX_WIDTH, tn), lambda i, j: (0, j)),
                 pl.BlockSpec(band.shape, lambda i, j: (0, 0, 0)),
                 _wspec(layer, (len(POOL_WINDOWS), GROUP_DIM, GROUP_DIM), lambda i, j: (0, 0, 0)),
                 pl.BlockSpec((1, MIX_WIDTH), lambda i, j: (0, 0)),
                 pl.BlockSpec((3, MIX_WIDTH), lambda i, j: (0, 0))]
    return pl.pallas_call(
        functools.partial(_merge_kernel, tiles_per_seq=tiles_per_seq, seq_len=seq_len),
        grid=(t // tm, d // tn),
        in_specs=in_specs,
        out_specs=pl.BlockSpec((tm, tn), lambda i, j: (i, j)),
        out_shape=jax.ShapeDtypeStruct((t, d), BF16),
        scratch_shapes=[pltpu.VMEM((tm, MIX_WIDTH), BF16), pltpu.VMEM((tm, MIX_WIDTH), BF16),
                        pltpu.VMEM((tm + 2 * HALO, MIX_WIDTH), F32)],
        compiler_params=_params(2),
        name="gated_merge",
    )(h, *([proj] * 10), attn, four, wg, wpa, wpb, wpc, wpd, band, wpool, pscale, convw)


def _resid_proj_kernel(a_ref, w_ref, x_ref, g_ref, o_ref):
    o_ref[...] = x_ref[...] + g_ref[...] * jnp.dot(a_ref[...], w_ref[...], preferred_element_type=F32)


def _resid_proj(a, w, layer, x2, modt, chunk, modrow, tm, tn):
    t, k = a.shape
    row_fn = _row_fn(modrow, tm)
    d = w.shape[2]
    return pl.pallas_call(
        _resid_proj_kernel,
        grid=(d // tn, t // tm),
        in_specs=[pl.BlockSpec((tm, k), lambda j, i: (i, 0)),
                  _wspec(layer, (k, tn), lambda j, i: (0, j)),
                  pl.BlockSpec((tm, tn), lambda j, i: (i, j)),
                  pl.BlockSpec((None, 1, tn), lambda j, i: (row_fn(i) * N_MOD + chunk, 0, j))],
        out_specs=pl.BlockSpec((tm, tn), lambda j, i: (i, j)),
        out_shape=jax.ShapeDtypeStruct((t, d), F32),
        compiler_params=_params(2),
        name="residual_projection",
    )(a, w, x2, modt)


def _resid_norm_kernel(a_ref, w_ref, x_ref, g_ref, sh_ref, sc_ref, nw_ref, o_ref, ho_ref, xa_scr, xb_scr):
    i = pl.program_id(0)

    @pl.when(i == 0)
    def _():
        xb_scr[...] = jnp.zeros(xb_scr.shape, xb_scr.dtype)

    def step(cur_scr, prev_scr):
        _norm_mod_rows(ho_ref, prev_scr, nw_ref[...] * (1.0 + sc_ref[...]), sh_ref[...])
        x1 = x_ref[...] + g_ref[...] * jnp.dot(a_ref[...], w_ref[...], preferred_element_type=F32)
        o_ref[...] = x1
        cur_scr[...] = x1

    @pl.when(i % 2 == 0)
    def _():
        step(xa_scr, xb_scr)

    @pl.when(i % 2 == 1)
    def _():
        step(xb_scr, xa_scr)


def _resid_norm(a, w, layer, x2, modt, modrow, nw, tm):
    t, k = a.shape
    d = w.shape[2]
    nt = t // tm
    row_fn = _row_fn(modrow, tm)
    cur = lambda i: jnp.minimum(i, nt - 1)
    prev = lambda i: jnp.maximum(i - 1, 0)
    mod_spec = lambda chunk, tile: pl.BlockSpec((None, 1, d), lambda i: (row_fn(tile(i)) * N_MOD + chunk, 0, 0))
    return pl.pallas_call(
        _resid_norm_kernel,
        grid=(nt + 1,),
        in_specs=[pl.BlockSpec((tm, k), lambda i: (cur(i), 0)),
                  _wspec(layer, (k, d), lambda i: (0, 0), pipeline_mode=pl.Buffered(1)),
                  pl.BlockSpec((tm, d), lambda i: (cur(i), 0)),
                  mod_spec(2, cur), mod_spec(3, prev), mod_spec(4, prev),
                  pl.BlockSpec((1, d), lambda i: (0, 0))],
        out_specs=[pl.BlockSpec((tm, d), lambda i: (cur(i), 0)),
                   pl.BlockSpec((tm, d), lambda i: (prev(i), 0))],
        out_shape=[jax.ShapeDtypeStruct((t, d), F32), jax.ShapeDtypeStruct((t, d), BF16)],
        scratch_shapes=[pltpu.VMEM((tm, d), F32), pltpu.VMEM((tm, d), F32)],
        compiler_params=_params(1),
        name="residual_projection_norm",
    )(a, w, x2, modt, modt, modt, nw)


def _ffn_up_kernel(h_ref, wg_ref, wu_ref, o_ref):
    h = h_ref[...]
    gt = jnp.dot(h, wg_ref[...], preferred_element_type=F32)
    up = jnp.dot(h, wu_ref[...], preferred_element_type=F32)
    o_ref[...] = (gt * jax.nn.sigmoid(gt) * up).astype(o_ref.dtype)


def _ffn_up(h, w_g, w_u, layer, tm, tn):
    t, d = h.shape
    n = w_g.shape[2]
    return pl.pallas_call(
        _ffn_up_kernel,
        grid=(t // tm, n // tn),
        in_specs=[pl.BlockSpec((tm, d), lambda i, j: (i, 0)),
                  _wspec(layer, (d, tn), lambda i, j: (0, j)),
                  _wspec(layer, (d, tn), lambda i, j: (0, j))],
        out_specs=pl.BlockSpec((tm, tn), lambda i, j: (i, j)),
        out_shape=jax.ShapeDtypeStruct((t, n), BF16),
        compiler_params=_params(2),
        name="ffn_up",
    )(h, w_g, w_u)


def _stream_layer(x2, modt, modrow, seq_len, tm, lw, mixers):
    l = lw["layer"]
    proj, h = _in_proj(x2, modt, modrow, lw["norm1"], lw["w_in"], l, lw["qw"], lw["kw"], min(tm, 512))
    attn, four = mixers(proj)
    y = _merge(h, proj, attn, four, lw["w_gate"], lw["w_pa"], lw["w_pb"], lw["w_pc"],
               lw["w_pd"], lw["w_pool"], l, lw["pool_scale"], lw["conv_w"], seq_len, min(tm, 512), 512)
    x2, h2 = _resid_norm(y, lw["w_o"], l, x2, modt, modrow, lw["norm2"], min(tm, 512))
    a = _ffn_up(h2, lw["w_ffn_gate"], lw["w_ffn_up"], l, min(2 * tm, x2.shape[0]), 512)
    return _resid_proj(a, lw["w_ffn_down"], l, x2, modt, 5, modrow, min(tm, 512), min(x2.shape[1], 1024))


def kernel(x, c, ctx, c_ctx, w_mod, b_mod, norm1_w, w_in, q_norm_w, k_norm_w, rpb, w_pool, pool_scale, conv_w,
           w_gate, w_pa, w_pb, w_pc, w_pd, w_o, norm2_w, w_ffn_gate, w_ffn_up, w_ffn_down):
    batch, seq, d = x.shape
    ctx_len = ctx.shape[1]
    depth = w_mod.shape[0]
    rows = seq // GRID_W
    assert seq % (NA_QROWS * GRID_W) == 0 and rows >= NA_KROWS and batch + 1 <= MOD_ROWS

    cvec = jnp.zeros((MOD_ROWS, d), F32).at[:batch].set(c).at[batch].set(c_ctx)
    mod = _modulation(cvec, w_mod, b_mod)

    tm_lat = 1024
    lat_row = (0, seq)
    ctx_row = (batch, batch * ctx_len)

    x2 = x.reshape(batch * seq, d)
    xc = ctx.reshape(batch * ctx_len, d)
    scale = HEAD_DIM ** -0.5
    stacked = {name: w.astype(BF16) for name, w in (
        ("w_in", w_in), ("w_gate", w_gate), ("w_pa", w_pa), ("w_pb", w_pb), ("w_pc", w_pc), ("w_pd", w_pd),
        ("w_o", w_o), ("w_pool", w_pool), ("w_ffn_gate", w_ffn_gate), ("w_ffn_up", w_ffn_up),
        ("w_ffn_down", w_ffn_down))}
    for l in range(depth):
        lw = dict(stacked)
        lw.update({
            "layer": l,
            "norm1": norm1_w[l].reshape(1, d), "norm2": norm2_w[l].reshape(1, d),
            "qw": (q_norm_w[l] * scale).reshape(1, HEAD_DIM), "kw": k_norm_w[l].reshape(1, HEAD_DIM),
            "pool_scale": pool_scale[l].reshape(1, MIX_WIDTH), "conv_w": conv_w[l],
        })
        modt = mod[l].reshape(MOD_ROWS * N_MOD, 1, d)
        bias = _na_bias_tables(rpb[l], rows)

        if l == depth - 1:
            proj_c, _ = _in_proj(xc, modt, ctx_row, lw["norm1"], lw["w_in"], l, lw["qw"], lw["kw"], ctx_len)
        else:
            holder = {}

            def ctx_mixers(proj):
                holder["proj"] = proj
                return _context_attention(proj, batch), _context_fourier(proj, batch)

            xc = _stream_layer(xc, modt, ctx_row, ctx_len, ctx_len, lw, ctx_mixers)
            proj_c = holder["proj"]

        def lat_mixers(proj):
            return (_neighbourhood_attention(proj, proj_c, bias, batch, seq),
                    _fourier_mix(proj, batch, seq))

        x2 = _stream_layer(x2, modt, lat_row, seq, tm_lat, lw, lat_mixers)
    return x2.reshape(batch, seq, d)
```

```python
import functools
import math

import numpy as np
import jax
import jax.numpy as jnp
from jax import lax
from jax.experimental import pallas as pl
from jax.experimental.pallas import tpu as pltpu

F32 = jnp.float32
BF16 = jnp.bfloat16

GRID_W = 64
NA_HEADS = 8
HEAD_DIM = 128
NA_WIDTH = NA_HEADS * HEAD_DIM
NA_KH = 8
NA_KW = 16
NA_QROWS = 4
NA_KROWS = 12
GROUP_DIM = 128
FT_GROUPS = 4
POOL_WINDOWS = (2, 4, 8, 16)
MIX_WIDTH = 512
N_MOD = 6
RMS_EPS = 1e-6
BF16_ROWS = 16
HALO = 64
NORM_ROWS = 2 * BF16_ROWS
MOD_ROWS = 8
NEG_BIG = -1e30
LOG2_E = math.log2(math.e)

COL_FT, COL_POOL, COL_SV, COL_SB, COL_SC = 6, 7, 8, 9, 10

VMEM_LIMIT = 56 * 1024 * 1024


def _params(n_axes):
    return pltpu.CompilerParams(dimension_semantics=("arbitrary",) * n_axes,
                                vmem_limit_bytes=VMEM_LIMIT)


def _mod_kernel(c_ref, w_ref, b_ref, o_ref):
    cv = c_ref[...]
    s = cv * jax.nn.sigmoid(cv)
    o_ref[...] = jnp.dot(s.astype(BF16), w_ref[...].astype(BF16),
                         preferred_element_type=F32) + b_ref[...]


def _modulation(cvec, w_mod, b_mod):
    depth, d, n = w_mod.shape
    tn = 1024
    return pl.pallas_call(
        _mod_kernel,
        grid=(depth, n // tn),
        in_specs=[
            pl.BlockSpec((MOD_ROWS, d), lambda l, j: (0, 0)),
            pl.BlockSpec((None, d, tn), lambda l, j: (l, 0, j)),
            pl.BlockSpec((None, 1, tn), lambda l, j: (l, 0, j)),
        ],
        out_specs=pl.BlockSpec((None, MOD_ROWS, tn), lambda l, j: (l, 0, j)),
        out_shape=jax.ShapeDtypeStruct((depth, MOD_ROWS, n), F32),
        compiler_params=_params(2),
        name="modulation",
    )(cvec, w_mod, b_mod.reshape(depth, 1, n))


def _row_fn(modrow, tm):
    base, tokens_per_row = modrow
    return lambda i: base + (i * tm) // tokens_per_row


def _wspec(layer, block, index_map, **kwargs):
    return pl.BlockSpec((None,) + tuple(block), lambda *g: (layer,) + tuple(index_map(*g)), **kwargs)


def _norm_mod_rows(h_ref, x_ref, w, sh, chunk=NORM_ROWS):
    for r in range(0, x_ref.shape[0], chunk):
        xf = x_ref[r:r + chunk, :]
        ms = jnp.mean(xf * xf, axis=-1, keepdims=True)
        h_ref[r:r + chunk, :] = (xf * lax.rsqrt(ms + RMS_EPS) * w + sh).astype(BF16)


def _in_proj_kernel(x_ref, sh_ref, sc_ref, nw_ref, w_ref, qw_ref, kw_ref, o_ref, ho_ref, h_scr):
    i = pl.program_id(0)

    @pl.when(i == 0)
    def _():
        h_scr[1] = jnp.zeros(h_scr.shape[1:], h_scr.dtype)

    h = h_scr[(i + 1) % 2]
    ho_ref[...] = h
    for c0 in range(0, w_ref.shape[1], MIX_WIDTH):
        acc = jnp.dot(h, w_ref[:, c0:c0 + MIX_WIDTH], preferred_element_type=F32)
        if c0 < 2 * NA_WIDTH:
            w = qw_ref[...] if c0 < NA_WIDTH else kw_ref[...]
            for hh in range(0, MIX_WIDTH, HEAD_DIM):
                a = acc[:, hh:hh + HEAD_DIM]
                ms = jnp.mean(a * a, axis=-1, keepdims=True)
                o_ref[:, c0 + hh:c0 + hh + HEAD_DIM] = (a * lax.rsqrt(ms + RMS_EPS) * w).astype(o_ref.dtype)
        else:
            o_ref[:, c0:c0 + MIX_WIDTH] = acc.astype(o_ref.dtype)
    _norm_mod_rows(h_scr.at[i % 2], x_ref, nw_ref[...] * (1.0 + sc_ref[...]), sh_ref[...])


def _in_proj(x2, modt, modrow, nw, w_in, layer, qw, kw, tm):
    t, d = x2.shape
    n = w_in.shape[2]
    nt = t // tm
    row_fn = _row_fn(modrow, tm)
    cur = lambda i: jnp.minimum(i, nt - 1)
    mod_spec = lambda chunk: pl.BlockSpec((None, 1, d), lambda i: (row_fn(cur(i)) * N_MOD + chunk, 0, 0))
    return pl.pallas_call(
        _in_proj_kernel,
        grid=(nt + 1,),
        in_specs=[
            pl.BlockSpec((tm, d), lambda i: (cur(i), 0)),
            mod_spec(0),
            mod_spec(1),
            pl.BlockSpec((1, d), lambda i: (0, 0)),
            _wspec(layer, (d, n), lambda i: (0, 0), pipeline_mode=pl.Buffered(1)),
            pl.BlockSpec((1, HEAD_DIM), lambda i: (0, 0)),
            pl.BlockSpec((1, HEAD_DIM), lambda i: (0, 0)),
        ],
        out_specs=[pl.BlockSpec((tm, n), lambda i: (jnp.maximum(i - 1, 0), 0)),
                   pl.BlockSpec((tm, d), lambda i: (jnp.maximum(i - 1, 0), 0))],
        out_shape=[jax.ShapeDtypeStruct((t, n), BF16), jax.ShapeDtypeStruct((t, d), BF16)],
        scratch_shapes=[pltpu.VMEM((2, tm, d), BF16)],
        compiler_params=_params(1),
        name="in_proj",
    )(x2, modt, modt, nw, w_in, qw, kw)


def _na_bias_tables(rpb, rows):
    cols = np.arange(GRID_W)
    cs = np.clip(cols - NA_KW // 2, 0, GRID_W - NA_KW)
    valid_c = (cols[None, :] >= cs[:, None]) & (cols[None, :] < cs[:, None] + NA_KW)
    pad = GRID_W - NA_KW
    rp = jnp.pad(rpb.astype(F32), ((0, 0), (0, 0), (pad, pad)))
    col_t = jnp.stack([rp[:, :, GRID_W - 1 - qc:2 * GRID_W - 1 - qc] for qc in range(GRID_W)], axis=2)
    tables = []
    for r0, kr0 in ((0, 0), (2 * NA_QROWS, 2 * NA_QROWS - NA_KH // 2), (rows - NA_QROWS, rows - NA_KROWS)):
        r = r0 + np.arange(NA_QROWS)
        rs = np.clip(r - NA_KH // 2, 0, rows - NA_KH)
        key_row = kr0 + np.arange(NA_KROWS)
        valid_r = (key_row[None, :] >= rs[:, None]) & (key_row[None, :] < rs[:, None] + NA_KH)
        row_off = np.clip(key_row[None, :] - r[:, None] + (NA_KH - 1), 0, 2 * NA_KH - 2)
        tab = jnp.concatenate([jnp.concatenate([col_t[:, int(row_off[qr, kr])] for kr in range(NA_KROWS)], axis=2)
                               for qr in range(NA_QROWS)], axis=1)
        valid = (valid_r[:, None, :, None] & valid_c[None, :, None, :]).reshape(tab.shape[1:])
        tables.append(jnp.where(valid[None], tab, NEG_BIG))
    return jnp.stack(tables)


def _qk(q, k):
    return lax.dot_general(q, k, (((1,), (1,)), ((), ())), preferred_element_type=F32)


def _lane_tiles(a, width=128):
    return [a[:, k:k + width] for k in range(0, a.shape[1], width)]


def _na_kernel(q_ref, k0_ref, k1_ref, k2_ref, v0_ref, v1_ref, v2_ref, kc_ref, vc_ref, bias_ref, o_ref):
    blk = k0_ref.shape[0]
    values = (vc_ref, v0_ref, v1_ref, v2_ref)

    def scores_of(h):
        hs = slice(h * HEAD_DIM, (h + 1) * HEAD_DIM)
        q = q_ref[:, hs]
        s_loc = [_qk(q, kr[:, hs]) + bias_ref[h, :, m * blk:(m + 1) * blk]
                 for m, kr in enumerate((k0_ref, k1_ref, k2_ref))]
        return [_qk(q, kc_ref[:, hs])] + s_loc

    nxt = scores_of(0)
    for h in range(NA_HEADS):
        hs = slice(h * HEAD_DIM, (h + 1) * HEAD_DIM)
        scores = nxt
        if h + 1 < NA_HEADS:
            nxt = scores_of(h + 1)
        mx = jnp.max(functools.reduce(jnp.maximum, [t for s in scores for t in _lane_tiles(s)]),
                     axis=-1, keepdims=True)
        probs = [jnp.exp2(s - mx) for s in scores]
        den = jnp.sum(functools.reduce(jnp.add, [t for p in probs for t in _lane_tiles(p)]),
                      axis=-1, keepdims=True)
        o = None
        for p, vr in zip(probs, values):
            pv = jnp.dot(p.astype(BF16), vr[:, hs], preferred_element_type=F32)
            o = pv if o is None else o + pv
        o_ref[:, hs] = (o / den).astype(o_ref.dtype)


def _neighbourhood_attention(proj, proj_ctx, bias, batch, seq):
    blk = NA_QROWS * GRID_W
    nrb = seq // blk
    n_kblk = NA_KROWS // NA_QROWS
    ctx_len = proj_ctx.shape[0] // batch

    def kblock(rb):
        return jnp.clip(rb - 1, 0, nrb - n_kblk)

    def kv_spec(m, col):
        return pl.BlockSpec((blk, NA_WIDTH), lambda b, rb: (b * nrb + kblock(rb) + m, col))

    def variant(rb):
        return jnp.where(rb == 0, 0, jnp.where(rb == nrb - 1, 2, 1))

    in_specs = [pl.BlockSpec((blk, NA_WIDTH), lambda b, rb: (b * nrb + rb, 0))]
    in_specs += [kv_spec(m, 1) for m in range(n_kblk)] + [kv_spec(m, 2) for m in range(n_kblk)]
    in_specs += [pl.BlockSpec((ctx_len, NA_WIDTH), lambda b, rb: (b, 1)),
                 pl.BlockSpec((ctx_len, NA_WIDTH), lambda b, rb: (b, 2)),
                 pl.BlockSpec((None, NA_HEADS, blk, NA_KROWS * GRID_W), lambda b, rb: (variant(rb), 0, 0, 0))]
    return pl.pallas_call(
        _na_kernel,
        grid=(batch, nrb),
        in_specs=in_specs,
        out_specs=pl.BlockSpec((blk, NA_WIDTH), lambda b, rb: (b * nrb + rb, 0)),
        out_shape=jax.ShapeDtypeStruct((batch * seq, NA_WIDTH), BF16),
        compiler_params=_params(2),
        name="neighbourhood_attention",
    )(*([proj] * (1 + 2 * n_kblk)), proj_ctx, proj_ctx, bias)


def _ctx_attn_kernel(q_ref, k_ref, v_ref, o_ref):
    for h in range(NA_HEADS):
        hs = slice(h * HEAD_DIM, (h + 1) * HEAD_DIM)
        s = _qk(q_ref[:, hs], k_ref[:, hs])
        p = jnp.exp2(s - jnp.max(s, axis=-1, keepdims=True))
        den = jnp.sum(p, axis=-1, keepdims=True)
        o = jnp.dot(p.astype(BF16), v_ref[:, hs], preferred_element_type=F32)
        o_ref[:, hs] = (o / den).astype(o_ref.dtype)


def _context_attention(proj_ctx, batch):
    ctx_len = proj_ctx.shape[0] // batch
    spec = lambda col: pl.BlockSpec((ctx_len, NA_WIDTH), lambda b: (b, col))
    return pl.pallas_call(
        _ctx_attn_kernel,
        grid=(batch,),
        in_specs=[spec(0), spec(1), spec(2)],
        out_specs=spec(0),
        out_shape=jax.ShapeDtypeStruct((proj_ctx.shape[0], NA_WIDTH), BF16),
        compiler_params=_params(1),
        name="context_attention",
    )(proj_ctx, proj_ctx, proj_ctx)


def _dft_cos_sin(n_out, n_in, period):
    ij = (np.arange(n_out, dtype=np.int64)[:, None] * np.arange(n_in, dtype=np.int64)[None, :]) % period
    ang = 2.0 * np.pi * ij.astype(np.float64) / period
    return np.cos(ang), np.sin(ang)


def _ft_stage_a_kernel(u_ref, mc_ref, m_ref, o_ref, xr_s, xi_s, yr_s, yi_s):
    c = GROUP_DIM
    n_a, bb = u_ref.shape[0], u_ref.shape[1]
    z = jnp.dot(u_ref[...].reshape(n_a * bb, c), mc_ref[...], preferred_element_type=F32)
    xr_s[...] = z[:, :c]
    xi_s[...] = z[:, c:]
    for k in range(bb):
        rows = pl.ds(k, n_a, stride=bb)
        xr, xi = xr_s[rows, :], xi_s[rows, :]
        rhs = jnp.concatenate([jnp.concatenate([xr, xi], axis=1), jnp.concatenate([-xi, xr], axis=1)], axis=0)
        y = jnp.dot(m_ref[k], rhs.astype(BF16), preferred_element_type=F32)
        yr_s[rows, :] = y[:, :c]
        yi_s[rows, :] = y[:, c:]
    for a in range(n_a):
        o_ref[a] = jnp.concatenate([yr_s[a * bb:(a + 1) * bb, :], yi_s[a * bb:(a + 1) * bb, :]], axis=1)


def _ft_stage_b_kernel(x_ref, m_ref, o_ref, o_s):
    c = GROUP_DIM
    aa, n_b = x_ref.shape[0], x_ref.shape[1]
    m = m_ref[...]
    for k in range(aa):
        y = x_ref[k]
        rhs = jnp.concatenate([y[:, :c], y[:, c:]], axis=0).astype(BF16)
        o_s[pl.ds(k, n_b, stride=aa), :] = jnp.dot(m, rhs, preferred_element_type=F32)
    for b in range(n_b):
        o_ref[b] = o_s[b * aa:(b + 1) * aa, :]


def _fourier_mix(proj, batch, seq):
    t = batch * seq
    n_b = GROUP_DIM
    n_a = seq // n_b
    g, c = FT_GROUPS, GROUP_DIM
    sub = 16

    cc, sc = _dft_cos_sin(c, c, c)
    m_c = jnp.asarray(np.concatenate([cc, -sc], axis=1), BF16)
    ca, sa = _dft_cos_sin(n_a, seq, seq)
    ca, sa = ca.reshape(n_a, n_a, n_b), sa.reshape(n_a, n_a, n_b)
    m_a = jnp.asarray(np.concatenate([ca, -sa], axis=1).transpose(2, 0, 1), BF16)
    cb, sb = _dft_cos_sin(n_b, n_b, n_b)
    m_b = jnp.asarray(np.concatenate([cb, sb], axis=1) / math.sqrt(seq * c), BF16)

    ft_col = COL_FT * MIX_WIDTH // c
    y = pl.pallas_call(
        _ft_stage_a_kernel,
        grid=(n_b // sub, batch, g),
        in_specs=[pl.BlockSpec((None, n_a, sub, c), lambda j, b, k: (b, 0, j, ft_col + k)),
                  pl.BlockSpec((c, 2 * c), lambda j, b, k: (0, 0)),
                  pl.BlockSpec((sub, n_a, 2 * n_a), lambda j, b, k: (j, 0, 0))],
        out_specs=pl.BlockSpec((None, n_a, sub, 2 * c), lambda j, b, k: (b, 0, j, k)),
        out_shape=jax.ShapeDtypeStruct((batch, n_a, n_b, 2 * g * c), F32),
        scratch_shapes=[pltpu.VMEM((n_a * sub, c), F32)] * 4,
        compiler_params=_params(3),
        name="fourier_stage_a",
    )(proj.reshape(batch, n_a, n_b, proj.shape[1]), m_c, m_a)

    o = pl.pallas_call(
        _ft_stage_b_kernel,
        grid=(batch, g, n_a // sub),
        in_specs=[pl.BlockSpec((None, sub, n_b, 2 * c), lambda b, k, j: (b, j, 0, k)),
                  pl.BlockSpec((n_b, 2 * n_b), lambda b, k, j: (0, 0))],
        out_specs=pl.BlockSpec((None, n_b, sub, c), lambda b, k, j: (b, 0, j, k)),
        out_shape=jax.ShapeDtypeStruct((batch, n_b, n_a, g * c), F32),
        scratch_shapes=[pltpu.VMEM((n_b * sub, c), F32)],
        compiler_params=_params(3),
        name="fourier_stage_b",
    )(y, m_b)
    return o.reshape(t, g * c)


def _ctx_fourier_kernel(u_ref, mc_ref, ml_ref, o_ref):
    mc = mc_ref[...]
    ml = ml_ref[...]
    for g in range(FT_GROUPS):
        z = jnp.dot(u_ref[:, g * GROUP_DIM:(g + 1) * GROUP_DIM], mc, preferred_element_type=F32)
        stack = jnp.concatenate([z[:, :GROUP_DIM], z[:, GROUP_DIM:]], axis=0).astype(BF16)
        o_ref[:, g * GROUP_DIM:(g + 1) * GROUP_DIM] = jnp.dot(ml, stack, preferred_element_type=F32).astype(o_ref.dtype)


def _context_fourier(proj_ctx, batch):
    ctx_len = proj_ctx.shape[0] // batch
    c = GROUP_DIM
    cc, sc = _dft_cos_sin(c, c, c)
    m_c = jnp.asarray(np.concatenate([cc, -sc], axis=1), BF16)
    cl, sl = _dft_cos_sin(ctx_len, ctx_len, ctx_len)
    m_l = jnp.asarray(np.concatenate([cl, sl], axis=1) / math.sqrt(ctx_len * c), BF16)
    return pl.pallas_call(
        _ctx_fourier_kernel,
        grid=(batch,),
        in_specs=[pl.BlockSpec((ctx_len, MIX_WIDTH), lambda b: (b, COL_FT)),
                  pl.BlockSpec((c, 2 * c), lambda b: (0, 0)),
                  pl.BlockSpec((ctx_len, 2 * ctx_len), lambda b: (0, 0))],
        out_specs=pl.BlockSpec((ctx_len, MIX_WIDTH), lambda b: (b, 0)),
        out_shape=jax.ShapeDtypeStruct((proj_ctx.shape[0], MIX_WIDTH), BF16),
        compiler_params=_params(1),
        name="context_fourier",
    )(proj_ctx, m_c, m_l)


def _merge_kernel(h_ref, pool_ref, pool_p_ref, pool_n_ref, sv_ref, sv_p_ref, sv_n_ref, scv_ref, scv_p_ref, scv_n_ref,
                  sb_ref, attn_ref, four_ref, wg_ref, wpa_ref, wpb_ref, wpc_ref, wpd_ref,
                  band_ref, wpool_ref, pscale_ref, convw_ref, o_ref,
                  brc_scr, brd_scr, ext_scr, *, tiles_per_seq, seq_len):
    i = pl.program_id(0)
    j = pl.program_id(1)
    tm = h_ref.shape[0]

    @pl.when(j == 0)
    def _():
        first = (i % tiles_per_seq) == 0
        last = (i % tiles_per_seq) == tiles_per_seq - 1
        pos = (i % tiles_per_seq) * tm + lax.broadcasted_iota(jnp.int32, (tm, 1), 0)

        zero = jnp.zeros(pool_p_ref.shape, pool_p_ref.dtype)
        ext = jnp.concatenate([jnp.where(first, zero, pool_p_ref[...]), pool_ref[...],
                               jnp.where(last, zero, pool_n_ref[...])], axis=0)
        for g, w in enumerate(POOL_WINDOWS):
            cs = slice(g * GROUP_DIM, (g + 1) * GROUP_DIM)
            win = jnp.dot(band_ref[g], ext[:, cs], preferred_element_type=F32)
            cnt = (jnp.minimum(pos + w // 2, seq_len) - jnp.maximum(pos - w // 2, 0)).astype(F32)
            p = win * (1.0 / cnt) - pool_ref[:, cs].astype(F32)
            yg = jnp.dot(p.astype(BF16), wpool_ref[g], preferred_element_type=F32) * pscale_ref[:, cs]
            brc_scr[:, cs] = yg.astype(BF16)

        ext_scr[0:HALO, :] = jnp.where(first, 0.0, sv_p_ref[...].astype(F32) * scv_p_ref[...].astype(F32))
        ext_scr[HALO:HALO + tm, :] = sv_ref[...].astype(F32) * scv_ref[...].astype(F32)
        ext_scr[HALO + tm:HALO + tm + HALO, :] = jnp.where(last, 0.0,
                                                           sv_n_ref[...].astype(F32) * scv_n_ref[...].astype(F32))
        y = (ext_scr[HALO - 1:HALO - 1 + tm, :] * convw_ref[0:1, :]
             + ext_scr[HALO:HALO + tm, :] * convw_ref[1:2, :]
             + ext_scr[HALO + 1:HALO + 1 + tm, :] * convw_ref[2:3, :])
        brd_scr[...] = (sb_ref[...].astype(F32) * y).astype(BF16)

    h = h_ref[...]
    branches = ((attn_ref, wpa_ref), (four_ref, wpb_ref), (brc_scr, wpc_ref), (brd_scr, wpd_ref))
    y = None
    for k, (br_ref, wp_ref) in enumerate(branches):
        gate = jax.nn.sigmoid(jnp.dot(h, wg_ref[k], preferred_element_type=F32))
        term = gate * jnp.dot(br_ref[...].astype(BF16), wp_ref[...], preferred_element_type=F32)
        y = term if y is None else y + term
    o_ref[...] = y.astype(o_ref.dtype)


def _merge(h, proj, attn, four, wg, wpa, wpb, wpc, wpd, wpool, layer, pscale, convw, seq_len, tm, tn):
    t, d = h.shape
    tiles_per_seq = seq_len // tm
    n_halo = t // HALO

    rel = np.arange(tm + 2 * HALO)[None, :] - HALO - np.arange(tm)[:, None]
    band = jnp.asarray(np.stack([(rel >= -(w // 2)) & (rel < w // 2) for w in POOL_WINDOWS]), BF16)

    def halo_specs(col):
        return [pl.BlockSpec((tm, MIX_WIDTH), lambda i, j: (i, col)),
                pl.BlockSpec((HALO, MIX_WIDTH), lambda i, j: (jnp.maximum(i * (tm // HALO) - 1, 0), col)),
                pl.BlockSpec((HALO, MIX_WIDTH), lambda i, j: (jnp.minimum((i + 1) * (tm // HALO), n_halo - 1), col))]

    in_specs = [pl.BlockSpec((tm, d), lambda i, j: (i, 0))]
    in_specs += halo_specs(COL_POOL) + halo_specs(COL_SV) + halo_specs(COL_SC)
    in_specs += [pl.BlockSpec((tm, MIX_WIDTH), lambda i, j: (i, COL_SB)),
                 pl.BlockSpec((tm, NA_WIDTH), lambda i, j: (i, 0)),
                 pl.BlockSpec((tm, MIX_WIDTH), lambda i, j: (i, 0)),
                 _wspec(layer, (4, d, tn), lambda i, j: (0, 0, j)),
                 _wspec(layer, (NA_WIDTH, tn), lambda i, j: (0, j)),
                 _wspec(layer, (MIX_WIDTH, tn), lambda i, j: (0, j)),
                 _wspec(layer, (MIX_WIDTH, tn), lambda i, j: (0, j)),
                 _wspec(layer, (MIX_WIDTH, tn), lambda i, j: (0, j)),
                 pl.BlockSpec(band.shape, lambda i, j: (0, 0, 0)),
                 _wspec(layer, (len(POOL_WINDOWS), GROUP_DIM, GROUP_DIM), lambda i, j: (0, 0, 0)),
                 pl.BlockSpec((1, MIX_WIDTH), lambda i, j: (0, 0)),
                 pl.BlockSpec((3, MIX_WIDTH), lambda i, j: (0, 0))]
    return pl.pallas_call(
        functools.partial(_merge_kernel, tiles_per_seq=tiles_per_seq, seq_len=seq_len),
        grid=(t // tm, d // tn),
        in_specs=in_specs,
        out_specs=pl.BlockSpec((tm, tn), lambda i, j: (i, j)),
        out_shape=jax.ShapeDtypeStruct((t, d), BF16),
        scratch_shapes=[pltpu.VMEM((tm, MIX_WIDTH), BF16), pltpu.VMEM((tm, MIX_WIDTH), BF16),
                        pltpu.VMEM((tm + 2 * HALO, MIX_WIDTH), F32)],
        compiler_params=_params(2),
        name="gated_merge",
    )(h, *([proj] * 10), attn, four, wg, wpa, wpb, wpc, wpd, band, wpool, pscale, convw)


def _resid_proj_kernel(a_ref, w_ref, x_ref, g_ref, o_ref):
    o_ref[...] = x_ref[...] + g_ref[...] * jnp.dot(a_ref[...], w_ref[...], preferred_element_type=F32)


def _resid_proj(a, w, layer, x2, modt, chunk, modrow, tm, tn):
    t, k = a.shape
    row_fn = _row_fn(modrow, tm)
    d = w.shape[2]
    return pl.pallas_call(
        _resid_proj_kernel,
        grid=(d // tn, t // tm),
        in_specs=[pl.BlockSpec((tm, k), lambda j, i: (i, 0)),
                  _wspec(layer, (k, tn), lambda j, i: (0, j)),
                  pl.BlockSpec((tm, tn), lambda j, i: (i, j)),
                  pl.BlockSpec((None, 1, tn), lambda j, i: (row_fn(i) * N_MOD + chunk, 0, j))],
        out_specs=pl.BlockSpec((tm, tn), lambda j, i: (i, j)),
        out_shape=jax.ShapeDtypeStruct((t, d), F32),
        compiler_params=_params(2),
        name="residual_projection",
    )(a, w, x2, modt)


def _resid_norm_kernel(a_ref, w_ref, x_ref, g_ref, sh_ref, sc_ref, nw_ref, o_ref, ho_ref, xa_scr, xb_scr):
    i = pl.program_id(0)

    @pl.when(i == 0)
    def _():
        xb_scr[...] = jnp.zeros(xb_scr.shape, xb_scr.dtype)

    def step(cur_scr, prev_scr):
        _norm_mod_rows(ho_ref, prev_scr, nw_ref[...] * (1.0 + sc_ref[...]), sh_ref[...])
        x1 = x_ref[...] + g_ref[...] * jnp.dot(a_ref[...], w_ref[...], preferred_element_type=F32)
        o_ref[...] = x1
        cur_scr[...] = x1

    @pl.when(i % 2 == 0)
    def _():
        step(xa_scr, xb_scr)

    @pl.when(i % 2 == 1)
    def _():
        step(xb_scr, xa_scr)


def _resid_norm(a, w, layer, x2, modt, modrow, nw, tm):
    t, k = a.shape
    d = w.shape[2]
    nt = t // tm
    row_fn = _row_fn(modrow, tm)
    cur = lambda i: jnp.minimum(i, nt - 1)
    prev = lambda i: jnp.maximum(i - 1, 0)
    mod_spec = lambda chunk, tile: pl.BlockSpec((None, 1, d), lambda i: (row_fn(tile(i)) * N_MOD + chunk, 0, 0))
    return pl.pallas_call(
        _resid_norm_kernel,
        grid=(nt + 1,),
        in_specs=[pl.BlockSpec((tm, k), lambda i: (cur(i), 0)),
                  _wspec(layer, (k, d), lambda i: (0, 0), pipeline_mode=pl.Buffered(1)),
                  pl.BlockSpec((tm, d), lambda i: (cur(i), 0)),
                  mod_spec(2, cur), mod_spec(3, prev), mod_spec(4, prev),
                  pl.BlockSpec((1, d), lambda i: (0, 0))],
        out_specs=[pl.BlockSpec((tm, d), lambda i: (cur(i), 0)),
                   pl.BlockSpec((tm, d), lambda i: (prev(i), 0))],
        out_shape=[jax.ShapeDtypeStruct((t, d), F32), jax.ShapeDtypeStruct((t, d), BF16)],
        scratch_shapes=[pltpu.VMEM((tm, d), F32), pltpu.VMEM((tm, d), F32)],
        compiler_params=_params(1),
        name="residual_projection_norm",
    )(a, w, x2, modt, modt, modt, nw)


def _ffn_up_kernel(h_ref, wg_ref, wu_ref, o_ref):
    h = h_ref[...]
    gt = jnp.dot(h, wg_ref[...], preferred_element_type=F32)
    up = jnp.dot(h, wu_ref[...], preferred_element_type=F32)
    o_ref[...] = (gt * jax.nn.sigmoid(gt) * up).astype(o_ref.dtype)


def _ffn_up(h, w_g, w_u, layer, tm, tn):
    t, d = h.shape
    n = w_g.shape[2]
    return pl.pallas_call(
        _ffn_up_kernel,
        grid=(t // tm, n // tn),
        in_specs=[pl.BlockSpec((tm, d), lambda i, j: (i, 0)),
                  _wspec(layer, (d, tn), lambda i, j: (0, j)),
                  _wspec(layer, (d, tn), lambda i, j: (0, j))],
        out_specs=pl.BlockSpec((tm, tn), lambda i, j: (i, j)),
        out_shape=jax.ShapeDtypeStruct((t, n), BF16),
        compiler_params=_params(2),
        name="ffn_up",
    )(h, w_g, w_u)


def _stream_layer(x2, modt, modrow, seq_len, tm, lw, mixers):
    l = lw["layer"]
    proj, h = _in_proj(x2, modt, modrow, lw["norm1"], lw["w_in"], l, lw["qw"], lw["kw"], min(tm, 512))
    attn, four = mixers(proj)
    y = _merge(h, proj, attn, four, lw["w_gate"], lw["w_pa"], lw["w_pb"], lw["w_pc"],
               lw["w_pd"], lw["w_pool"], l, lw["pool_scale"], lw["conv_w"], seq_len, min(tm, 512), 512)
    x2, h2 = _resid_norm(y, lw["w_o"], l, x2, modt, modrow, lw["norm2"], min(tm, 512))
    a = _ffn_up(h2, lw["w_ffn_gate"], lw["w_ffn_up"], l, min(2 * tm, x2.shape[0]), 512)
    return _resid_proj(a, lw["w_ffn_down"], l, x2, modt, 5, modrow, min(tm, 512), min(x2.shape[1], 1024))


def kernel(x, c, ctx, c_ctx, w_mod, b_mod, norm1_w, w_in, q_norm_w, k_norm_w, rpb, w_pool, pool_scale, conv_w,
           w_gate, w_pa, w_pb, w_pc, w_pd, w_o, norm2_w, w_ffn_gate, w_ffn_up, w_ffn_down):
    batch, seq, d = x.shape
    ctx_len = ctx.shape[1]
    depth = w_mod.shape[0]
    rows = seq // GRID_W
    assert seq % (NA_QROWS * GRID_W) == 0 and rows >= NA_KROWS and batch + 1 <= MOD_ROWS

    cvec = jnp.zeros((MOD_ROWS, d), F32).at[:batch].set(c).at[batch].set(c_ctx)
    mod = _modulation(cvec, w_mod, b_mod)

    tm_lat = 1024
    lat_row = (0, seq)
    ctx_row = (batch, batch * ctx_len)

    x2 = x.reshape(batch * seq, d)
    xc = ctx.reshape(batch * ctx_len, d)
    scale = HEAD_DIM ** -0.5 * LOG2_E
    stacked = {name: w.astype(BF16) for name, w in (
        ("w_in", w_in), ("w_gate", w_gate), ("w_pa", w_pa), ("w_pb", w_pb), ("w_pc", w_pc), ("w_pd", w_pd),
        ("w_o", w_o), ("w_pool", w_pool), ("w_ffn_gate", w_ffn_gate), ("w_ffn_up", w_ffn_up),
        ("w_ffn_down", w_ffn_down))}
    for l in range(depth):
        lw = dict(stacked)
        lw.update({
            "layer": l,
            "norm1": norm1_w[l].reshape(1, d), "norm2": norm2_w[l].reshape(1, d),
            "qw": (q_norm_w[l] * scale).reshape(1, HEAD_DIM), "kw": k_norm_w[l].reshape(1, HEAD_DIM),
            "pool_scale": pool_scale[l].reshape(1, MIX_WIDTH), "conv_w": conv_w[l],
        })
        modt = mod[l].reshape(MOD_ROWS * N_MOD, 1, d)
        bias = _na_bias_tables(rpb[l] * LOG2_E, rows)

        if l == depth - 1:
            proj_c, _ = _in_proj(xc, modt, ctx_row, lw["norm1"], lw["w_in"], l, lw["qw"], lw["kw"], ctx_len)
        else:
            holder = {}

            def ctx_mixers(proj):
                holder["proj"] = proj
                return _context_attention(proj, batch), _context_fourier(proj, batch)

            xc = _stream_layer(xc, modt, ctx_row, ctx_len, ctx_len, lw, ctx_mixers)
            proj_c = holder["proj"]

        def lat_mixers(proj):
            return (_neighbourhood_attention(proj, proj_c, bias, batch, seq),
                    _fourier_mix(proj, batch, seq))

        x2 = _stream_layer(x2, modt, lat_row, seq, tm_lat, lw, lat_mixers)
    return x2.reshape(batch, seq, d)
```

```python
import functools
import math

import numpy as np
import jax
import jax.numpy as jnp
from jax import lax
from jax.experimental import pallas as pl
from jax.experimental.pallas import tpu as pltpu

F32 = jnp.float32
BF16 = jnp.bfloat16

GRID_W = 64
NA_HEADS = 8
HEAD_DIM = 128
NA_WIDTH = NA_HEADS * HEAD_DIM
NA_KH = 8
NA_KW = 16
NA_QROWS = 4
NA_KROWS = 12
GROUP_DIM = 128
FT_GROUPS = 4
POOL_WINDOWS = (2, 4, 8, 16)
MIX_WIDTH = 512
N_MOD = 6
RMS_EPS = 1e-6
BF16_ROWS = 16
HALO = 64
NORM_ROWS = 2 * BF16_ROWS
MOD_ROWS = 8
NEG_BIG = -1e30
LOG2_E = math.log2(math.e)

COL_FT, COL_POOL, COL_SV, COL_SB, COL_SC = 6, 7, 8, 9, 10

VMEM_LIMIT = 56 * 1024 * 1024


def _params(n_axes):
    return pltpu.CompilerParams(dimension_semantics=("arbitrary",) * n_axes,
                                vmem_limit_bytes=VMEM_LIMIT)


def _mod_kernel(c_ref, w_ref, b_ref, o_ref):
    cv = c_ref[...]
    s = cv * jax.nn.sigmoid(cv)
    o_ref[...] = jnp.dot(s.astype(BF16), w_ref[...].astype(BF16),
                         preferred_element_type=F32) + b_ref[...]


def _modulation(cvec, w_mod, b_mod):
    depth, d, n = w_mod.shape
    tn = 1024
    return pl.pallas_call(
        _mod_kernel,
        grid=(depth, n // tn),
        in_specs=[
            pl.BlockSpec((MOD_ROWS, d), lambda l, j: (0, 0)),
            pl.BlockSpec((None, d, tn), lambda l, j: (l, 0, j)),
            pl.BlockSpec((None, 1, tn), lambda l, j: (l, 0, j)),
        ],
        out_specs=pl.BlockSpec((None, MOD_ROWS, tn), lambda l, j: (l, 0, j)),
        out_shape=jax.ShapeDtypeStruct((depth, MOD_ROWS, n), F32),
        compiler_params=_params(2),
        name="modulation",
    )(cvec, w_mod, b_mod.reshape(depth, 1, n))


def _row_fn(modrow, tm):
    base, tokens_per_row = modrow
    return lambda i: base + (i * tm) // tokens_per_row


def _wspec(layer, block, index_map, **kwargs):
    return pl.BlockSpec((None,) + tuple(block), lambda *g: (layer,) + tuple(index_map(*g)), **kwargs)


def _norm_mod_rows(h_ref, x_ref, w, sh, chunk=NORM_ROWS):
    for r in range(0, x_ref.shape[0], chunk):
        xf = x_ref[r:r + chunk, :]
        ms = jnp.mean(xf * xf, axis=-1, keepdims=True)
        h_ref[r:r + chunk, :] = (xf * lax.rsqrt(ms + RMS_EPS) * w + sh).astype(BF16)


def _in_proj_kernel(x_ref, sh_ref, sc_ref, nw_ref, w_ref, qw_ref, kw_ref, o_ref, ho_ref, h_scr):
    i = pl.program_id(0)

    @pl.when(i == 0)
    def _():
        h_scr[1] = jnp.zeros(h_scr.shape[1:], h_scr.dtype)

    h = h_scr[(i + 1) % 2]
    ho_ref[...] = h
    for c0 in range(0, w_ref.shape[1], MIX_WIDTH):
        acc = jnp.dot(h, w_ref[:, c0:c0 + MIX_WIDTH], preferred_element_type=F32)
        if c0 < 2 * NA_WIDTH:
            w = qw_ref[...] if c0 < NA_WIDTH else kw_ref[...]
            for hh in range(0, MIX_WIDTH, HEAD_DIM):
                a = acc[:, hh:hh + HEAD_DIM]
                ms = jnp.mean(a * a, axis=-1, keepdims=True)
                o_ref[:, c0 + hh:c0 + hh + HEAD_DIM] = (a * lax.rsqrt(ms + RMS_EPS) * w).astype(o_ref.dtype)
        else:
            o_ref[:, c0:c0 + MIX_WIDTH] = acc.astype(o_ref.dtype)
    _norm_mod_rows(h_scr.at[i % 2], x_ref, nw_ref[...] * (1.0 + sc_ref[...]), sh_ref[...])


def _in_proj(x2, modt, modrow, nw, w_in, layer, qw, kw, tm):
    t, d = x2.shape
    n = w_in.shape[2]
    nt = t // tm
    row_fn = _row_fn(modrow, tm)
    cur = lambda i: jnp.minimum(i, nt - 1)
    mod_spec = lambda chunk: pl.BlockSpec((None, 1, d), lambda i: (row_fn(cur(i)) * N_MOD + chunk, 0, 0))
    return pl.pallas_call(
        _in_proj_kernel,
        grid=(nt + 1,),
        in_specs=[
            pl.BlockSpec((tm, d), lambda i: (cur(i), 0)),
            mod_spec(0),
            mod_spec(1),
            pl.BlockSpec((1, d), lambda i: (0, 0)),
            _wspec(layer, (d, n), lambda i: (0, 0), pipeline_mode=pl.Buffered(1)),
            pl.BlockSpec((1, HEAD_DIM), lambda i: (0, 0)),
            pl.BlockSpec((1, HEAD_DIM), lambda i: (0, 0)),
        ],
        out_specs=[pl.BlockSpec((tm, n), lambda i: (jnp.maximum(i - 1, 0), 0)),
                   pl.BlockSpec((tm, d), lambda i: (jnp.maximum(i - 1, 0), 0))],
        out_shape=[jax.ShapeDtypeStruct((t, n), BF16), jax.ShapeDtypeStruct((t, d), BF16)],
        scratch_shapes=[pltpu.VMEM((2, tm, d), BF16)],
        compiler_params=_params(1),
        name="in_proj",
    )(x2, modt, modt, nw, w_in, qw, kw)


def _na_bias_tables(rpb, rows):
    cols = np.arange(GRID_W)
    cs = np.clip(cols - NA_KW // 2, 0, GRID_W - NA_KW)
    valid_c = (cols[None, :] >= cs[:, None]) & (cols[None, :] < cs[:, None] + NA_KW)
    pad = GRID_W - NA_KW
    rp = jnp.pad(rpb.astype(F32), ((0, 0), (0, 0), (pad, pad)))
    col_t = jnp.stack([rp[:, :, GRID_W - 1 - qc:2 * GRID_W - 1 - qc] for qc in range(GRID_W)], axis=2)
    tables = []
    for r0, kr0 in ((0, 0), (2 * NA_QROWS, 2 * NA_QROWS - NA_KH // 2), (rows - NA_QROWS, rows - NA_KROWS)):
        r = r0 + np.arange(NA_QROWS)
        rs = np.clip(r - NA_KH // 2, 0, rows - NA_KH)
        key_row = kr0 + np.arange(NA_KROWS)
        valid_r = (key_row[None, :] >= rs[:, None]) & (key_row[None, :] < rs[:, None] + NA_KH)
        row_off = np.clip(key_row[None, :] - r[:, None] + (NA_KH - 1), 0, 2 * NA_KH - 2)
        tab = jnp.concatenate([jnp.concatenate([col_t[:, int(row_off[qr, kr])] for kr in range(NA_KROWS)], axis=2)
                               for qr in range(NA_QROWS)], axis=1)
        valid = (valid_r[:, None, :, None] & valid_c[None, :, None, :]).reshape(tab.shape[1:])
        tables.append(jnp.where(valid[None], tab, NEG_BIG))
    return jnp.stack(tables)


def _qk(q, k):
    return lax.dot_general(q, k, (((1,), (1,)), ((), ())), preferred_element_type=F32)


def _lane_tiles(a, width=128):
    return [a[:, k:k + width] for k in range(0, a.shape[1], width)]


def _na_kernel(q_ref, k0_ref, k1_ref, k2_ref, v0_ref, v1_ref, v2_ref, kc_ref, vc_ref, bias_ref, o_ref):
    blk = k0_ref.shape[0]
    values = (vc_ref, v0_ref, v1_ref, v2_ref)

    def scores_of(h):
        hs = slice(h * HEAD_DIM, (h + 1) * HEAD_DIM)
        q = q_ref[:, hs]
        s_loc = [_qk(q, kr[:, hs]) + bias_ref[h, :, m * blk:(m + 1) * blk]
                 for m, kr in enumerate((k0_ref, k1_ref, k2_ref))]
        return [_qk(q, kc_ref[:, hs])] + s_loc

    nxt = scores_of(0)
    for h in range(NA_HEADS):
        hs = slice(h * HEAD_DIM, (h + 1) * HEAD_DIM)
        scores = nxt
        if h + 1 < NA_HEADS:
            nxt = scores_of(h + 1)
        mx = jnp.max(functools.reduce(jnp.maximum, [t for s in scores for t in _lane_tiles(s)]),
                     axis=-1, keepdims=True)
        probs = [jnp.exp2(s - mx) for s in scores]
        den = jnp.sum(functools.reduce(jnp.add, [t for p in probs for t in _lane_tiles(p)]),
                      axis=-1, keepdims=True)
        o = None
        for p, vr in zip(probs, values):
            pv = jnp.dot(p.astype(BF16), vr[:, hs], preferred_element_type=F32)
            o = pv if o is None else o + pv
        o_ref[:, hs] = (o / den).astype(o_ref.dtype)


def _neighbourhood_attention(proj, proj_ctx, bias, batch, seq):
    blk = NA_QROWS * GRID_W
    nrb = seq // blk
    n_kblk = NA_KROWS // NA_QROWS
    ctx_len = proj_ctx.shape[0] // batch

    def kblock(rb):
        return jnp.clip(rb - 1, 0, nrb - n_kblk)

    def kv_spec(m, col):
        return pl.BlockSpec((blk, NA_WIDTH), lambda b, rb: (b * nrb + kblock(rb) + m, col))

    def variant(rb):
        return jnp.where(rb == 0, 0, jnp.where(rb == nrb - 1, 2, 1))

    in_specs = [pl.BlockSpec((blk, NA_WIDTH), lambda b, rb: (b * nrb + rb, 0))]
    in_specs += [kv_spec(m, 1) for m in range(n_kblk)] + [kv_spec(m, 2) for m in range(n_kblk)]
    in_specs += [pl.BlockSpec((ctx_len, NA_WIDTH), lambda b, rb: (b, 1)),
                 pl.BlockSpec((ctx_len, NA_WIDTH), lambda b, rb: (b, 2)),
                 pl.BlockSpec((None, NA_HEADS, blk, NA_KROWS * GRID_W), lambda b, rb: (variant(rb), 0, 0, 0))]
    return pl.pallas_call(
        _na_kernel,
        grid=(batch, nrb),
        in_specs=in_specs,
        out_specs=pl.BlockSpec((blk, NA_WIDTH), lambda b, rb: (b * nrb + rb, 0)),
        out_shape=jax.ShapeDtypeStruct((batch * seq, NA_WIDTH), BF16),
        compiler_params=_params(2),
        name="neighbourhood_attention",
    )(*([proj] * (1 + 2 * n_kblk)), proj_ctx, proj_ctx, bias)


def _ctx_attn_kernel(q_ref, k_ref, v_ref, o_ref):
    for h in range(NA_HEADS):
        hs = slice(h * HEAD_DIM, (h + 1) * HEAD_DIM)
        s = _qk(q_ref[:, hs], k_ref[:, hs])
        p = jnp.exp2(s - jnp.max(s, axis=-1, keepdims=True))
        den = jnp.sum(p, axis=-1, keepdims=True)
        o = jnp.dot(p.astype(BF16), v_ref[:, hs], preferred_element_type=F32)
        o_ref[:, hs] = (o / den).astype(o_ref.dtype)


def _context_attention(proj_ctx, batch):
    ctx_len = proj_ctx.shape[0] // batch
    spec = lambda col: pl.BlockSpec((ctx_len, NA_WIDTH), lambda b: (b, col))
    return pl.pallas_call(
        _ctx_attn_kernel,
        grid=(batch,),
        in_specs=[spec(0), spec(1), spec(2)],
        out_specs=spec(0),
        out_shape=jax.ShapeDtypeStruct((proj_ctx.shape[0], NA_WIDTH), BF16),
        compiler_params=_params(1),
        name="context_attention",
    )(proj_ctx, proj_ctx, proj_ctx)


def _dft_cos_sin(n_out, n_in, period):
    ij = (np.arange(n_out, dtype=np.int64)[:, None] * np.arange(n_in, dtype=np.int64)[None, :]) % period
    ang = 2.0 * np.pi * ij.astype(np.float64) / period
    return np.cos(ang), np.sin(ang)


def _ft_stage_a_kernel(u_ref, mc_ref, m_ref, o_ref, xr_s, xi_s, yr_s, yi_s):
    c = GROUP_DIM
    n_a, bb = u_ref.shape[0], u_ref.shape[1]
    z = jnp.dot(u_ref[...].reshape(n_a * bb, c), mc_ref[...], preferred_element_type=F32)
    xr_s[...] = z[:, :c]
    xi_s[...] = z[:, c:]
    for k in range(bb):
        rows = pl.ds(k, n_a, stride=bb)
        xr, xi = xr_s[rows, :], xi_s[rows, :]
        rhs = jnp.concatenate([jnp.concatenate([xr, xi], axis=1), jnp.concatenate([-xi, xr], axis=1)], axis=0)
        y = jnp.dot(m_ref[k], rhs.astype(BF16), preferred_element_type=F32)
        yr_s[rows, :] = y[:, :c]
        yi_s[rows, :] = y[:, c:]
    for a in range(n_a):
        o_ref[a] = jnp.concatenate([yr_s[a * bb:(a + 1) * bb, :], yi_s[a * bb:(a + 1) * bb, :]],
                                   axis=1).astype(o_ref.dtype)


def _ft_stage_b_kernel(x_ref, m_ref, o_ref, o_s):
    c = GROUP_DIM
    aa, n_b = x_ref.shape[0], x_ref.shape[1]
    m = m_ref[...]
    for k in range(aa):
        y = x_ref[k]
        rhs = jnp.concatenate([y[:, :c], y[:, c:]], axis=0)
        o_s[pl.ds(k, n_b, stride=aa), :] = jnp.dot(m, rhs, preferred_element_type=F32)
    for b in range(n_b):
        o_ref[b] = o_s[b * aa:(b + 1) * aa, :]


def _fourier_mix(proj, batch, seq):
    t = batch * seq
    n_b = GROUP_DIM
    n_a = seq // n_b
    g, c = FT_GROUPS, GROUP_DIM
    sub = 16

    cc, sc = _dft_cos_sin(c, c, c)
    m_c = jnp.asarray(np.concatenate([cc, -sc], axis=1), BF16)
    ca, sa = _dft_cos_sin(n_a, seq, seq)
    ca, sa = ca.reshape(n_a, n_a, n_b), sa.reshape(n_a, n_a, n_b)
    m_a = jnp.asarray(np.concatenate([ca, -sa], axis=1).transpose(2, 0, 1), BF16)
    cb, sb = _dft_cos_sin(n_b, n_b, n_b)
    m_b = jnp.asarray(np.concatenate([cb, sb], axis=1) / math.sqrt(seq * c), BF16)

    ft_col = COL_FT * MIX_WIDTH // c
    y = pl.pallas_call(
        _ft_stage_a_kernel,
        grid=(n_b // sub, batch, g),
        in_specs=[pl.BlockSpec((None, n_a, sub, c), lambda j, b, k: (b, 0, j, ft_col + k)),
                  pl.BlockSpec((c, 2 * c), lambda j, b, k: (0, 0)),
                  pl.BlockSpec((sub, n_a, 2 * n_a), lambda j, b, k: (j, 0, 0))],
        out_specs=pl.BlockSpec((None, n_a, sub, 2 * c), lambda j, b, k: (b, 0, j, k)),
        out_shape=jax.ShapeDtypeStruct((batch, n_a, n_b, 2 * g * c), BF16),
        scratch_shapes=[pltpu.VMEM((n_a * sub, c), F32)] * 4,
        compiler_params=_params(3),
        name="fourier_stage_a",
    )(proj.reshape(batch, n_a, n_b, proj.shape[1]), m_c, m_a)

    o = pl.pallas_call(
        _ft_stage_b_kernel,
        grid=(batch, g, n_a // sub),
        in_specs=[pl.BlockSpec((None, sub, n_b, 2 * c), lambda b, k, j: (b, j, 0, k)),
                  pl.BlockSpec((n_b, 2 * n_b), lambda b, k, j: (0, 0))],
        out_specs=pl.BlockSpec((None, n_b, sub, c), lambda b, k, j: (b, 0, j, k)),
        out_shape=jax.ShapeDtypeStruct((batch, n_b, n_a, g * c), F32),
        scratch_shapes=[pltpu.VMEM((n_b * sub, c), F32)],
        compiler_params=_params(3),
        name="fourier_stage_b",
    )(y, m_b)
    return o.reshape(t, g * c)


def _ctx_fourier_kernel(u_ref, mc_ref, ml_ref, o_ref):
    mc = mc_ref[...]
    ml = ml_ref[...]
    for g in range(FT_GROUPS):
        z = jnp.dot(u_ref[:, g * GROUP_DIM:(g + 1) * GROUP_DIM], mc, preferred_element_type=F32)
        stack = jnp.concatenate([z[:, :GROUP_DIM], z[:, GROUP_DIM:]], axis=0).astype(BF16)
        o_ref[:, g * GROUP_DIM:(g + 1) * GROUP_DIM] = jnp.dot(ml, stack, preferred_element_type=F32).astype(o_ref.dtype)


def _context_fourier(proj_ctx, batch):
    ctx_len = proj_ctx.shape[0] // batch
    c = GROUP_DIM
    cc, sc = _dft_cos_sin(c, c, c)
    m_c = jnp.asarray(np.concatenate([cc, -sc], axis=1), BF16)
    cl, sl = _dft_cos_sin(ctx_len, ctx_len, ctx_len)
    m_l = jnp.asarray(np.concatenate([cl, sl], axis=1) / math.sqrt(ctx_len * c), BF16)
    return pl.pallas_call(
        _ctx_fourier_kernel,
        grid=(batch,),
        in_specs=[pl.BlockSpec((ctx_len, MIX_WIDTH), lambda b: (b, COL_FT)),
                  pl.BlockSpec((c, 2 * c), lambda b: (0, 0)),
                  pl.BlockSpec((ctx_len, 2 * ctx_len), lambda b: (0, 0))],
        out_specs=pl.BlockSpec((ctx_len, MIX_WIDTH), lambda b: (b, 0)),
        out_shape=jax.ShapeDtypeStruct((proj_ctx.shape[0], MIX_WIDTH), BF16),
        compiler_params=_params(1),
        name="context_fourier",
    )(proj_ctx, m_c, m_l)


def _merge_kernel(h_ref, pool_ref, pool_p_ref, pool_n_ref, sv_ref, sv_p_ref, sv_n_ref, scv_ref, scv_p_ref, scv_n_ref,
                  sb_ref, attn_ref, four_ref, wg_ref, wpa_ref, wpb_ref, wpc_ref, wpd_ref,
                  band_ref, wpool_ref, pscale_ref, convw_ref, o_ref,
                  brc_scr, brd_scr, ext_scr, *, tiles_per_seq, seq_len):
    i = pl.program_id(0)
    j = pl.program_id(1)
    tm = h_ref.shape[0]

    @pl.when(j == 0)
    def _():
        first = (i % tiles_per_seq) == 0
        last = (i % tiles_per_seq) == tiles_per_seq - 1
        pos = (i % tiles_per_seq) * tm + lax.broadcasted_iota(jnp.int32, (tm, 1), 0)

        zero = jnp.zeros(pool_p_ref.shape, pool_p_ref.dtype)
        ext = jnp.concatenate([jnp.where(first, zero, pool_p_ref[...]), pool_ref[...],
                               jnp.where(last, zero, pool_n_ref[...])], axis=0)
        for g, w in enumerate(POOL_WINDOWS):
            cs = slice(g * GROUP_DIM, (g + 1) * GROUP_DIM)
            win = jnp.dot(band_ref[g], ext[:, cs], preferred_element_type=F32)
            cnt = (jnp.minimum(pos + w // 2, seq_len) - jnp.maximum(pos - w // 2, 0)).astype(F32)
            p = win * (1.0 / cnt) - pool_ref[:, cs].astype(F32)
            yg = jnp.dot(p.astype(BF16), wpool_ref[g], preferred_element_type=F32) * pscale_ref[:, cs]
            brc_scr[:, cs] = yg.astype(BF16)

        ext_scr[0:HALO, :] = jnp.where(first, 0.0, sv_p_ref[...].astype(F32) * scv_p_ref[...].astype(F32))
        ext_scr[HALO:HALO + tm, :] = sv_ref[...].astype(F32) * scv_ref[...].astype(F32)
        ext_scr[HALO + tm:HALO + tm + HALO, :] = jnp.where(last, 0.0,
                                                           sv_n_ref[...].astype(F32) * scv_n_ref[...].astype(F32))
        y = (ext_scr[HALO - 1:HALO - 1 + tm, :] * convw_ref[0:1, :]
             + ext_scr[HALO:HALO + tm, :] * convw_ref[1:2, :]
             + ext_scr[HALO + 1:HALO + 1 + tm, :] * convw_ref[2:3, :])
        brd_scr[...] = (sb_ref[...].astype(F32) * y).astype(BF16)

    h = h_ref[...]
    branches = ((attn_ref, wpa_ref), (four_ref, wpb_ref), (brc_scr, wpc_ref), (brd_scr, wpd_ref))
    y = None
    for k, (br_ref, wp_ref) in enumerate(branches):
        gate = jax.nn.sigmoid(jnp.dot(h, wg_ref[k], preferred_element_type=F32))
        term = gate * jnp.dot(br_ref[...].astype(BF16), wp_ref[...], preferred_element_type=F32)
        y = term if y is None else y + term
    o_ref[...] = y.astype(o_ref.dtype)


def _merge(h, proj, attn, four, wg, wpa, wpb, wpc, wpd, wpool, layer, pscale, convw, seq_len, tm, tn):
    t, d = h.shape
    tiles_per_seq = seq_len // tm
    n_halo = t // HALO

    rel = np.arange(tm + 2 * HALO)[None, :] - HALO - np.arange(tm)[:, None]
    band = jnp.asarray(np.stack([(rel >= -(w // 2)) & (rel < w // 2) for w in POOL_WINDOWS]), BF16)

    def halo_specs(col):
        return [pl.BlockSpec((tm, MIX_WIDTH), lambda i, j: (i, col)),
                pl.BlockSpec((HALO, MIX_WIDTH), lambda i, j: (jnp.maximum(i * (tm // HALO) - 1, 0), col)),
                pl.BlockSpec((HALO, MIX_WIDTH), lambda i, j: (jnp.minimum((i + 1) * (tm // HALO), n_halo - 1), col))]

    in_specs = [pl.BlockSpec((tm, d), lambda i, j: (i, 0))]
    in_specs += halo_specs(COL_POOL) + halo_specs(COL_SV) + halo_specs(COL_SC)
    in_specs += [pl.BlockSpec((tm, MIX_WIDTH), lambda i, j: (i, COL_SB)),
                 pl.BlockSpec((tm, NA_WIDTH), lambda i, j: (i, 0)),
                 pl.BlockSpec((tm, MIX_WIDTH), lambda i, j: (i, 0)),
                 _wspec(layer, (4, d, tn), lambda i, j: (0, 0, j)),
                 _wspec(layer, (NA_WIDTH, tn), lambda i, j: (0, j)),
                 _wspec(layer, (MIX_WIDTH, tn), lambda i, j: (0, j)),
                 _wspec(layer, (MIX_WIDTH, tn), lambda i, j: (0, j)),
                 _wspec(layer, (MIX_WIDTH, tn), lambda i, j: (0, j)),
                 pl.BlockSpec(band.shape, lambda i, j: (0, 0, 0)),
                 _wspec(layer, (len(POOL_WINDOWS), GROUP_DIM, GROUP_DIM), lambda i, j: (0, 0, 0)),
                 pl.BlockSpec((1, MIX_WIDTH), lambda i, j: (0, 0)),
                 pl.BlockSpec((3, MIX_WIDTH), lambda i, j: (0, 0))]
    return pl.pallas_call(
        functools.partial(_merge_kernel, tiles_per_seq=tiles_per_seq, seq_len=seq_len),
        grid=(t // tm, d // tn),
        in_specs=in_specs,
        out_specs=pl.BlockSpec((tm, tn), lambda i, j: (i, j)),
        out_shape=jax.ShapeDtypeStruct((t, d), BF16),
        scratch_shapes=[pltpu.VMEM((tm, MIX_WIDTH), BF16), pltpu.VMEM((tm, MIX_WIDTH), BF16),
                        pltpu.VMEM((tm + 2 * HALO, MIX_WIDTH), F32)],
        compiler_params=_params(2),
        name="gated_merge",
    )(h, *([proj] * 10), attn, four, wg, wpa, wpb, wpc, wpd, band, wpool, pscale, convw)


def _resid_proj_kernel(a_ref, w_ref, x_ref, g_ref, o_ref):
    o_ref[...] = x_ref[...] + g_ref[...] * jnp.dot(a_ref[...], w_ref[...], preferred_element_type=F32)


def _resid_proj(a, w, layer, x2, modt, chunk, modrow, tm, tn):
    t, k = a.shape
    row_fn = _row_fn(modrow, tm)
    d = w.shape[2]
    return pl.pallas_call(
        _resid_proj_kernel,
        grid=(d // tn, t // tm),
        in_specs=[pl.BlockSpec((tm, k), lambda j, i: (i, 0)),
                  _wspec(layer, (k, tn), lambda j, i: (0, j)),
                  pl.BlockSpec((tm, tn), lambda j, i: (i, j)),
                  pl.BlockSpec((None, 1, tn), lambda j, i: (row_fn(i) * N_MOD + chunk, 0, j))],
        out_specs=pl.BlockSpec((tm, tn), lambda j, i: (i, j)),
        out_shape=jax.ShapeDtypeStruct((t, d), F32),
        compiler_params=_params(2),
        name="residual_projection",
    )(a, w, x2, modt)


def _resid_norm_kernel(a_ref, w_ref, x_ref, g_ref, sh_ref, sc_ref, nw_ref, o_ref, ho_ref, xa_scr, xb_scr):
    i = pl.program_id(0)

    @pl.when(i == 0)
    def _():
        xb_scr[...] = jnp.zeros(xb_scr.shape, xb_scr.dtype)

    def step(cur_scr, prev_scr):
        _norm_mod_rows(ho_ref, prev_scr, nw_ref[...] * (1.0 + sc_ref[...]), sh_ref[...])
        x1 = x_ref[...] + g_ref[...] * jnp.dot(a_ref[...], w_ref[...], preferred_element_type=F32)
        o_ref[...] = x1
        cur_scr[...] = x1

    @pl.when(i % 2 == 0)
    def _():
        step(xa_scr, xb_scr)

    @pl.when(i % 2 == 1)
    def _():
        step(xb_scr, xa_scr)


def _resid_norm(a, w, layer, x2, modt, modrow, nw, tm):
    t, k = a.shape
    d = w.shape[2]
    nt = t // tm
    row_fn = _row_fn(modrow, tm)
    cur = lambda i: jnp.minimum(i, nt - 1)
    prev = lambda i: jnp.maximum(i - 1, 0)
    mod_spec = lambda chunk, tile: pl.BlockSpec((None, 1, d), lambda i: (row_fn(tile(i)) * N_MOD + chunk, 0, 0))
    return pl.pallas_call(
        _resid_norm_kernel,
        grid=(nt + 1,),
        in_specs=[pl.BlockSpec((tm, k), lambda i: (cur(i), 0)),
                  _wspec(layer, (k, d), lambda i: (0, 0), pipeline_mode=pl.Buffered(1)),
                  pl.BlockSpec((tm, d), lambda i: (cur(i), 0)),
                  mod_spec(2, cur), mod_spec(3, prev), mod_spec(4, prev),
                  pl.BlockSpec((1, d), lambda i: (0, 0))],
        out_specs=[pl.BlockSpec((tm, d), lambda i: (cur(i), 0)),
                   pl.BlockSpec((tm, d), lambda i: (prev(i), 0))],
        out_shape=[jax.ShapeDtypeStruct((t, d), F32), jax.ShapeDtypeStruct((t, d), BF16)],
        scratch_shapes=[pltpu.VMEM((tm, d), F32), pltpu.VMEM((tm, d), F32)],
        compiler_params=_params(1),
        name="residual_projection_norm",
    )(a, w, x2, modt, modt, modt, nw)


def _ffn_up_kernel(h_ref, wg_ref, wu_ref, o_ref):
    h = h_ref[...]
    gt = jnp.dot(h, wg_ref[...], preferred_element_type=F32)
    up = jnp.dot(h, wu_ref[...], preferred_element_type=F32)
    o_ref[...] = (gt * jax.nn.sigmoid(gt) * up).astype(o_ref.dtype)


def _ffn_up(h, w_g, w_u, layer, tm, tn):
    t, d = h.shape
    n = w_g.shape[2]
    return pl.pallas_call(
        _ffn_up_kernel,
        grid=(t // tm, n // tn),
        in_specs=[pl.BlockSpec((tm, d), lambda i, j: (i, 0)),
                  _wspec(layer, (d, tn), lambda i, j: (0, j)),
                  _wspec(layer, (d, tn), lambda i, j: (0, j))],
        out_specs=pl.BlockSpec((tm, tn), lambda i, j: (i, j)),
        out_shape=jax.ShapeDtypeStruct((t, n), BF16),
        compiler_params=_params(2),
        name="ffn_up",
    )(h, w_g, w_u)


def _stream_layer(x2, modt, modrow, seq_len, tm, lw, mixers):
    l = lw["layer"]
    proj, h = _in_proj(x2, modt, modrow, lw["norm1"], lw["w_in"], l, lw["qw"], lw["kw"], min(tm, 512))
    attn, four = mixers(proj)
    y = _merge(h, proj, attn, four, lw["w_gate"], lw["w_pa"], lw["w_pb"], lw["w_pc"],
               lw["w_pd"], lw["w_pool"], l, lw["pool_scale"], lw["conv_w"], seq_len, min(tm, 512), 512)
    x2, h2 = _resid_norm(y, lw["w_o"], l, x2, modt, modrow, lw["norm2"], min(tm, 512))
    a = _ffn_up(h2, lw["w_ffn_gate"], lw["w_ffn_up"], l, min(2 * tm, x2.shape[0]), 512)
    return _resid_proj(a, lw["w_ffn_down"], l, x2, modt, 5, modrow, min(tm, 512), min(x2.shape[1], 1024))


def kernel(x, c, ctx, c_ctx, w_mod, b_mod, norm1_w, w_in, q_norm_w, k_norm_w, rpb, w_pool, pool_scale, conv_w,
           w_gate, w_pa, w_pb, w_pc, w_pd, w_o, norm2_w, w_ffn_gate, w_ffn_up, w_ffn_down):
    batch, seq, d = x.shape
    ctx_len = ctx.shape[1]
    depth = w_mod.shape[0]
    rows = seq // GRID_W
    assert seq % (NA_QROWS * GRID_W) == 0 and rows >= NA_KROWS and batch + 1 <= MOD_ROWS

    cvec = jnp.zeros((MOD_ROWS, d), F32).at[:batch].set(c).at[batch].set(c_ctx)
    mod = _modulation(cvec, w_mod, b_mod)

    tm_lat = 1024
    lat_row = (0, seq)
    ctx_row = (batch, batch * ctx_len)

    x2 = x.reshape(batch * seq, d)
    xc = ctx.reshape(batch * ctx_len, d)
    scale = HEAD_DIM ** -0.5 * LOG2_E
    stacked = {name: w.astype(BF16) for name, w in (
        ("w_in", w_in), ("w_gate", w_gate), ("w_pa", w_pa), ("w_pb", w_pb), ("w_pc", w_pc), ("w_pd", w_pd),
        ("w_o", w_o), ("w_pool", w_pool), ("w_ffn_gate", w_ffn_gate), ("w_ffn_up", w_ffn_up),
        ("w_ffn_down", w_ffn_down))}
    for l in range(depth):
        lw = dict(stacked)
        lw.update({
            "layer": l,
            "norm1": norm1_w[l].reshape(1, d), "norm2": norm2_w[l].reshape(1, d),
            "qw": (q_norm_w[l] * scale).reshape(1, HEAD_DIM), "kw": k_norm_w[l].reshape(1, HEAD_DIM),
            "pool_scale": pool_scale[l].reshape(1, MIX_WIDTH), "conv_w": conv_w[l],
        })
        modt = mod[l].reshape(MOD_ROWS * N_MOD, 1, d)
        bias = _na_bias_tables(rpb[l] * LOG2_E, rows)

        if l == depth - 1:
            proj_c, _ = _in_proj(xc, modt, ctx_row, lw["norm1"], lw["w_in"], l, lw["qw"], lw["kw"], ctx_len)
        else:
            holder = {}

            def ctx_mixers(proj):
                holder["proj"] = proj
                return _context_attention(proj, batch), _context_fourier(proj, batch)

            xc = _stream_layer(xc, modt, ctx_row, ctx_len, ctx_len, lw, ctx_mixers)
            proj_c = holder["proj"]

        def lat_mixers(proj):
            return (_neighbourhood_attention(proj, proj_c, bias, batch, seq),
                    _fourier_mix(proj, batch, seq))

        x2 = _stream_layer(x2, modt, lat_row, seq, tm_lat, lw, lat_mixers)
    return x2.reshape(batch, seq, d)
```

```python
import functools
import math

import numpy as np
import jax
import jax.numpy as jnp
from jax import lax
from jax.experimental import pallas as pl
from jax.experimental.pallas import tpu as pltpu

F32 = jnp.float32
BF16 = jnp.bfloat16

GRID_W = 64
NA_HEADS = 8
HEAD_DIM = 128
NA_WIDTH = NA_HEADS * HEAD_DIM
NA_KH = 8
NA_KW = 16
NA_QROWS = 4
NA_KROWS = 12
GROUP_DIM = 128
FT_GROUPS = 4
POOL_WINDOWS = (2, 4, 8, 16)
MIX_WIDTH = 512
N_MOD = 6
RMS_EPS = 1e-6
F32_ROWS = 8
BF16_ROWS = 16
HALO = 64
NORM_ROWS = 2 * BF16_ROWS
MOD_ROWS = 8
NEG_BIG = -1e30
LOG2_E = math.log2(math.e)

COL_FT, COL_POOL, COL_SV, COL_SB, COL_SC = 6, 7, 8, 9, 10

VMEM_LIMIT = 56 * 1024 * 1024


def _params(n_axes):
    return pltpu.CompilerParams(dimension_semantics=("arbitrary",) * n_axes,
                                vmem_limit_bytes=VMEM_LIMIT)


def _mod_kernel(c_ref, w_ref, b_ref, o_ref):
    cv = c_ref[...]
    s = cv * jax.nn.sigmoid(cv)
    o_ref[...] = jnp.dot(s.astype(BF16), w_ref[...].astype(BF16),
                         preferred_element_type=F32) + b_ref[...]


def _modulation(cvec, w_mod, b_mod):
    depth, d, n = w_mod.shape
    tn = 1024
    return pl.pallas_call(
        _mod_kernel,
        grid=(depth, n // tn),
        in_specs=[
            pl.BlockSpec((MOD_ROWS, d), lambda l, j: (0, 0)),
            pl.BlockSpec((None, d, tn), lambda l, j: (l, 0, j)),
            pl.BlockSpec((None, 1, tn), lambda l, j: (l, 0, j)),
        ],
        out_specs=pl.BlockSpec((None, MOD_ROWS, tn), lambda l, j: (l, 0, j)),
        out_shape=jax.ShapeDtypeStruct((depth, MOD_ROWS, n), F32),
        compiler_params=_params(2),
        name="modulation",
    )(cvec, w_mod, b_mod.reshape(depth, 1, n))


def _row_fn(modrow, tm):
    base, tokens_per_row = modrow
    return lambda i: base + (i * tm) // tokens_per_row


def _wspec(layer, block, index_map, **kwargs):
    return pl.BlockSpec((None,) + tuple(block), lambda *g: (layer,) + tuple(index_map(*g)), **kwargs)


def _norm_mod_rows(h_ref, x_ref, w, sh, chunk=NORM_ROWS):
    for r in range(0, x_ref.shape[0], chunk):
        xf = x_ref[r:r + chunk, :]
        ms = jnp.mean(xf * xf, axis=-1, keepdims=True)
        h_ref[r:r + chunk, :] = (xf * lax.rsqrt(ms + RMS_EPS) * w + sh).astype(BF16)


def _in_proj_kernel(x_ref, sh_ref, sc_ref, nw_ref, w_ref, qw_ref, kw_ref, o_ref, ho_ref, h_scr):
    i = pl.program_id(0)

    @pl.when(i == 0)
    def _():
        h_scr[1] = jnp.zeros(h_scr.shape[1:], h_scr.dtype)

    h = h_scr[(i + 1) % 2]
    ho_ref[...] = h
    for c0 in range(0, w_ref.shape[1], MIX_WIDTH):
        acc = jnp.dot(h, w_ref[:, c0:c0 + MIX_WIDTH], preferred_element_type=F32)
        if c0 < 2 * NA_WIDTH:
            w = qw_ref[...] if c0 < NA_WIDTH else kw_ref[...]
            for hh in range(0, MIX_WIDTH, HEAD_DIM):
                a = acc[:, hh:hh + HEAD_DIM]
                ms = jnp.mean(a * a, axis=-1, keepdims=True)
                o_ref[:, c0 + hh:c0 + hh + HEAD_DIM] = (a * lax.rsqrt(ms + RMS_EPS) * w).astype(o_ref.dtype)
        else:
            o_ref[:, c0:c0 + MIX_WIDTH] = acc.astype(o_ref.dtype)
    _norm_mod_rows(h_scr.at[i % 2], x_ref, nw_ref[...] * (1.0 + sc_ref[...]), sh_ref[...])


def _in_proj(x2, modt, modrow, nw, w_in, layer, qw, kw, tm):
    t, d = x2.shape
    n = w_in.shape[2]
    nt = t // tm
    row_fn = _row_fn(modrow, tm)
    cur = lambda i: jnp.minimum(i, nt - 1)
    mod_spec = lambda chunk: pl.BlockSpec((None, 1, d), lambda i: (row_fn(cur(i)) * N_MOD + chunk, 0, 0))
    return pl.pallas_call(
        _in_proj_kernel,
        grid=(nt + 1,),
        in_specs=[
            pl.BlockSpec((tm, d), lambda i: (cur(i), 0)),
            mod_spec(0),
            mod_spec(1),
            pl.BlockSpec((1, d), lambda i: (0, 0)),
            _wspec(layer, (d, n), lambda i: (0, 0), pipeline_mode=pl.Buffered(1)),
            pl.BlockSpec((1, HEAD_DIM), lambda i: (0, 0)),
            pl.BlockSpec((1, HEAD_DIM), lambda i: (0, 0)),
        ],
        out_specs=[pl.BlockSpec((tm, n), lambda i: (jnp.maximum(i - 1, 0), 0)),
                   pl.BlockSpec((tm, d), lambda i: (jnp.maximum(i - 1, 0), 0))],
        out_shape=[jax.ShapeDtypeStruct((t, n), BF16), jax.ShapeDtypeStruct((t, d), BF16)],
        scratch_shapes=[pltpu.VMEM((2, tm, d), BF16)],
        compiler_params=_params(1),
        name="in_proj",
    )(x2, modt, modt, nw, w_in, qw, kw)


def _na_bias_tables(rpb, rows):
    cols = np.arange(GRID_W)
    cs = np.clip(cols - NA_KW // 2, 0, GRID_W - NA_KW)
    valid_c = (cols[None, :] >= cs[:, None]) & (cols[None, :] < cs[:, None] + NA_KW)
    pad = GRID_W - NA_KW
    rp = jnp.pad(rpb.astype(F32), ((0, 0), (0, 0), (pad, pad)))
    col_t = jnp.stack([rp[:, :, GRID_W - 1 - qc:2 * GRID_W - 1 - qc] for qc in range(GRID_W)], axis=2)
    tables = []
    for r0, kr0 in ((0, 0), (2 * NA_QROWS, 2 * NA_QROWS - NA_KH // 2), (rows - NA_QROWS, rows - NA_KROWS)):
        r = r0 + np.arange(NA_QROWS)
        rs = np.clip(r - NA_KH // 2, 0, rows - NA_KH)
        key_row = kr0 + np.arange(NA_KROWS)
        valid_r = (key_row[None, :] >= rs[:, None]) & (key_row[None, :] < rs[:, None] + NA_KH)
        row_off = np.clip(key_row[None, :] - r[:, None] + (NA_KH - 1), 0, 2 * NA_KH - 2)
        tab = jnp.concatenate([jnp.concatenate([col_t[:, int(row_off[qr, kr])] for kr in range(NA_KROWS)], axis=2)
                               for qr in range(NA_QROWS)], axis=1)
        valid = (valid_r[:, None, :, None] & valid_c[None, :, None, :]).reshape(tab.shape[1:])
        tables.append(jnp.where(valid[None], tab, NEG_BIG))
    return jnp.stack(tables)


def _qk(q, k):
    return lax.dot_general(q, k, (((1,), (1,)), ((), ())), preferred_element_type=F32)


def _lane_tiles(a, width=128):
    return [a[:, k:k + width] for k in range(0, a.shape[1], width)]


def _na_kernel(q_ref, k0_ref, k1_ref, k2_ref, v0_ref, v1_ref, v2_ref, kc_ref, vc_ref, bias_ref, o_ref):
    blk = k0_ref.shape[0]
    values = (vc_ref, v0_ref, v1_ref, v2_ref)

    def scores_of(h):
        hs = slice(h * HEAD_DIM, (h + 1) * HEAD_DIM)
        q = q_ref[:, hs]
        s_loc = [_qk(q, kr[:, hs]) + bias_ref[h, :, m * blk:(m + 1) * blk]
                 for m, kr in enumerate((k0_ref, k1_ref, k2_ref))]
        return [_qk(q, kc_ref[:, hs])] + s_loc

    nxt = scores_of(0)
    for h in range(NA_HEADS):
        hs = slice(h * HEAD_DIM, (h + 1) * HEAD_DIM)
        scores = nxt
        if h + 1 < NA_HEADS:
            nxt = scores_of(h + 1)
        mx = jnp.max(functools.reduce(jnp.maximum, [t for s in scores for t in _lane_tiles(s)]),
                     axis=-1, keepdims=True)
        probs = [jnp.exp2(s - mx) for s in scores]
        den = jnp.sum(functools.reduce(jnp.add, [t for p in probs for t in _lane_tiles(p)]),
                      axis=-1, keepdims=True)
        o = None
        for p, vr in zip(probs, values):
            pv = jnp.dot(p.astype(BF16), vr[:, hs], preferred_element_type=F32)
            o = pv if o is None else o + pv
        o_ref[:, hs] = (o / den).astype(o_ref.dtype)


def _neighbourhood_attention(proj, proj_ctx, bias, batch, seq):
    blk = NA_QROWS * GRID_W
    nrb = seq // blk
    n_kblk = NA_KROWS // NA_QROWS
    ctx_len = proj_ctx.shape[0] // batch

    def kblock(rb):
        return jnp.clip(rb - 1, 0, nrb - n_kblk)

    def kv_spec(m, col):
        return pl.BlockSpec((blk, NA_WIDTH), lambda b, rb: (b * nrb + kblock(rb) + m, col))

    def variant(rb):
        return jnp.where(rb == 0, 0, jnp.where(rb == nrb - 1, 2, 1))

    in_specs = [pl.BlockSpec((blk, NA_WIDTH), lambda b, rb: (b * nrb + rb, 0))]
    in_specs += [kv_spec(m, 1) for m in range(n_kblk)] + [kv_spec(m, 2) for m in range(n_kblk)]
    in_specs += [pl.BlockSpec((ctx_len, NA_WIDTH), lambda b, rb: (b, 1)),
                 pl.BlockSpec((ctx_len, NA_WIDTH), lambda b, rb: (b, 2)),
                 pl.BlockSpec((None, NA_HEADS, blk, NA_KROWS * GRID_W), lambda b, rb: (variant(rb), 0, 0, 0))]
    return pl.pallas_call(
        _na_kernel,
        grid=(batch, nrb),
        in_specs=in_specs,
        out_specs=pl.BlockSpec((blk, NA_WIDTH), lambda b, rb: (b * nrb + rb, 0)),
        out_shape=jax.ShapeDtypeStruct((batch * seq, NA_WIDTH), BF16),
        compiler_params=_params(2),
        name="neighbourhood_attention",
    )(*([proj] * (1 + 2 * n_kblk)), proj_ctx, proj_ctx, bias)


def _ctx_attn_kernel(q_ref, k_ref, v_ref, o_ref):
    for h in range(NA_HEADS):
        hs = slice(h * HEAD_DIM, (h + 1) * HEAD_DIM)
        s = _qk(q_ref[:, hs], k_ref[:, hs])
        p = jnp.exp2(s - jnp.max(s, axis=-1, keepdims=True))
        den = jnp.sum(p, axis=-1, keepdims=True)
        o = jnp.dot(p.astype(BF16), v_ref[:, hs], preferred_element_type=F32)
        o_ref[:, hs] = (o / den).astype(o_ref.dtype)


def _context_attention(proj_ctx, batch):
    ctx_len = proj_ctx.shape[0] // batch
    spec = lambda col: pl.BlockSpec((ctx_len, NA_WIDTH), lambda b: (b, col))
    return pl.pallas_call(
        _ctx_attn_kernel,
        grid=(batch,),
        in_specs=[spec(0), spec(1), spec(2)],
        out_specs=spec(0),
        out_shape=jax.ShapeDtypeStruct((proj_ctx.shape[0], NA_WIDTH), BF16),
        compiler_params=_params(1),
        name="context_attention",
    )(proj_ctx, proj_ctx, proj_ctx)


def _dft_cos_sin(n_out, n_in, period):
    ij = (np.arange(n_out, dtype=np.int64)[:, None] * np.arange(n_in, dtype=np.int64)[None, :]) % period
    ang = 2.0 * np.pi * ij.astype(np.float64) / period
    return np.cos(ang), np.sin(ang)


def _ft_stage_a_kernel(u_ref, mc_ref, m_ref, o_ref, xr0, xi0, xr1, xi1, yr0, yi0, yr1, yi1):
    c = GROUP_DIM
    n_a, bb = u_ref.shape[0], u_ref.shape[1]
    hb = bb // 2
    z = jnp.dot(u_ref[...].reshape(n_a * bb, c), mc_ref[...], preferred_element_type=F32)
    for a in range(n_a):
        lo, hi = z[a * bb:a * bb + hb], z[a * bb + hb:(a + 1) * bb]
        xr0[a * hb:(a + 1) * hb, :] = lo[:, :c]
        xi0[a * hb:(a + 1) * hb, :] = lo[:, c:]
        xr1[a * hb:(a + 1) * hb, :] = hi[:, :c]
        xi1[a * hb:(a + 1) * hb, :] = hi[:, c:]
    for k in range(bb):
        xr_s, xi_s, yr_s, yi_s = (xr0, xi0, yr0, yi0) if k < hb else (xr1, xi1, yr1, yi1)
        rows = pl.ds(k % hb, n_a, stride=hb)
        xr, xi = xr_s[rows, :], xi_s[rows, :]
        rhs = jnp.concatenate([jnp.concatenate([xr, xi], axis=1), jnp.concatenate([-xi, xr], axis=1)], axis=0)
        y = jnp.dot(m_ref[k], rhs.astype(BF16), preferred_element_type=F32)
        yr_s[rows, :] = y[:, :c]
        yi_s[rows, :] = y[:, c:]
    for a in range(n_a):
        sl = slice(a * hb, (a + 1) * hb)
        re = jnp.concatenate([yr0[sl, :], yr1[sl, :]], axis=0)
        im = jnp.concatenate([yi0[sl, :], yi1[sl, :]], axis=0)
        o_ref[a] = jnp.concatenate([re, im], axis=1).astype(o_ref.dtype)


def _ft_stage_b_kernel(x_ref, m_ref, o_ref, o_s):
    c = GROUP_DIM
    aa, n_b = x_ref.shape[0], x_ref.shape[1]
    m = m_ref[...]
    for k in range(aa):
        y = x_ref[k]
        rhs = jnp.concatenate([y[:, :c], y[:, c:]], axis=0)
        o_s[pl.ds(k, n_b, stride=aa), :] = jnp.dot(m, rhs, preferred_element_type=F32)
    for b in range(n_b):
        o_ref[b] = o_s[b * aa:(b + 1) * aa, :]


def _fourier_mix(proj, batch, seq):
    t = batch * seq
    n_b = GROUP_DIM
    n_a = seq // n_b
    g, c = FT_GROUPS, GROUP_DIM
    sub = 2 * F32_ROWS

    cc, sc = _dft_cos_sin(c, c, c)
    m_c = jnp.asarray(np.concatenate([cc, -sc], axis=1), BF16)
    ca, sa = _dft_cos_sin(n_a, seq, seq)
    ca, sa = ca.reshape(n_a, n_a, n_b), sa.reshape(n_a, n_a, n_b)
    m_a = jnp.asarray(np.concatenate([ca, -sa], axis=1).transpose(2, 0, 1), BF16)
    cb, sb = _dft_cos_sin(n_b, n_b, n_b)
    m_b = jnp.asarray(np.concatenate([cb, sb], axis=1) / math.sqrt(seq * c), BF16)

    ft_col = COL_FT * MIX_WIDTH // c
    y = pl.pallas_call(
        _ft_stage_a_kernel,
        grid=(n_b // sub, batch, g),
        in_specs=[pl.BlockSpec((None, n_a, sub, c), lambda j, b, k: (b, 0, j, ft_col + k)),
                  pl.BlockSpec((c, 2 * c), lambda j, b, k: (0, 0)),
                  pl.BlockSpec((sub, n_a, 2 * n_a), lambda j, b, k: (j, 0, 0))],
        out_specs=pl.BlockSpec((None, n_a, sub, 2 * c), lambda j, b, k: (b, 0, j, k)),
        out_shape=jax.ShapeDtypeStruct((batch, n_a, n_b, 2 * g * c), BF16),
        scratch_shapes=[pltpu.VMEM((n_a * sub // 2, c), F32)] * 8,
        compiler_params=_params(3),
        name="fourier_stage_a",
    )(proj.reshape(batch, n_a, n_b, proj.shape[1]), m_c, m_a)

    o = pl.pallas_call(
        _ft_stage_b_kernel,
        grid=(batch, g, n_a // sub),
        in_specs=[pl.BlockSpec((None, sub, n_b, 2 * c), lambda b, k, j: (b, j, 0, k)),
                  pl.BlockSpec((n_b, 2 * n_b), lambda b, k, j: (0, 0))],
        out_specs=pl.BlockSpec((None, n_b, sub, c), lambda b, k, j: (b, 0, j, k)),
        out_shape=jax.ShapeDtypeStruct((batch, n_b, n_a, g * c), F32),
        scratch_shapes=[pltpu.VMEM((n_b * sub, c), F32)],
        compiler_params=_params(3),
        name="fourier_stage_b",
    )(y, m_b)
    return o.reshape(t, g * c)


def _ctx_fourier_kernel(u_ref, mc_ref, ml_ref, o_ref):
    mc = mc_ref[...]
    ml = ml_ref[...]
    for g in range(FT_GROUPS):
        z = jnp.dot(u_ref[:, g * GROUP_DIM:(g + 1) * GROUP_DIM], mc, preferred_element_type=F32)
        stack = jnp.concatenate([z[:, :GROUP_DIM], z[:, GROUP_DIM:]], axis=0).astype(BF16)
        o_ref[:, g * GROUP_DIM:(g + 1) * GROUP_DIM] = jnp.dot(ml, stack, preferred_element_type=F32).astype(o_ref.dtype)


def _context_fourier(proj_ctx, batch):
    ctx_len = proj_ctx.shape[0] // batch
    c = GROUP_DIM
    cc, sc = _dft_cos_sin(c, c, c)
    m_c = jnp.asarray(np.concatenate([cc, -sc], axis=1), BF16)
    cl, sl = _dft_cos_sin(ctx_len, ctx_len, ctx_len)
    m_l = jnp.asarray(np.concatenate([cl, sl], axis=1) / math.sqrt(ctx_len * c), BF16)
    return pl.pallas_call(
        _ctx_fourier_kernel,
        grid=(batch,),
        in_specs=[pl.BlockSpec((ctx_len, MIX_WIDTH), lambda b: (b, COL_FT)),
                  pl.BlockSpec((c, 2 * c), lambda b: (0, 0)),
                  pl.BlockSpec((ctx_len, 2 * ctx_len), lambda b: (0, 0))],
        out_specs=pl.BlockSpec((ctx_len, MIX_WIDTH), lambda b: (b, 0)),
        out_shape=jax.ShapeDtypeStruct((proj_ctx.shape[0], MIX_WIDTH), BF16),
        compiler_params=_params(1),
        name="context_fourier",
    )(proj_ctx, m_c, m_l)


def _merge_kernel(h_ref, pool_ref, pool_p_ref, pool_n_ref, sv_ref, sv_p_ref, sv_n_ref, scv_ref, scv_p_ref, scv_n_ref,
                  sb_ref, attn_ref, four_ref, wg_ref, wpa_ref, wpb_ref, wpc_ref, wpd_ref,
                  band_ref, wpool_ref, pscale_ref, convw_ref, o_ref,
                  brc_scr, brd_scr, ext_scr, *, tiles_per_seq, seq_len):
    i = pl.program_id(0)
    j = pl.program_id(1)
    tm = h_ref.shape[0]

    @pl.when(j == 0)
    def _():
        first = (i % tiles_per_seq) == 0
        last = (i % tiles_per_seq) == tiles_per_seq - 1
        pos = (i % tiles_per_seq) * tm + lax.broadcasted_iota(jnp.int32, (tm, 1), 0)

        zero = jnp.zeros(pool_p_ref.shape, pool_p_ref.dtype)
        ext = jnp.concatenate([jnp.where(first, zero, pool_p_ref[...]), pool_ref[...],
                               jnp.where(last, zero, pool_n_ref[...])], axis=0)
        for g, w in enumerate(POOL_WINDOWS):
            cs = slice(g * GROUP_DIM, (g + 1) * GROUP_DIM)
            win = jnp.dot(band_ref[g], ext[:, cs], preferred_element_type=F32)
            cnt = (jnp.minimum(pos + w // 2, seq_len) - jnp.maximum(pos - w // 2, 0)).astype(F32)
            p = win * (1.0 / cnt) - pool_ref[:, cs].astype(F32)
            yg = jnp.dot(p.astype(BF16), wpool_ref[g], preferred_element_type=F32) * pscale_ref[:, cs]
            brc_scr[:, cs] = yg.astype(BF16)

        ext_scr[0:HALO, :] = jnp.where(first, 0.0, sv_p_ref[...].astype(F32) * scv_p_ref[...].astype(F32))
        ext_scr[HALO:HALO + tm, :] = sv_ref[...].astype(F32) * scv_ref[...].astype(F32)
        ext_scr[HALO + tm:HALO + tm + HALO, :] = jnp.where(last, 0.0,
                                                           sv_n_ref[...].astype(F32) * scv_n_ref[...].astype(F32))
        y = (ext_scr[HALO - 1:HALO - 1 + tm, :] * convw_ref[0:1, :]
             + ext_scr[HALO:HALO + tm, :] * convw_ref[1:2, :]
             + ext_scr[HALO + 1:HALO + 1 + tm, :] * convw_ref[2:3, :])
        brd_scr[...] = (sb_ref[...].astype(F32) * y).astype(BF16)

    h = h_ref[...]
    branches = ((attn_ref, wpa_ref), (four_ref, wpb_ref), (brc_scr, wpc_ref), (brd_scr, wpd_ref))
    y = None
    for k, (br_ref, wp_ref) in enumerate(branches):
        gate = jax.nn.sigmoid(jnp.dot(h, wg_ref[k], preferred_element_type=F32))
        term = gate * jnp.dot(br_ref[...].astype(BF16), wp_ref[...], preferred_element_type=F32)
        y = term if y is None else y + term
    o_ref[...] = y.astype(o_ref.dtype)


def _merge(h, proj, attn, four, wg, wpa, wpb, wpc, wpd, wpool, layer, pscale, convw, seq_len, tm, tn):
    t, d = h.shape
    tiles_per_seq = seq_len // tm
    n_halo = t // HALO

    rel = np.arange(tm + 2 * HALO)[None, :] - HALO - np.arange(tm)[:, None]
    band = jnp.asarray(np.stack([(rel >= -(w // 2)) & (rel < w // 2) for w in POOL_WINDOWS]), BF16)

    def halo_specs(col):
        return [pl.BlockSpec((tm, MIX_WIDTH), lambda i, j: (i, col)),
                pl.BlockSpec((HALO, MIX_WIDTH), lambda i, j: (jnp.maximum(i * (tm // HALO) - 1, 0), col)),
                pl.BlockSpec((HALO, MIX_WIDTH), lambda i, j: (jnp.minimum((i + 1) * (tm // HALO), n_halo - 1), col))]

    in_specs = [pl.BlockSpec((tm, d), lambda i, j: (i, 0))]
    in_specs += halo_specs(COL_POOL) + halo_specs(COL_SV) + halo_specs(COL_SC)
    in_specs += [pl.BlockSpec((tm, MIX_WIDTH), lambda i, j: (i, COL_SB)),
                 pl.BlockSpec((tm, NA_WIDTH), lambda i, j: (i, 0)),
                 pl.BlockSpec((tm, MIX_WIDTH), lambda i, j: (i, 0)),
                 _wspec(layer, (4, d, tn), lambda i, j: (0, 0, j)),
                 _wspec(layer, (NA_WIDTH, tn), lambda i, j: (0, j)),
                 _wspec(layer, (MIX_WIDTH, tn), lambda i, j: (0, j)),
                 _wspec(layer, (MIX_WIDTH, tn), lambda i, j: (0, j)),
                 _wspec(layer, (MIX_WIDTH, tn), lambda i, j: (0, j)),
                 pl.BlockSpec(band.shape, lambda i, j: (0, 0, 0)),
                 _wspec(layer, (len(POOL_WINDOWS), GROUP_DIM, GROUP_DIM), lambda i, j: (0, 0, 0)),
                 pl.BlockSpec((1, MIX_WIDTH), lambda i, j: (0, 0)),
                 pl.BlockSpec((3, MIX_WIDTH), lambda i, j: (0, 0))]
    return pl.pallas_call(
        functools.partial(_merge_kernel, tiles_per_seq=tiles_per_seq, seq_len=seq_len),
        grid=(t // tm, d // tn),
        in_specs=in_specs,
        out_specs=pl.BlockSpec((tm, tn), lambda i, j: (i, j)),
        out_shape=jax.ShapeDtypeStruct((t, d), BF16),
        scratch_shapes=[pltpu.VMEM((tm, MIX_WIDTH), BF16), pltpu.VMEM((tm, MIX_WIDTH), BF16),
                        pltpu.VMEM((tm + 2 * HALO, MIX_WIDTH), F32)],
        compiler_params=_params(2),
        name="gated_merge",
    )(h, *([proj] * 10), attn, four, wg, wpa, wpb, wpc, wpd, band, wpool, pscale, convw)


def _resid_proj_kernel(a_ref, w_ref, x_ref, g_ref, o_ref):
    o_ref[...] = x_ref[...] + g_ref[...] * jnp.dot(a_ref[...], w_ref[...], preferred_element_type=F32)


def _resid_proj(a, w, layer, x2, modt, chunk, modrow, tm, tn):
    t, k = a.shape
    row_fn = _row_fn(modrow, tm)
    d = w.shape[2]
    return pl.pallas_call(
        _resid_proj_kernel,
        grid=(d // tn, t // tm),
        in_specs=[pl.BlockSpec((tm, k), lambda j, i: (i, 0)),
                  _wspec(layer, (k, tn), lambda j, i: (0, j)),
                  pl.BlockSpec((tm, tn), lambda j, i: (i, j)),
                  pl.BlockSpec((None, 1, tn), lambda j, i: (row_fn(i) * N_MOD + chunk, 0, j))],
        out_specs=pl.BlockSpec((tm, tn), lambda j, i: (i, j)),
        out_shape=jax.ShapeDtypeStruct((t, d), F32),
        compiler_params=_params(2),
        name="residual_projection",
    )(a, w, x2, modt)


def _resid_norm_kernel(a_ref, w_ref, x_ref, g_ref, sh_ref, sc_ref, nw_ref, o_ref, ho_ref, xa_scr, xb_scr):
    i = pl.program_id(0)

    @pl.when(i == 0)
    def _():
        xb_scr[...] = jnp.zeros(xb_scr.shape, xb_scr.dtype)

    def step(cur_scr, prev_scr):
        _norm_mod_rows(ho_ref, prev_scr, nw_ref[...] * (1.0 + sc_ref[...]), sh_ref[...])
        x1 = x_ref[...] + g_ref[...] * jnp.dot(a_ref[...], w_ref[...], preferred_element_type=F32)
        o_ref[...] = x1
        cur_scr[...] = x1

    @pl.when(i % 2 == 0)
    def _():
        step(xa_scr, xb_scr)

    @pl.when(i % 2 == 1)
    def _():
        step(xb_scr, xa_scr)


def _resid_norm(a, w, layer, x2, modt, modrow, nw, tm):
    t, k = a.shape
    d = w.shape[2]
    nt = t // tm
    row_fn = _row_fn(modrow, tm)
    cur = lambda i: jnp.minimum(i, nt - 1)
    prev = lambda i: jnp.maximum(i - 1, 0)
    mod_spec = lambda chunk, tile: pl.BlockSpec((None, 1, d), lambda i: (row_fn(tile(i)) * N_MOD + chunk, 0, 0))
    return pl.pallas_call(
        _resid_norm_kernel,
        grid=(nt + 1,),
        in_specs=[pl.BlockSpec((tm, k), lambda i: (cur(i), 0)),
                  _wspec(layer, (k, d), lambda i: (0, 0), pipeline_mode=pl.Buffered(1)),
                  pl.BlockSpec((tm, d), lambda i: (cur(i), 0)),
                  mod_spec(2, cur), mod_spec(3, prev), mod_spec(4, prev),
                  pl.BlockSpec((1, d), lambda i: (0, 0))],
        out_specs=[pl.BlockSpec((tm, d), lambda i: (cur(i), 0)),
                   pl.BlockSpec((tm, d), lambda i: (prev(i), 0))],
        out_shape=[jax.ShapeDtypeStruct((t, d), F32), jax.ShapeDtypeStruct((t, d), BF16)],
        scratch_shapes=[pltpu.VMEM((tm, d), F32), pltpu.VMEM((tm, d), F32)],
        compiler_params=_params(1),
        name="residual_projection_norm",
    )(a, w, x2, modt, modt, modt, nw)


def _ffn_up_kernel(h_ref, wg_ref, wu_ref, o_ref):
    h = h_ref[...]
    gt = jnp.dot(h, wg_ref[...], preferred_element_type=F32)
    up = jnp.dot(h, wu_ref[...], preferred_element_type=F32)
    o_ref[...] = (gt * jax.nn.sigmoid(gt) * up).astype(o_ref.dtype)


def _ffn_up(h, w_g, w_u, layer, tm, tn):
    t, d = h.shape
    n = w_g.shape[2]
    return pl.pallas_call(
        _ffn_up_kernel,
        grid=(t // tm, n // tn),
        in_specs=[pl.BlockSpec((tm, d), lambda i, j: (i, 0)),
                  _wspec(layer, (d, tn), lambda i, j: (0, j)),
                  _wspec(layer, (d, tn), lambda i, j: (0, j))],
        out_specs=pl.BlockSpec((tm, tn), lambda i, j: (i, j)),
        out_shape=jax.ShapeDtypeStruct((t, n), BF16),
        compiler_params=_params(2),
        name="ffn_up",
    )(h, w_g, w_u)


def _stream_layer(x2, modt, modrow, seq_len, tm, lw, mixers):
    l = lw["layer"]
    proj, h = _in_proj(x2, modt, modrow, lw["norm1"], lw["w_in"], l, lw["qw"], lw["kw"], min(tm, 512))
    attn, four = mixers(proj)
    y = _merge(h, proj, attn, four, lw["w_gate"], lw["w_pa"], lw["w_pb"], lw["w_pc"],
               lw["w_pd"], lw["w_pool"], l, lw["pool_scale"], lw["conv_w"], seq_len, min(tm, 512), 512)
    x2, h2 = _resid_norm(y, lw["w_o"], l, x2, modt, modrow, lw["norm2"], min(tm, 512))
    a = _ffn_up(h2, lw["w_ffn_gate"], lw["w_ffn_up"], l, min(2 * tm, x2.shape[0]), 512)
    return _resid_proj(a, lw["w_ffn_down"], l, x2, modt, 5, modrow, min(tm, 512), min(x2.shape[1], 1024))


def kernel(x, c, ctx, c_ctx, w_mod, b_mod, norm1_w, w_in, q_norm_w, k_norm_w, rpb, w_pool, pool_scale, conv_w,
           w_gate, w_pa, w_pb, w_pc, w_pd, w_o, norm2_w, w_ffn_gate, w_ffn_up, w_ffn_down):
    batch, seq, d = x.shape
    ctx_len = ctx.shape[1]
    depth = w_mod.shape[0]
    rows = seq // GRID_W
    assert seq % (NA_QROWS * GRID_W) == 0 and rows >= NA_KROWS and batch + 1 <= MOD_ROWS

    cvec = jnp.zeros((MOD_ROWS, d), F32).at[:batch].set(c).at[batch].set(c_ctx)
    mod = _modulation(cvec, w_mod, b_mod)

    tm_lat = 1024
    lat_row = (0, seq)
    ctx_row = (batch, batch * ctx_len)

    x2 = x.reshape(batch * seq, d)
    xc = ctx.reshape(batch * ctx_len, d)
    scale = HEAD_DIM ** -0.5 * LOG2_E
    stacked = {name: w.astype(BF16) for name, w in (
        ("w_in", w_in), ("w_gate", w_gate), ("w_pa", w_pa), ("w_pb", w_pb), ("w_pc", w_pc), ("w_pd", w_pd),
        ("w_o", w_o), ("w_pool", w_pool), ("w_ffn_gate", w_ffn_gate), ("w_ffn_up", w_ffn_up),
        ("w_ffn_down", w_ffn_down))}
    for l in range(depth):
        lw = dict(stacked)
        lw.update({
            "layer": l,
            "norm1": norm1_w[l].reshape(1, d), "norm2": norm2_w[l].reshape(1, d),
            "qw": (q_norm_w[l] * scale).reshape(1, HEAD_DIM), "kw": k_norm_w[l].reshape(1, HEAD_DIM),
            "pool_scale": pool_scale[l].reshape(1, MIX_WIDTH), "conv_w": conv_w[l],
        })
        modt = mod[l].reshape(MOD_ROWS * N_MOD, 1, d)
        bias = _na_bias_tables(rpb[l] * LOG2_E, rows)

        if l == depth - 1:
            proj_c, _ = _in_proj(xc, modt, ctx_row, lw["norm1"], lw["w_in"], l, lw["qw"], lw["kw"], ctx_len)
        else:
            holder = {}

            def ctx_mixers(proj):
                holder["proj"] = proj
                return _context_attention(proj, batch), _context_fourier(proj, batch)

            xc = _stream_layer(xc, modt, ctx_row, ctx_len, ctx_len, lw, ctx_mixers)
            proj_c = holder["proj"]

        def lat_mixers(proj):
            return (_neighbourhood_attention(proj, proj_c, bias, batch, seq),
                    _fourier_mix(proj, batch, seq))

        x2 = _stream_layer(x2, modt, lat_row, seq, tm_lat, lw, lat_mixers)
    return x2.reshape(batch, seq, d)
```

```python
import functools
import math

import numpy as np
import jax
import jax.numpy as jnp
from jax import lax
from jax.experimental import pallas as pl
from jax.experimental.pallas import tpu as pltpu

F32 = jnp.float32
BF16 = jnp.bfloat16

GRID_W = 64
NA_HEADS = 8
HEAD_DIM = 128
NA_WIDTH = NA_HEADS * HEAD_DIM
NA_KH = 8
NA_KW = 16
NA_QROWS = 4
NA_KROWS = 12
GROUP_DIM = 128
FT_GROUPS = 4
POOL_WINDOWS = (2, 4, 8, 16)
MIX_WIDTH = 512
N_MOD = 6
RMS_EPS = 1e-6
F32_ROWS = 8
BF16_ROWS = 16
HALO = 64
NORM_ROWS = 2 * BF16_ROWS
MOD_ROWS = 8
NEG_BIG = -1e30
LOG2_E = math.log2(math.e)

COL_FT, COL_POOL, COL_SV, COL_SB, COL_SC = 6, 7, 8, 9, 10

VMEM_LIMIT = 56 * 1024 * 1024


def _params(n_axes):
    return pltpu.CompilerParams(dimension_semantics=("arbitrary",) * n_axes,
                                vmem_limit_bytes=VMEM_LIMIT)


def _mod_kernel(c_ref, w_ref, b_ref, o_ref):
    cv = c_ref[...]
    s = cv * jax.nn.sigmoid(cv)
    o_ref[...] = jnp.dot(s.astype(BF16), w_ref[...].astype(BF16),
                         preferred_element_type=F32) + b_ref[...]


def _modulation(cvec, w_mod, b_mod):
    depth, d, n = w_mod.shape
    tn = 1024
    return pl.pallas_call(
        _mod_kernel,
        grid=(depth, n // tn),
        in_specs=[
            pl.BlockSpec((MOD_ROWS, d), lambda l, j: (0, 0)),
            pl.BlockSpec((None, d, tn), lambda l, j: (l, 0, j)),
            pl.BlockSpec((None, 1, tn), lambda l, j: (l, 0, j)),
        ],
        out_specs=pl.BlockSpec((None, MOD_ROWS, tn), lambda l, j: (l, 0, j)),
        out_shape=jax.ShapeDtypeStruct((depth, MOD_ROWS, n), F32),
        compiler_params=_params(2),
        name="modulation",
    )(cvec, w_mod, b_mod.reshape(depth, 1, n))


def _row_fn(modrow, tm):
    base, tokens_per_row = modrow
    return lambda i: base + (i * tm) // tokens_per_row


def _wspec(layer, block, index_map, **kwargs):
    return pl.BlockSpec((None,) + tuple(block), lambda *g: (layer,) + tuple(index_map(*g)), **kwargs)


def _norm_mod_rows(h_ref, x_ref, w, sh, chunk=NORM_ROWS):
    for r in range(0, x_ref.shape[0], chunk):
        xf = x_ref[r:r + chunk, :]
        ms = jnp.mean(xf * xf, axis=-1, keepdims=True)
        h_ref[r:r + chunk, :] = (xf * lax.rsqrt(ms + RMS_EPS) * w + sh).astype(BF16)


def _in_proj_kernel(x_ref, sh_ref, sc_ref, nw_ref, w_ref, qw_ref, kw_ref, o_ref, ho_ref, h_scr):
    i = pl.program_id(0)

    @pl.when(i == 0)
    def _():
        h_scr[1] = jnp.zeros(h_scr.shape[1:], h_scr.dtype)

    h = h_scr[(i + 1) % 2]
    ho_ref[...] = h
    for c0 in range(0, w_ref.shape[1], MIX_WIDTH):
        acc = jnp.dot(h, w_ref[:, c0:c0 + MIX_WIDTH], preferred_element_type=F32)
        if c0 < 2 * NA_WIDTH:
            w = qw_ref[...] if c0 < NA_WIDTH else kw_ref[...]
            for hh in range(0, MIX_WIDTH, HEAD_DIM):
                a = acc[:, hh:hh + HEAD_DIM]
                ms = jnp.mean(a * a, axis=-1, keepdims=True)
                o_ref[:, c0 + hh:c0 + hh + HEAD_DIM] = (a * lax.rsqrt(ms + RMS_EPS) * w).astype(o_ref.dtype)
        else:
            o_ref[:, c0:c0 + MIX_WIDTH] = acc.astype(o_ref.dtype)
    _norm_mod_rows(h_scr.at[i % 2], x_ref, nw_ref[...] * (1.0 + sc_ref[...]), sh_ref[...])


def _in_proj(x2, modt, modrow, nw, w_in, layer, qw, kw, tm):
    t, d = x2.shape
    n = w_in.shape[2]
    nt = t // tm
    row_fn = _row_fn(modrow, tm)
    cur = lambda i: jnp.minimum(i, nt - 1)
    mod_spec = lambda chunk: pl.BlockSpec((None, 1, d), lambda i: (row_fn(cur(i)) * N_MOD + chunk, 0, 0))
    return pl.pallas_call(
        _in_proj_kernel,
        grid=(nt + 1,),
        in_specs=[
            pl.BlockSpec((tm, d), lambda i: (cur(i), 0)),
            mod_spec(0),
            mod_spec(1),
            pl.BlockSpec((1, d), lambda i: (0, 0)),
            _wspec(layer, (d, n), lambda i: (0, 0), pipeline_mode=pl.Buffered(1)),
            pl.BlockSpec((1, HEAD_DIM), lambda i: (0, 0)),
            pl.BlockSpec((1, HEAD_DIM), lambda i: (0, 0)),
        ],
        out_specs=[pl.BlockSpec((tm, n), lambda i: (jnp.maximum(i - 1, 0), 0)),
                   pl.BlockSpec((tm, d), lambda i: (jnp.maximum(i - 1, 0), 0))],
        out_shape=[jax.ShapeDtypeStruct((t, n), BF16), jax.ShapeDtypeStruct((t, d), BF16)],
        scratch_shapes=[pltpu.VMEM((2, tm, d), BF16)],
        compiler_params=_params(1),
        name="in_proj",
    )(x2, modt, modt, nw, w_in, qw, kw)


def _na_bias_tables(rpb, rows):
    cols = np.arange(GRID_W)
    cs = np.clip(cols - NA_KW // 2, 0, GRID_W - NA_KW)
    valid_c = (cols[None, :] >= cs[:, None]) & (cols[None, :] < cs[:, None] + NA_KW)
    pad = GRID_W - NA_KW
    rp = jnp.pad(rpb.astype(F32), ((0, 0), (0, 0), (pad, pad)))
    col_t = jnp.stack([rp[:, :, GRID_W - 1 - qc:2 * GRID_W - 1 - qc] for qc in range(GRID_W)], axis=2)
    tables = []
    for r0, kr0 in ((0, 0), (2 * NA_QROWS, 2 * NA_QROWS - NA_KH // 2), (rows - NA_QROWS, rows - NA_KROWS)):
        r = r0 + np.arange(NA_QROWS)
        rs = np.clip(r - NA_KH // 2, 0, rows - NA_KH)
        key_row = kr0 + np.arange(NA_KROWS)
        valid_r = (key_row[None, :] >= rs[:, None]) & (key_row[None, :] < rs[:, None] + NA_KH)
        row_off = np.clip(key_row[None, :] - r[:, None] + (NA_KH - 1), 0, 2 * NA_KH - 2)
        tab = jnp.concatenate([jnp.concatenate([col_t[:, int(row_off[qr, kr])] for kr in range(NA_KROWS)], axis=2)
                               for qr in range(NA_QROWS)], axis=1)
        valid = (valid_r[:, None, :, None] & valid_c[None, :, None, :]).reshape(tab.shape[1:])
        tables.append(jnp.where(valid[None], tab, NEG_BIG))
    return jnp.stack(tables)


def _qk(q, k):
    return lax.dot_general(q, k, (((1,), (1,)), ((), ())), preferred_element_type=F32)


def _lane_tiles(a, width=128):
    return [a[:, k:k + width] for k in range(0, a.shape[1], width)]


def _na_kernel(q_ref, k0_ref, k1_ref, k2_ref, v0_ref, v1_ref, v2_ref, kc_ref, vc_ref, bias_ref, o_ref):
    blk = k0_ref.shape[0]
    values = (vc_ref, v0_ref, v1_ref, v2_ref)

    def scores_of(h):
        hs = slice(h * HEAD_DIM, (h + 1) * HEAD_DIM)
        q = q_ref[:, hs]
        s_loc = [_qk(q, kr[:, hs]) + bias_ref[h, :, m * blk:(m + 1) * blk]
                 for m, kr in enumerate((k0_ref, k1_ref, k2_ref))]
        return [_qk(q, kc_ref[:, hs])] + s_loc

    nxt = scores_of(0)
    for h in range(NA_HEADS):
        hs = slice(h * HEAD_DIM, (h + 1) * HEAD_DIM)
        scores = nxt
        if h + 1 < NA_HEADS:
            nxt = scores_of(h + 1)
        mx = jnp.max(functools.reduce(jnp.maximum, [t for s in scores for t in _lane_tiles(s)]),
                     axis=-1, keepdims=True)
        probs = [jnp.exp2(s - mx) for s in scores]
        den = jnp.sum(functools.reduce(jnp.add, [t for p in probs for t in _lane_tiles(p)]),
                      axis=-1, keepdims=True)
        o = None
        for p, vr in zip(probs, values):
            pv = jnp.dot(p.astype(BF16), vr[:, hs], preferred_element_type=F32)
            o = pv if o is None else o + pv
        o_ref[:, hs] = (o / den).astype(o_ref.dtype)


def _neighbourhood_attention(proj, proj_ctx, bias, batch, seq):
    blk = NA_QROWS * GRID_W
    nrb = seq // blk
    n_kblk = NA_KROWS // NA_QROWS
    ctx_len = proj_ctx.shape[0] // batch

    def kblock(rb):
        return jnp.clip(rb - 1, 0, nrb - n_kblk)

    def kv_spec(m, col):
        return pl.BlockSpec((blk, NA_WIDTH), lambda b, rb: (b * nrb + kblock(rb) + m, col))

    def variant(rb):
        return jnp.where(rb == 0, 0, jnp.where(rb == nrb - 1, 2, 1))

    in_specs = [pl.BlockSpec((blk, NA_WIDTH), lambda b, rb: (b * nrb + rb, 0))]
    in_specs += [kv_spec(m, 1) for m in range(n_kblk)] + [kv_spec(m, 2) for m in range(n_kblk)]
    in_specs += [pl.BlockSpec((ctx_len, NA_WIDTH), lambda b, rb: (b, 1)),
                 pl.BlockSpec((ctx_len, NA_WIDTH), lambda b, rb: (b, 2)),
                 pl.BlockSpec((None, NA_HEADS, blk, NA_KROWS * GRID_W), lambda b, rb: (variant(rb), 0, 0, 0))]
    return pl.pallas_call(
        _na_kernel,
        grid=(batch, nrb),
        in_specs=in_specs,
        out_specs=pl.BlockSpec((blk, NA_WIDTH), lambda b, rb: (b * nrb + rb, 0)),
        out_shape=jax.ShapeDtypeStruct((batch * seq, NA_WIDTH), BF16),
        compiler_params=_params(2),
        name="neighbourhood_attention",
    )(*([proj] * (1 + 2 * n_kblk)), proj_ctx, proj_ctx, bias)


def _ctx_attn_kernel(q_ref, k_ref, v_ref, o_ref):
    for h in range(NA_HEADS):
        hs = slice(h * HEAD_DIM, (h + 1) * HEAD_DIM)
        s = _qk(q_ref[:, hs], k_ref[:, hs])
        p = jnp.exp2(s - jnp.max(s, axis=-1, keepdims=True))
        den = jnp.sum(p, axis=-1, keepdims=True)
        o = jnp.dot(p.astype(BF16), v_ref[:, hs], preferred_element_type=F32)
        o_ref[:, hs] = (o / den).astype(o_ref.dtype)


def _context_attention(proj_ctx, batch):
    ctx_len = proj_ctx.shape[0] // batch
    spec = lambda col: pl.BlockSpec((ctx_len, NA_WIDTH), lambda b: (b, col))
    return pl.pallas_call(
        _ctx_attn_kernel,
        grid=(batch,),
        in_specs=[spec(0), spec(1), spec(2)],
        out_specs=spec(0),
        out_shape=jax.ShapeDtypeStruct((proj_ctx.shape[0], NA_WIDTH), BF16),
        compiler_params=_params(1),
        name="context_attention",
    )(proj_ctx, proj_ctx, proj_ctx)


def _dft_cos_sin(n_out, n_in, period):
    ij = (np.arange(n_out, dtype=np.int64)[:, None] * np.arange(n_in, dtype=np.int64)[None, :]) % period
    ang = 2.0 * np.pi * ij.astype(np.float64) / period
    return np.cos(ang), np.sin(ang)


def _ft_stage_a_kernel(u_ref, mc_ref, m_ref, o_ref, xr0, xi0, xr1, xi1, yr0, yi0, yr1, yi1):
    c = GROUP_DIM
    n_a, bb = u_ref.shape[0], u_ref.shape[1]
    hb = bb // 2
    z = jnp.dot(u_ref[...].reshape(n_a * bb, c), mc_ref[...], preferred_element_type=F32)
    for a in range(n_a):
        lo, hi = z[a * bb:a * bb + hb], z[a * bb + hb:(a + 1) * bb]
        xr0[a * hb:(a + 1) * hb, :] = lo[:, :c]
        xi0[a * hb:(a + 1) * hb, :] = lo[:, c:]
        xr1[a * hb:(a + 1) * hb, :] = hi[:, :c]
        xi1[a * hb:(a + 1) * hb, :] = hi[:, c:]
    for k in range(bb):
        xr_s, xi_s, yr_s, yi_s = (xr0, xi0, yr0, yi0) if k < hb else (xr1, xi1, yr1, yi1)
        rows = pl.ds(k % hb, n_a, stride=hb)
        xr, xi = xr_s[rows, :], xi_s[rows, :]
        rhs = jnp.concatenate([jnp.concatenate([xr, xi], axis=1), jnp.concatenate([-xi, xr], axis=1)], axis=0)
        y = jnp.dot(m_ref[k], rhs.astype(BF16), preferred_element_type=F32)
        yr_s[rows, :] = y[:, :c]
        yi_s[rows, :] = y[:, c:]
    for a in range(n_a):
        sl = slice(a * hb, (a + 1) * hb)
        re = jnp.concatenate([yr0[sl, :], yr1[sl, :]], axis=0)
        im = jnp.concatenate([yi0[sl, :], yi1[sl, :]], axis=0)
        o_ref[a] = jnp.concatenate([re, im], axis=1).astype(o_ref.dtype)


def _ft_stage_b_kernel(x_ref, m_ref, o_ref, o_s):
    c = GROUP_DIM
    aa, n_b = x_ref.shape[0], x_ref.shape[1]
    m = m_ref[...]
    for k in range(aa):
        y = x_ref[k]
        rhs = jnp.concatenate([y[:, :c], y[:, c:]], axis=0)
        o_s[pl.ds(k, n_b, stride=aa), :] = jnp.dot(m, rhs, preferred_element_type=F32)
    for b in range(n_b):
        o_ref[b] = o_s[b * aa:(b + 1) * aa, :].astype(o_ref.dtype)


def _fourier_mix(proj, batch, seq):
    t = batch * seq
    n_b = GROUP_DIM
    n_a = seq // n_b
    g, c = FT_GROUPS, GROUP_DIM
    sub = 2 * F32_ROWS

    cc, sc = _dft_cos_sin(c, c, c)
    m_c = jnp.asarray(np.concatenate([cc, -sc], axis=1), BF16)
    ca, sa = _dft_cos_sin(n_a, seq, seq)
    ca, sa = ca.reshape(n_a, n_a, n_b), sa.reshape(n_a, n_a, n_b)
    m_a = jnp.asarray(np.concatenate([ca, -sa], axis=1).transpose(2, 0, 1), BF16)
    cb, sb = _dft_cos_sin(n_b, n_b, n_b)
    m_b = jnp.asarray(np.concatenate([cb, sb], axis=1) / math.sqrt(seq * c), BF16)

    ft_col = COL_FT * MIX_WIDTH // c
    y = pl.pallas_call(
        _ft_stage_a_kernel,
        grid=(n_b // sub, batch, g),
        in_specs=[pl.BlockSpec((None, n_a, sub, c), lambda j, b, k: (b, 0, j, ft_col + k)),
                  pl.BlockSpec((c, 2 * c), lambda j, b, k: (0, 0)),
                  pl.BlockSpec((sub, n_a, 2 * n_a), lambda j, b, k: (j, 0, 0))],
        out_specs=pl.BlockSpec((None, n_a, sub, 2 * c), lambda j, b, k: (b, 0, j, k)),
        out_shape=jax.ShapeDtypeStruct((batch, n_a, n_b, 2 * g * c), BF16),
        scratch_shapes=[pltpu.VMEM((n_a * sub // 2, c), F32)] * 8,
        compiler_params=_params(3),
        name="fourier_stage_a",
    )(proj.reshape(batch, n_a, n_b, proj.shape[1]), m_c, m_a)

    o = pl.pallas_call(
        _ft_stage_b_kernel,
        grid=(batch, g, n_a // sub),
        in_specs=[pl.BlockSpec((None, sub, n_b, 2 * c), lambda b, k, j: (b, j, 0, k)),
                  pl.BlockSpec((n_b, 2 * n_b), lambda b, k, j: (0, 0))],
        out_specs=pl.BlockSpec((None, n_b, sub, c), lambda b, k, j: (b, 0, j, k)),
        out_shape=jax.ShapeDtypeStruct((batch, n_b, n_a, g * c), BF16),
        scratch_shapes=[pltpu.VMEM((n_b * sub, c), F32)],
        compiler_params=_params(3),
        name="fourier_stage_b",
    )(y, m_b)
    return o.reshape(t, g * c)


def _ctx_fourier_kernel(u_ref, mc_ref, ml_ref, o_ref):
    mc = mc_ref[...]
    ml = ml_ref[...]
    for g in range(FT_GROUPS):
        z = jnp.dot(u_ref[:, g * GROUP_DIM:(g + 1) * GROUP_DIM], mc, preferred_element_type=F32)
        stack = jnp.concatenate([z[:, :GROUP_DIM], z[:, GROUP_DIM:]], axis=0).astype(BF16)
        o_ref[:, g * GROUP_DIM:(g + 1) * GROUP_DIM] = jnp.dot(ml, stack, preferred_element_type=F32).astype(o_ref.dtype)


def _context_fourier(proj_ctx, batch):
    ctx_len = proj_ctx.shape[0] // batch
    c = GROUP_DIM
    cc, sc = _dft_cos_sin(c, c, c)
    m_c = jnp.asarray(np.concatenate([cc, -sc], axis=1), BF16)
    cl, sl = _dft_cos_sin(ctx_len, ctx_len, ctx_len)
    m_l = jnp.asarray(np.concatenate([cl, sl], axis=1) / math.sqrt(ctx_len * c), BF16)
    return pl.pallas_call(
        _ctx_fourier_kernel,
        grid=(batch,),
        in_specs=[pl.BlockSpec((ctx_len, MIX_WIDTH), lambda b: (b, COL_FT)),
                  pl.BlockSpec((c, 2 * c), lambda b: (0, 0)),
                  pl.BlockSpec((ctx_len, 2 * ctx_len), lambda b: (0, 0))],
        out_specs=pl.BlockSpec((ctx_len, MIX_WIDTH), lambda b: (b, 0)),
        out_shape=jax.ShapeDtypeStruct((proj_ctx.shape[0], MIX_WIDTH), BF16),
        compiler_params=_params(1),
        name="context_fourier",
    )(proj_ctx, m_c, m_l)


def _merge_kernel(h_ref, pool_ref, pool_p_ref, pool_n_ref, sv_ref, sv_p_ref, sv_n_ref, scv_ref, scv_p_ref, scv_n_ref,
                  sb_ref, attn_ref, four_ref, wg_ref, wpa_ref, wpb_ref, wpc_ref, wpd_ref,
                  band_ref, wpool_ref, pscale_ref, convw_ref, o_ref,
                  brc_scr, brd_scr, ext_scr, *, tiles_per_seq, seq_len):
    i = pl.program_id(0)
    j = pl.program_id(1)
    tm = h_ref.shape[0]

    @pl.when(j == 0)
    def _():
        first = (i % tiles_per_seq) == 0
        last = (i % tiles_per_seq) == tiles_per_seq - 1
        pos = (i % tiles_per_seq) * tm + lax.broadcasted_iota(jnp.int32, (tm, 1), 0)

        zero = jnp.zeros(pool_p_ref.shape, pool_p_ref.dtype)
        ext = jnp.concatenate([jnp.where(first, zero, pool_p_ref[...]), pool_ref[...],
                               jnp.where(last, zero, pool_n_ref[...])], axis=0)
        for g, w in enumerate(POOL_WINDOWS):
            cs = slice(g * GROUP_DIM, (g + 1) * GROUP_DIM)
            win = jnp.dot(band_ref[g], ext[:, cs], preferred_element_type=F32)
            cnt = (jnp.minimum(pos + w // 2, seq_len) - jnp.maximum(pos - w // 2, 0)).astype(F32)
            p = win * (1.0 / cnt) - pool_ref[:, cs].astype(F32)
            yg = jnp.dot(p.astype(BF16), wpool_ref[g], preferred_element_type=F32) * pscale_ref[:, cs]
            brc_scr[:, cs] = yg.astype(BF16)

        ext_scr[0:HALO, :] = jnp.where(first, 0.0, sv_p_ref[...].astype(F32) * scv_p_ref[...].astype(F32))
        ext_scr[HALO:HALO + tm, :] = sv_ref[...].astype(F32) * scv_ref[...].astype(F32)
        ext_scr[HALO + tm:HALO + tm + HALO, :] = jnp.where(last, 0.0,
                                                           sv_n_ref[...].astype(F32) * scv_n_ref[...].astype(F32))
        y = (ext_scr[HALO - 1:HALO - 1 + tm, :] * convw_ref[0:1, :]
             + ext_scr[HALO:HALO + tm, :] * convw_ref[1:2, :]
             + ext_scr[HALO + 1:HALO + 1 + tm, :] * convw_ref[2:3, :])
        brd_scr[...] = (sb_ref[...].astype(F32) * y).astype(BF16)

    h = h_ref[...]
    branches = ((attn_ref, wpa_ref), (four_ref, wpb_ref), (brc_scr, wpc_ref), (brd_scr, wpd_ref))
    y = None
    for k, (br_ref, wp_ref) in enumerate(branches):
        gate = jax.nn.sigmoid(jnp.dot(h, wg_ref[k], preferred_element_type=F32))
        term = gate * jnp.dot(br_ref[...], wp_ref[...], preferred_element_type=F32)
        y = term if y is None else y + term
    o_ref[...] = y.astype(o_ref.dtype)


def _merge(h, proj, attn, four, wg, wpa, wpb, wpc, wpd, wpool, layer, pscale, convw, seq_len, tm, tn):
    t, d = h.shape
    tiles_per_seq = seq_len // tm
    n_halo = t // HALO

    rel = np.arange(tm + 2 * HALO)[None, :] - HALO - np.arange(tm)[:, None]
    band = jnp.asarray(np.stack([(rel >= -(w // 2)) & (rel < w // 2) for w in POOL_WINDOWS]), BF16)

    def halo_specs(col):
        return [pl.BlockSpec((tm, MIX_WIDTH), lambda i, j: (i, col)),
                pl.BlockSpec((HALO, MIX_WIDTH), lambda i, j: (jnp.maximum(i * (tm // HALO) - 1, 0), col)),
                pl.BlockSpec((HALO, MIX_WIDTH), lambda i, j: (jnp.minimum((i + 1) * (tm // HALO), n_halo - 1), col))]

    in_specs = [pl.BlockSpec((tm, d), lambda i, j: (i, 0))]
    in_specs += halo_specs(COL_POOL) + halo_specs(COL_SV) + halo_specs(COL_SC)
    in_specs += [pl.BlockSpec((tm, MIX_WIDTH), lambda i, j: (i, COL_SB)),
                 pl.BlockSpec((tm, NA_WIDTH), lambda i, j: (i, 0)),
                 pl.BlockSpec((tm, MIX_WIDTH), lambda i, j: (i, 0)),
                 _wspec(layer, (4, d, tn), lambda i, j: (0, 0, j)),
                 _wspec(layer, (NA_WIDTH, tn), lambda i, j: (0, j)),
                 _wspec(layer, (MIX_WIDTH, tn), lambda i, j: (0, j)),
                 _wspec(layer, (MIX_WIDTH, tn), lambda i, j: (0, j)),
                 _wspec(layer, (MIX_WIDTH, tn), lambda i, j: (0, j)),
                 pl.BlockSpec(band.shape, lambda i, j: (0, 0, 0)),
                 _wspec(layer, (len(POOL_WINDOWS), GROUP_DIM, GROUP_DIM), lambda i, j: (0, 0, 0)),
                 pl.BlockSpec((1, MIX_WIDTH), lambda i, j: (0, 0)),
                 pl.BlockSpec((3, MIX_WIDTH), lambda i, j: (0, 0))]
    return pl.pallas_call(
        functools.partial(_merge_kernel, tiles_per_seq=tiles_per_seq, seq_len=seq_len),
        grid=(t // tm, d // tn),
        in_specs=in_specs,
        out_specs=pl.BlockSpec((tm, tn), lambda i, j: (i, j)),
        out_shape=jax.ShapeDtypeStruct((t, d), BF16),
        scratch_shapes=[pltpu.VMEM((tm, MIX_WIDTH), BF16), pltpu.VMEM((tm, MIX_WIDTH), BF16),
                        pltpu.VMEM((tm + 2 * HALO, MIX_WIDTH), F32)],
        compiler_params=_params(2),
        name="gated_merge",
    )(h, *([proj] * 10), attn, four, wg, wpa, wpb, wpc, wpd, band, wpool, pscale, convw)


def _resid_proj_kernel(a_ref, w_ref, x_ref, g_ref, o_ref):
    o_ref[...] = x_ref[...] + g_ref[...] * jnp.dot(a_ref[...], w_ref[...], preferred_element_type=F32)


def _resid_proj(a, w, layer, x2, modt, chunk, modrow, tm, tn):
    t, k = a.shape
    row_fn = _row_fn(modrow, tm)
    d = w.shape[2]
    return pl.pallas_call(
        _resid_proj_kernel,
        grid=(d // tn, t // tm),
        in_specs=[pl.BlockSpec((tm, k), lambda j, i: (i, 0)),
                  _wspec(layer, (k, tn), lambda j, i: (0, j)),
                  pl.BlockSpec((tm, tn), lambda j, i: (i, j)),
                  pl.BlockSpec((None, 1, tn), lambda j, i: (row_fn(i) * N_MOD + chunk, 0, j))],
        out_specs=pl.BlockSpec((tm, tn), lambda j, i: (i, j)),
        out_shape=jax.ShapeDtypeStruct((t, d), F32),
        compiler_params=_params(2),
        name="residual_projection",
    )(a, w, x2, modt)


def _resid_norm_kernel(a_ref, w_ref, x_ref, g_ref, sh_ref, sc_ref, nw_ref, o_ref, ho_ref, xa_scr, xb_scr):
    i = pl.program_id(0)

    @pl.when(i == 0)
    def _():
        xb_scr[...] = jnp.zeros(xb_scr.shape, xb_scr.dtype)

    def step(cur_scr, prev_scr):
        _norm_mod_rows(ho_ref, prev_scr, nw_ref[...] * (1.0 + sc_ref[...]), sh_ref[...])
        x1 = x_ref[...] + g_ref[...] * jnp.dot(a_ref[...], w_ref[...], preferred_element_type=F32)
        o_ref[...] = x1
        cur_scr[...] = x1

    @pl.when(i % 2 == 0)
    def _():
        step(xa_scr, xb_scr)

    @pl.when(i % 2 == 1)
    def _():
        step(xb_scr, xa_scr)


def _resid_norm(a, w, layer, x2, modt, modrow, nw, tm):
    t, k = a.shape
    d = w.shape[2]
    nt = t // tm
    row_fn = _row_fn(modrow, tm)
    cur = lambda i: jnp.minimum(i, nt - 1)
    prev = lambda i: jnp.maximum(i - 1, 0)
    mod_spec = lambda chunk, tile: pl.BlockSpec((None, 1, d), lambda i: (row_fn(tile(i)) * N_MOD + chunk, 0, 0))
    return pl.pallas_call(
        _resid_norm_kernel,
        grid=(nt + 1,),
        in_specs=[pl.BlockSpec((tm, k), lambda i: (cur(i), 0)),
                  _wspec(layer, (k, d), lambda i: (0, 0), pipeline_mode=pl.Buffered(1)),
                  pl.BlockSpec((tm, d), lambda i: (cur(i), 0)),
                  mod_spec(2, cur), mod_spec(3, prev), mod_spec(4, prev),
                  pl.BlockSpec((1, d), lambda i: (0, 0))],
        out_specs=[pl.BlockSpec((tm, d), lambda i: (cur(i), 0)),
                   pl.BlockSpec((tm, d), lambda i: (prev(i), 0))],
        out_shape=[jax.ShapeDtypeStruct((t, d), F32), jax.ShapeDtypeStruct((t, d), BF16)],
        scratch_shapes=[pltpu.VMEM((tm, d), F32), pltpu.VMEM((tm, d), F32)],
        compiler_params=_params(1),
        name="residual_projection_norm",
    )(a, w, x2, modt, modt, modt, nw)


def _ffn_up_kernel(h_ref, wg_ref, wu_ref, o_ref):
    h = h_ref[...]
    gt = jnp.dot(h, wg_ref[...], preferred_element_type=F32)
    up = jnp.dot(h, wu_ref[...], preferred_element_type=F32)
    o_ref[...] = (gt * jax.nn.sigmoid(gt) * up).astype(o_ref.dtype)


def _ffn_up(h, w_g, w_u, layer, tm, tn):
    t, d = h.shape
    n = w_g.shape[2]
    return pl.pallas_call(
        _ffn_up_kernel,
        grid=(t // tm, n // tn),
        in_specs=[pl.BlockSpec((tm, d), lambda i, j: (i, 0)),
                  _wspec(layer, (d, tn), lambda i, j: (0, j)),
                  _wspec(layer, (d, tn), lambda i, j: (0, j))],
        out_specs=pl.BlockSpec((tm, tn), lambda i, j: (i, j)),
        out_shape=jax.ShapeDtypeStruct((t, n), BF16),
        compiler_params=_params(2),
        name="ffn_up",
    )(h, w_g, w_u)


def _stream_layer(x2, modt, modrow, seq_len, tm, lw, mixers):
    l = lw["layer"]
    proj, h = _in_proj(x2, modt, modrow, lw["norm1"], lw["w_in"], l, lw["qw"], lw["kw"], min(tm, 512))
    attn, four = mixers(proj)
    y = _merge(h, proj, attn, four, lw["w_gate"], lw["w_pa"], lw["w_pb"], lw["w_pc"],
               lw["w_pd"], lw["w_pool"], l, lw["pool_scale"], lw["conv_w"], seq_len, min(tm, 512), 512)
    x2, h2 = _resid_norm(y, lw["w_o"], l, x2, modt, modrow, lw["norm2"], min(tm, 512))
    a = _ffn_up(h2, lw["w_ffn_gate"], lw["w_ffn_up"], l, tm, 512)
    return _resid_proj(a, lw["w_ffn_down"], l, x2, modt, 5, modrow, min(tm, 512), min(x2.shape[1], 1024))


def kernel(x, c, ctx, c_ctx, w_mod, b_mod, norm1_w, w_in, q_norm_w, k_norm_w, rpb, w_pool, pool_scale, conv_w,
           w_gate, w_pa, w_pb, w_pc, w_pd, w_o, norm2_w, w_ffn_gate, w_ffn_up, w_ffn_down):
    batch, seq, d = x.shape
    ctx_len = ctx.shape[1]
    depth = w_mod.shape[0]
    rows = seq // GRID_W
    assert seq % (NA_QROWS * GRID_W) == 0 and rows >= NA_KROWS and batch + 1 <= MOD_ROWS

    cvec = jnp.zeros((MOD_ROWS, d), F32).at[:batch].set(c).at[batch].set(c_ctx)
    mod = _modulation(cvec, w_mod, b_mod)

    tm_lat = 1024
    lat_row = (0, seq)
    ctx_row = (batch, batch * ctx_len)

    x2 = x.reshape(batch * seq, d)
    xc = ctx.reshape(batch * ctx_len, d)
    scale = HEAD_DIM ** -0.5 * LOG2_E
    stacked = {name: w.astype(BF16) for name, w in (
        ("w_in", w_in), ("w_gate", w_gate), ("w_pa", w_pa), ("w_pb", w_pb), ("w_pc", w_pc), ("w_pd", w_pd),
        ("w_o", w_o), ("w_pool", w_pool), ("w_ffn_gate", w_ffn_gate), ("w_ffn_up", w_ffn_up),
        ("w_ffn_down", w_ffn_down))}
    for l in range(depth):
        lw = dict(stacked)
        lw.update({
            "layer": l,
            "norm1": norm1_w[l].reshape(1, d), "norm2": norm2_w[l].reshape(1, d),
            "qw": (q_norm_w[l] * scale).reshape(1, HEAD_DIM), "kw": k_norm_w[l].reshape(1, HEAD_DIM),
            "pool_scale": pool_scale[l].reshape(1, MIX_WIDTH), "conv_w": conv_w[l],
        })
        modt = mod[l].reshape(MOD_ROWS * N_MOD, 1, d)
        bias = _na_bias_tables(rpb[l] * LOG2_E, rows)

        if l == depth - 1:
            proj_c, _ = _in_proj(xc, modt, ctx_row, lw["norm1"], lw["w_in"], l, lw["qw"], lw["kw"], ctx_len)
        else:
            holder = {}

            def ctx_mixers(proj):
                holder["proj"] = proj
                return _context_attention(proj, batch), _context_fourier(proj, batch)

            xc = _stream_layer(xc, modt, ctx_row, ctx_len, ctx_len, lw, ctx_mixers)
            proj_c = holder["proj"]

        def lat_mixers(proj):
            return (_neighbourhood_attention(proj, proj_c, bias, batch, seq),
                    _fourier_mix(proj, batch, seq))

        x2 = _stream_layer(x2, modt, lat_row, seq, tm_lat, lw, lat_mixers)
    return x2.reshape(batch, seq, d)
```

```python
import functools
import math

import numpy as np
import jax
import jax.numpy as jnp
from jax import lax
from jax.experimental import pallas as pl
from jax.experimental.pallas import tpu as pltpu

F32 = jnp.float32
BF16 = jnp.bfloat16

GRID_W = 64
NA_HEADS = 8
HEAD_DIM = 128
NA_WIDTH = NA_HEADS * HEAD_DIM
NA_KH = 8
NA_KW = 16
NA_QROWS = 4
NA_KROWS = 12
GROUP_DIM = 128
FT_GROUPS = 4
POOL_WINDOWS = (2, 4, 8, 16)
MIX_WIDTH = 512
N_MOD = 6
RMS_EPS = 1e-6
F32_ROWS = 8
BF16_ROWS = 16
HALO = 64
NORM_ROWS = 2 * BF16_ROWS
MOD_ROWS = 8
NEG_BIG = -1e30
LOG2_E = math.log2(math.e)

COL_FT, COL_POOL, COL_SV, COL_SB, COL_SC = 6, 7, 8, 9, 10

VMEM_LIMIT = 56 * 1024 * 1024


def _params(n_axes):
    return pltpu.CompilerParams(dimension_semantics=("arbitrary",) * n_axes,
                                vmem_limit_bytes=VMEM_LIMIT)


def _mod_kernel(c_ref, w_ref, b_ref, o_ref):
    cv = c_ref[...]
    s = cv * jax.nn.sigmoid(cv)
    o_ref[...] = jnp.dot(s.astype(BF16), w_ref[...].astype(BF16),
                         preferred_element_type=F32) + b_ref[...]


def _modulation(cvec, w_mod, b_mod):
    depth, d, n = w_mod.shape
    tn = 1024
    return pl.pallas_call(
        _mod_kernel,
        grid=(depth, n // tn),
        in_specs=[
            pl.BlockSpec((MOD_ROWS, d), lambda l, j: (0, 0)),
            pl.BlockSpec((None, d, tn), lambda l, j: (l, 0, j)),
            pl.BlockSpec((None, 1, tn), lambda l, j: (l, 0, j)),
        ],
        out_specs=pl.BlockSpec((None, MOD_ROWS, tn), lambda l, j: (l, 0, j)),
        out_shape=jax.ShapeDtypeStruct((depth, MOD_ROWS, n), F32),
        compiler_params=_params(2),
        name="modulation",
    )(cvec, w_mod, b_mod.reshape(depth, 1, n))


def _row_fn(modrow, tm):
    base, tokens_per_row = modrow
    return lambda i: base + (i * tm) // tokens_per_row


def _wspec(layer, block, index_map, **kwargs):
    return pl.BlockSpec((None,) + tuple(block), lambda *g: (layer,) + tuple(index_map(*g)), **kwargs)


def _norm_mod_rows(h_ref, x_ref, w, sh, chunk=NORM_ROWS):
    for r in range(0, x_ref.shape[0], chunk):
        xf = x_ref[r:r + chunk, :]
        ms = jnp.mean(xf * xf, axis=-1, keepdims=True)
        h_ref[r:r + chunk, :] = (xf * lax.rsqrt(ms + RMS_EPS) * w + sh).astype(BF16)


def _in_proj_kernel(x_ref, sh_ref, sc_ref, nw_ref, w_ref, qw_ref, kw_ref, o_ref, ho_ref, h_scr):
    i = pl.program_id(0)

    @pl.when(i == 0)
    def _():
        h_scr[1] = jnp.zeros(h_scr.shape[1:], h_scr.dtype)

    h = h_scr[(i + 1) % 2]
    ho_ref[...] = h
    for c0 in range(0, w_ref.shape[1], MIX_WIDTH):
        acc = jnp.dot(h, w_ref[:, c0:c0 + MIX_WIDTH], preferred_element_type=F32)
        if c0 < 2 * NA_WIDTH:
            w = qw_ref[...] if c0 < NA_WIDTH else kw_ref[...]
            for hh in range(0, MIX_WIDTH, HEAD_DIM):
                a = acc[:, hh:hh + HEAD_DIM]
                ms = jnp.mean(a * a, axis=-1, keepdims=True)
                o_ref[:, c0 + hh:c0 + hh + HEAD_DIM] = (a * lax.rsqrt(ms + RMS_EPS) * w).astype(o_ref.dtype)
        else:
            o_ref[:, c0:c0 + MIX_WIDTH] = acc.astype(o_ref.dtype)
    _norm_mod_rows(h_scr.at[i % 2], x_ref, nw_ref[...] * (1.0 + sc_ref[...]), sh_ref[...])


def _in_proj(x2, modt, modrow, nw, w_in, layer, qw, kw, tm):
    t, d = x2.shape
    n = w_in.shape[2]
    nt = t // tm
    row_fn = _row_fn(modrow, tm)
    cur = lambda i: jnp.minimum(i, nt - 1)
    mod_spec = lambda chunk: pl.BlockSpec((None, 1, d), lambda i: (row_fn(cur(i)) * N_MOD + chunk, 0, 0))
    return pl.pallas_call(
        _in_proj_kernel,
        grid=(nt + 1,),
        in_specs=[
            pl.BlockSpec((tm, d), lambda i: (cur(i), 0)),
            mod_spec(0),
            mod_spec(1),
            pl.BlockSpec((1, d), lambda i: (0, 0)),
            _wspec(layer, (d, n), lambda i: (0, 0), pipeline_mode=pl.Buffered(1)),
            pl.BlockSpec((1, HEAD_DIM), lambda i: (0, 0)),
            pl.BlockSpec((1, HEAD_DIM), lambda i: (0, 0)),
        ],
        out_specs=[pl.BlockSpec((tm, n), lambda i: (jnp.maximum(i - 1, 0), 0)),
                   pl.BlockSpec((tm, d), lambda i: (jnp.maximum(i - 1, 0), 0))],
        out_shape=[jax.ShapeDtypeStruct((t, n), BF16), jax.ShapeDtypeStruct((t, d), BF16)],
        scratch_shapes=[pltpu.VMEM((2, tm, d), BF16)],
        compiler_params=_params(1),
        name="in_proj",
    )(x2, modt, modt, nw, w_in, qw, kw)


def _na_bias_tables(rpb, rows):
    cols = np.arange(GRID_W)
    cs = np.clip(cols - NA_KW // 2, 0, GRID_W - NA_KW)
    valid_c = (cols[None, :] >= cs[:, None]) & (cols[None, :] < cs[:, None] + NA_KW)
    pad = GRID_W - NA_KW
    rp = jnp.pad(rpb.astype(F32), ((0, 0), (0, 0), (pad, pad)))
    col_t = jnp.stack([rp[:, :, GRID_W - 1 - qc:2 * GRID_W - 1 - qc] for qc in range(GRID_W)], axis=2)
    tables = []
    for r0, kr0 in ((0, 0), (2 * NA_QROWS, 2 * NA_QROWS - NA_KH // 2), (rows - NA_QROWS, rows - NA_KROWS)):
        r = r0 + np.arange(NA_QROWS)
        rs = np.clip(r - NA_KH // 2, 0, rows - NA_KH)
        key_row = kr0 + np.arange(NA_KROWS)
        valid_r = (key_row[None, :] >= rs[:, None]) & (key_row[None, :] < rs[:, None] + NA_KH)
        row_off = np.clip(key_row[None, :] - r[:, None] + (NA_KH - 1), 0, 2 * NA_KH - 2)
        tab = jnp.concatenate([jnp.concatenate([col_t[:, int(row_off[qr, kr])] for kr in range(NA_KROWS)], axis=2)
                               for qr in range(NA_QROWS)], axis=1)
        valid = (valid_r[:, None, :, None] & valid_c[None, :, None, :]).reshape(tab.shape[1:])
        tables.append(jnp.where(valid[None], tab, NEG_BIG))
    return jnp.stack(tables)


def _qk(q, k):
    return lax.dot_general(q, k, (((1,), (1,)), ((), ())), preferred_element_type=F32)


def _lane_tiles(a, width=128):
    return [a[:, k:k + width] for k in range(0, a.shape[1], width)]


def _na_kernel(q_ref, k0_ref, k1_ref, k2_ref, v0_ref, v1_ref, v2_ref, kc_ref, vc_ref, bias_ref, o_ref):
    blk = k0_ref.shape[0]
    values = (vc_ref, v0_ref, v1_ref, v2_ref)

    def scores_of(h):
        hs = slice(h * HEAD_DIM, (h + 1) * HEAD_DIM)
        q = q_ref[:, hs]
        s_loc = [_qk(q, kr[:, hs]) + bias_ref[h, :, m * blk:(m + 1) * blk]
                 for m, kr in enumerate((k0_ref, k1_ref, k2_ref))]
        return [_qk(q, kc_ref[:, hs])] + s_loc

    nxt = scores_of(0)
    for h in range(NA_HEADS):
        hs = slice(h * HEAD_DIM, (h + 1) * HEAD_DIM)
        scores = nxt
        if h + 1 < NA_HEADS:
            nxt = scores_of(h + 1)
        mx = jnp.max(functools.reduce(jnp.maximum, [t for s in scores for t in _lane_tiles(s)]),
                     axis=-1, keepdims=True)
        probs = [jnp.exp2(s - mx) for s in scores]
        den = jnp.sum(functools.reduce(jnp.add, [t for p in probs for t in _lane_tiles(p)]),
                      axis=-1, keepdims=True)
        o = None
        for p, vr in zip(probs, values):
            pv = jnp.dot(p.astype(BF16), vr[:, hs], preferred_element_type=F32)
            o = pv if o is None else o + pv
        o_ref[:, hs] = (o / den).astype(o_ref.dtype)


def _neighbourhood_attention(proj, proj_ctx, bias, batch, seq):
    blk = NA_QROWS * GRID_W
    nrb = seq // blk
    n_kblk = NA_KROWS // NA_QROWS
    ctx_len = proj_ctx.shape[0] // batch

    def kblock(rb):
        return jnp.clip(rb - 1, 0, nrb - n_kblk)

    def kv_spec(m, col):
        return pl.BlockSpec((blk, NA_WIDTH), lambda b, rb: (b * nrb + kblock(rb) + m, col))

    def variant(rb):
        return jnp.where(rb == 0, 0, jnp.where(rb == nrb - 1, 2, 1))

    in_specs = [pl.BlockSpec((blk, NA_WIDTH), lambda b, rb: (b * nrb + rb, 0))]
    in_specs += [kv_spec(m, 1) for m in range(n_kblk)] + [kv_spec(m, 2) for m in range(n_kblk)]
    in_specs += [pl.BlockSpec((ctx_len, NA_WIDTH), lambda b, rb: (b, 1)),
                 pl.BlockSpec((ctx_len, NA_WIDTH), lambda b, rb: (b, 2)),
                 pl.BlockSpec((None, NA_HEADS, blk, NA_KROWS * GRID_W), lambda b, rb: (variant(rb), 0, 0, 0))]
    return pl.pallas_call(
        _na_kernel,
        grid=(batch, nrb),
        in_specs=in_specs,
        out_specs=pl.BlockSpec((blk, NA_WIDTH), lambda b, rb: (b * nrb + rb, 0)),
        out_shape=jax.ShapeDtypeStruct((batch * seq, NA_WIDTH), BF16),
        compiler_params=_params(2),
        name="neighbourhood_attention",
    )(*([proj] * (1 + 2 * n_kblk)), proj_ctx, proj_ctx, bias)


def _ctx_attn_kernel(q_ref, k_ref, v_ref, o_ref):
    for h in range(NA_HEADS):
        hs = slice(h * HEAD_DIM, (h + 1) * HEAD_DIM)
        s = _qk(q_ref[:, hs], k_ref[:, hs])
        p = jnp.exp2(s - jnp.max(s, axis=-1, keepdims=True))
        den = jnp.sum(p, axis=-1, keepdims=True)
        o = jnp.dot(p.astype(BF16), v_ref[:, hs], preferred_element_type=F32)
        o_ref[:, hs] = (o / den).astype(o_ref.dtype)


def _context_attention(proj_ctx, batch):
    ctx_len = proj_ctx.shape[0] // batch
    spec = lambda col: pl.BlockSpec((ctx_len, NA_WIDTH), lambda b: (b, col))
    return pl.pallas_call(
        _ctx_attn_kernel,
        grid=(batch,),
        in_specs=[spec(0), spec(1), spec(2)],
        out_specs=spec(0),
        out_shape=jax.ShapeDtypeStruct((proj_ctx.shape[0], NA_WIDTH), BF16),
        compiler_params=_params(1),
        name="context_attention",
    )(proj_ctx, proj_ctx, proj_ctx)


def _dft_cos_sin(n_out, n_in, period):
    ij = (np.arange(n_out, dtype=np.int64)[:, None] * np.arange(n_in, dtype=np.int64)[None, :]) % period
    ang = 2.0 * np.pi * ij.astype(np.float64) / period
    return np.cos(ang), np.sin(ang)


def _ft_stage_a_kernel(u_ref, mc_ref, m_ref, o_ref, xr0, xi0, xr1, xi1, yr0, yi0, yr1, yi1):
    c = GROUP_DIM
    n_a, bb = u_ref.shape[0], u_ref.shape[1]
    hb = bb // 2
    z = jnp.dot(u_ref[...].reshape(n_a * bb, c), mc_ref[...], preferred_element_type=F32)
    for a in range(n_a):
        lo, hi = z[a * bb:a * bb + hb], z[a * bb + hb:(a + 1) * bb]
        xr0[a * hb:(a + 1) * hb, :] = lo[:, :c]
        xi0[a * hb:(a + 1) * hb, :] = lo[:, c:]
        xr1[a * hb:(a + 1) * hb, :] = hi[:, :c]
        xi1[a * hb:(a + 1) * hb, :] = hi[:, c:]
    for k in range(bb):
        xr_s, xi_s, yr_s, yi_s = (xr0, xi0, yr0, yi0) if k < hb else (xr1, xi1, yr1, yi1)
        rows = pl.ds(k % hb, n_a, stride=hb)
        xr, xi = xr_s[rows, :], xi_s[rows, :]
        rhs = jnp.concatenate([jnp.concatenate([xr, xi], axis=1), jnp.concatenate([-xi, xr], axis=1)], axis=0)
        y = jnp.dot(m_ref[k], rhs.astype(BF16), preferred_element_type=F32)
        yr_s[rows, :] = y[:, :c]
        yi_s[rows, :] = y[:, c:]
    for a in range(n_a):
        sl = slice(a * hb, (a + 1) * hb)
        re = jnp.concatenate([yr0[sl, :], yr1[sl, :]], axis=0)
        im = jnp.concatenate([yi0[sl, :], yi1[sl, :]], axis=0)
        o_ref[a] = jnp.concatenate([re, im], axis=1).astype(o_ref.dtype)


def _ft_stage_b_kernel(x_ref, m_ref, o_ref, o_s):
    c = GROUP_DIM
    aa, n_b = x_ref.shape[0], x_ref.shape[1]
    m = m_ref[...]
    for k in range(aa):
        y = x_ref[k]
        rhs = jnp.concatenate([y[:, :c], y[:, c:]], axis=0)
        o_s[pl.ds(k, n_b, stride=aa), :] = jnp.dot(m, rhs, preferred_element_type=F32)
    for b in range(n_b):
        o_ref[b] = o_s[b * aa:(b + 1) * aa, :].astype(o_ref.dtype)


def _fourier_mix(proj, batch, seq):
    t = batch * seq
    n_b = GROUP_DIM
    n_a = seq // n_b
    g, c = FT_GROUPS, GROUP_DIM
    sub = 2 * F32_ROWS

    cc, sc = _dft_cos_sin(c, c, c)
    m_c = jnp.asarray(np.concatenate([cc, -sc], axis=1), BF16)
    ca, sa = _dft_cos_sin(n_a, seq, seq)
    ca, sa = ca.reshape(n_a, n_a, n_b), sa.reshape(n_a, n_a, n_b)
    m_a = jnp.asarray(np.concatenate([ca, -sa], axis=1).transpose(2, 0, 1), BF16)
    cb, sb = _dft_cos_sin(n_b, n_b, n_b)
    m_b = jnp.asarray(np.concatenate([cb, sb], axis=1) / math.sqrt(seq * c), BF16)

    ft_col = COL_FT * MIX_WIDTH // c
    y = pl.pallas_call(
        _ft_stage_a_kernel,
        grid=(n_b // sub, batch, g),
        in_specs=[pl.BlockSpec((None, n_a, sub, c), lambda j, b, k: (b, 0, j, ft_col + k)),
                  pl.BlockSpec((c, 2 * c), lambda j, b, k: (0, 0)),
                  pl.BlockSpec((sub, n_a, 2 * n_a), lambda j, b, k: (j, 0, 0))],
        out_specs=pl.BlockSpec((None, n_a, sub, 2 * c), lambda j, b, k: (b, 0, j, k)),
        out_shape=jax.ShapeDtypeStruct((batch, n_a, n_b, 2 * g * c), BF16),
        scratch_shapes=[pltpu.VMEM((n_a * sub // 2, c), F32)] * 8,
        compiler_params=_params(3),
        name="fourier_stage_a",
    )(proj.reshape(batch, n_a, n_b, proj.shape[1]), m_c, m_a)

    o = pl.pallas_call(
        _ft_stage_b_kernel,
        grid=(batch, g, n_a // sub),
        in_specs=[pl.BlockSpec((None, sub, n_b, 2 * c), lambda b, k, j: (b, j, 0, k)),
                  pl.BlockSpec((n_b, 2 * n_b), lambda b, k, j: (0, 0))],
        out_specs=pl.BlockSpec((None, n_b, sub, c), lambda b, k, j: (b, 0, j, k)),
        out_shape=jax.ShapeDtypeStruct((batch, n_b, n_a, g * c), BF16),
        scratch_shapes=[pltpu.VMEM((n_b * sub, c), F32)],
        compiler_params=_params(3),
        name="fourier_stage_b",
    )(y, m_b)
    return o.reshape(t, g * c)


def _ctx_fourier_kernel(u_ref, mc_ref, ml_ref, o_ref):
    mc = mc_ref[...]
    ml = ml_ref[...]
    for g in range(FT_GROUPS):
        z = jnp.dot(u_ref[:, g * GROUP_DIM:(g + 1) * GROUP_DIM], mc, preferred_element_type=F32)
        stack = jnp.concatenate([z[:, :GROUP_DIM], z[:, GROUP_DIM:]], axis=0).astype(BF16)
        o_ref[:, g * GROUP_DIM:(g + 1) * GROUP_DIM] = jnp.dot(ml, stack, preferred_element_type=F32).astype(o_ref.dtype)


def _context_fourier(proj_ctx, batch):
    ctx_len = proj_ctx.shape[0] // batch
    c = GROUP_DIM
    cc, sc = _dft_cos_sin(c, c, c)
    m_c = jnp.asarray(np.concatenate([cc, -sc], axis=1), BF16)
    cl, sl = _dft_cos_sin(ctx_len, ctx_len, ctx_len)
    m_l = jnp.asarray(np.concatenate([cl, sl], axis=1) / math.sqrt(ctx_len * c), BF16)
    return pl.pallas_call(
        _ctx_fourier_kernel,
        grid=(batch,),
        in_specs=[pl.BlockSpec((ctx_len, MIX_WIDTH), lambda b: (b, COL_FT)),
                  pl.BlockSpec((c, 2 * c), lambda b: (0, 0)),
                  pl.BlockSpec((ctx_len, 2 * ctx_len), lambda b: (0, 0))],
        out_specs=pl.BlockSpec((ctx_len, MIX_WIDTH), lambda b: (b, 0)),
        out_shape=jax.ShapeDtypeStruct((proj_ctx.shape[0], MIX_WIDTH), BF16),
        compiler_params=_params(1),
        name="context_fourier",
    )(proj_ctx, m_c, m_l)


def _merge_kernel(h_ref, pool_ref, pool_p_ref, pool_n_ref, sv_ref, sv_p_ref, sv_n_ref, scv_ref, scv_p_ref, scv_n_ref,
                  sb_ref, attn_ref, four_ref, wg_ref, wpa_ref, wpb_ref, wpc_ref, wpd_ref,
                  band_ref, wpool_ref, pscale_ref, convw_ref, o_ref,
                  brc_scr, brd_scr, ext_scr, *, tiles_per_seq, seq_len):
    i = pl.program_id(0)
    j = pl.program_id(1)
    tm = h_ref.shape[0]

    @pl.when(j == 0)
    def _():
        first = (i % tiles_per_seq) == 0
        last = (i % tiles_per_seq) == tiles_per_seq - 1
        pos = (i % tiles_per_seq) * tm + lax.broadcasted_iota(jnp.int32, (tm, 1), 0)

        zero = jnp.zeros(pool_p_ref.shape, pool_p_ref.dtype)
        ext = jnp.concatenate([jnp.where(first, zero, pool_p_ref[...]), pool_ref[...],
                               jnp.where(last, zero, pool_n_ref[...])], axis=0)
        for g, w in enumerate(POOL_WINDOWS):
            cs = slice(g * GROUP_DIM, (g + 1) * GROUP_DIM)
            win = jnp.dot(band_ref[g], ext[:, cs], preferred_element_type=F32)
            cnt = (jnp.minimum(pos + w // 2, seq_len) - jnp.maximum(pos - w // 2, 0)).astype(F32)
            p = win * (1.0 / cnt) - pool_ref[:, cs].astype(F32)
            yg = jnp.dot(p.astype(BF16), wpool_ref[g], preferred_element_type=F32) * pscale_ref[:, cs]
            brc_scr[:, cs] = yg.astype(BF16)

        ext_scr[0:HALO, :] = jnp.where(first, 0.0, sv_p_ref[...].astype(F32) * scv_p_ref[...].astype(F32))
        ext_scr[HALO:HALO + tm, :] = sv_ref[...].astype(F32) * scv_ref[...].astype(F32)
        ext_scr[HALO + tm:HALO + tm + HALO, :] = jnp.where(last, 0.0,
                                                           sv_n_ref[...].astype(F32) * scv_n_ref[...].astype(F32))
        y = (ext_scr[HALO - 1:HALO - 1 + tm, :] * convw_ref[0:1, :]
             + ext_scr[HALO:HALO + tm, :] * convw_ref[1:2, :]
             + ext_scr[HALO + 1:HALO + 1 + tm, :] * convw_ref[2:3, :])
        brd_scr[...] = (sb_ref[...].astype(F32) * y).astype(BF16)

    h = h_ref[...]
    branches = ((attn_ref, wpa_ref), (four_ref, wpb_ref), (brc_scr, wpc_ref), (brd_scr, wpd_ref))
    y = None
    for k, (br_ref, wp_ref) in enumerate(branches):
        gate = jax.nn.sigmoid(jnp.dot(h, wg_ref[k], preferred_element_type=F32))
        term = gate * jnp.dot(br_ref[...], wp_ref[...], preferred_element_type=F32)
        y = term if y is None else y + term
    o_ref[...] = y.astype(o_ref.dtype)


def _merge(h, proj, attn, four, wg, wpa, wpb, wpc, wpd, wpool, layer, pscale, convw, seq_len, tm, tn):
    t, d = h.shape
    tiles_per_seq = seq_len // tm
    n_halo = t // HALO

    rel = np.arange(tm + 2 * HALO)[None, :] - HALO - np.arange(tm)[:, None]
    band = jnp.asarray(np.stack([(rel >= -(w // 2)) & (rel < w // 2) for w in POOL_WINDOWS]), BF16)

    def halo_specs(col):
        return [pl.BlockSpec((tm, MIX_WIDTH), lambda i, j: (i, col)),
                pl.BlockSpec((HALO, MIX_WIDTH), lambda i, j: (jnp.maximum(i * (tm // HALO) - 1, 0), col)),
                pl.BlockSpec((HALO, MIX_WIDTH), lambda i, j: (jnp.minimum((i + 1) * (tm // HALO), n_halo - 1), col))]

    in_specs = [pl.BlockSpec((tm, d), lambda i, j: (i, 0))]
    in_specs += halo_specs(COL_POOL) + halo_specs(COL_SV) + halo_specs(COL_SC)
    in_specs += [pl.BlockSpec((tm, MIX_WIDTH), lambda i, j: (i, COL_SB)),
                 pl.BlockSpec((tm, NA_WIDTH), lambda i, j: (i, 0)),
                 pl.BlockSpec((tm, MIX_WIDTH), lambda i, j: (i, 0)),
                 _wspec(layer, (4, d, tn), lambda i, j: (0, 0, j)),
                 _wspec(layer, (NA_WIDTH, tn), lambda i, j: (0, j)),
                 _wspec(layer, (MIX_WIDTH, tn), lambda i, j: (0, j)),
                 _wspec(layer, (MIX_WIDTH, tn), lambda i, j: (0, j)),
                 _wspec(layer, (MIX_WIDTH, tn), lambda i, j: (0, j)),
                 pl.BlockSpec(band.shape, lambda i, j: (0, 0, 0)),
                 _wspec(layer, (len(POOL_WINDOWS), GROUP_DIM, GROUP_DIM), lambda i, j: (0, 0, 0)),
                 pl.BlockSpec((1, MIX_WIDTH), lambda i, j: (0, 0)),
                 pl.BlockSpec((3, MIX_WIDTH), lambda i, j: (0, 0))]
    return pl.pallas_call(
        functools.partial(_merge_kernel, tiles_per_seq=tiles_per_seq, seq_len=seq_len),
        grid=(t // tm, d // tn),
        in_specs=in_specs,
        out_specs=pl.BlockSpec((tm, tn), lambda i, j: (i, j)),
        out_shape=jax.ShapeDtypeStruct((t, d), BF16),
        scratch_shapes=[pltpu.VMEM((tm, MIX_WIDTH), BF16), pltpu.VMEM((tm, MIX_WIDTH), BF16),
                        pltpu.VMEM((tm + 2 * HALO, MIX_WIDTH), F32)],
        compiler_params=_params(2),
        name="gated_merge",
    )(h, *([proj] * 10), attn, four, wg, wpa, wpb, wpc, wpd, band, wpool, pscale, convw)


def _resid_proj_kernel(a_ref, w_ref, x_ref, g_ref, o_ref):
    o_ref[...] = x_ref[...] + g_ref[...] * jnp.dot(a_ref[...], w_ref[...], preferred_element_type=F32)


def _resid_proj(a, w, layer, x2, modt, chunk, modrow, tm, tn):
    t, k = a.shape
    row_fn = _row_fn(modrow, tm)
    d = w.shape[2]
    return pl.pallas_call(
        _resid_proj_kernel,
        grid=(d // tn, t // tm),
        in_specs=[pl.BlockSpec((tm, k), lambda j, i: (i, 0)),
                  _wspec(layer, (k, tn), lambda j, i: (0, j)),
                  pl.BlockSpec((tm, tn), lambda j, i: (i, j)),
                  pl.BlockSpec((None, 1, tn), lambda j, i: (row_fn(i) * N_MOD + chunk, 0, j))],
        out_specs=pl.BlockSpec((tm, tn), lambda j, i: (i, j)),
        out_shape=jax.ShapeDtypeStruct((t, d), F32),
        compiler_params=_params(2),
        name="residual_projection",
    )(a, w, x2, modt)


def _resid_norm_kernel(a_ref, w_ref, x_ref, g_ref, sh_ref, sc_ref, nw_ref, o_ref, ho_ref, xa_scr, xb_scr):
    i = pl.program_id(0)

    @pl.when(i == 0)
    def _():
        xb_scr[...] = jnp.zeros(xb_scr.shape, xb_scr.dtype)

    def step(cur_scr, prev_scr):
        _norm_mod_rows(ho_ref, prev_scr, nw_ref[...] * (1.0 + sc_ref[...]), sh_ref[...])
        x1 = x_ref[...] + g_ref[...] * jnp.dot(a_ref[...], w_ref[...], preferred_element_type=F32)
        o_ref[...] = x1
        cur_scr[...] = x1

    @pl.when(i % 2 == 0)
    def _():
        step(xa_scr, xb_scr)

    @pl.when(i % 2 == 1)
    def _():
        step(xb_scr, xa_scr)


def _resid_norm(a, w, layer, x2, modt, modrow, nw, tm):
    t, k = a.shape
    d = w.shape[2]
    nt = t // tm
    row_fn = _row_fn(modrow, tm)
    cur = lambda i: jnp.minimum(i, nt - 1)
    prev = lambda i: jnp.maximum(i - 1, 0)
    mod_spec = lambda chunk, tile: pl.BlockSpec((None, 1, d), lambda i: (row_fn(tile(i)) * N_MOD + chunk, 0, 0))
    return pl.pallas_call(
        _resid_norm_kernel,
        grid=(nt + 1,),
        in_specs=[pl.BlockSpec((tm, k), lambda i: (cur(i), 0)),
                  _wspec(layer, (k, d), lambda i: (0, 0), pipeline_mode=pl.Buffered(1)),
                  pl.BlockSpec((tm, d), lambda i: (cur(i), 0)),
                  mod_spec(2, cur), mod_spec(3, prev), mod_spec(4, prev),
                  pl.BlockSpec((1, d), lambda i: (0, 0))],
        out_specs=[pl.BlockSpec((tm, d), lambda i: (cur(i), 0)),
                   pl.BlockSpec((tm, d), lambda i: (prev(i), 0))],
        out_shape=[jax.ShapeDtypeStruct((t, d), F32), jax.ShapeDtypeStruct((t, d), BF16)],
        scratch_shapes=[pltpu.VMEM((tm, d), F32), pltpu.VMEM((tm, d), F32)],
        compiler_params=_params(1),
        name="residual_projection_norm",
    )(a, w, x2, modt, modt, modt, nw)


def _ffn_up_kernel(h_ref, wg_ref, wu_ref, o_ref):
    h = h_ref[...]
    gt = jnp.dot(h, wg_ref[...], preferred_element_type=F32)
    up = jnp.dot(h, wu_ref[...], preferred_element_type=F32)
    o_ref[...] = (gt * jax.nn.sigmoid(gt) * up).astype(o_ref.dtype)


def _ffn_up(h, w_g, w_u, layer, tm, tn):
    t, d = h.shape
    n = w_g.shape[2]
    return pl.pallas_call(
        _ffn_up_kernel,
        grid=(t // tm, n // tn),
        in_specs=[pl.BlockSpec((tm, d), lambda i, j: (i, 0)),
                  _wspec(layer, (d, tn), lambda i, j: (0, j)),
                  _wspec(layer, (d, tn), lambda i, j: (0, j))],
        out_specs=pl.BlockSpec((tm, tn), lambda i, j: (i, j)),
        out_shape=jax.ShapeDtypeStruct((t, n), BF16),
        compiler_params=_params(2),
        name="ffn_up",
    )(h, w_g, w_u)


def _ffn_fused_kernel(h_ref, x_ref, g_ref, wg_ref, wu_ref, wd_ref, o_ref):
    @pl.when(pl.program_id(1) == 0)
    def _():
        o_ref[...] = x_ref[...]

    h = h_ref[...]
    gt = jnp.dot(h, wg_ref[...], preferred_element_type=F32)
    up = jnp.dot(h, wu_ref[...], preferred_element_type=F32)
    a = (gt * jax.nn.sigmoid(gt) * up).astype(BF16)
    o_ref[...] += g_ref[...] * jnp.dot(a, wd_ref[...], preferred_element_type=F32)


def _ffn_fused(h, x2, w_g, w_u, w_d, layer, modt, modrow, tm, tf):
    t, d = h.shape
    n = w_g.shape[2]
    row_fn = _row_fn(modrow, tm)
    return pl.pallas_call(
        _ffn_fused_kernel,
        grid=(t // tm, n // tf),
        in_specs=[pl.BlockSpec((tm, d), lambda i, j: (i, 0)),
                  pl.BlockSpec((tm, d), lambda i, j: (i, 0)),
                  pl.BlockSpec((None, 1, d), lambda i, j: (row_fn(i) * N_MOD + 5, 0, 0)),
                  _wspec(layer, (d, tf), lambda i, j: (0, j)),
                  _wspec(layer, (d, tf), lambda i, j: (0, j)),
                  _wspec(layer, (tf, d), lambda i, j: (j, 0))],
        out_specs=pl.BlockSpec((tm, d), lambda i, j: (i, 0)),
        out_shape=jax.ShapeDtypeStruct((t, d), F32),
        compiler_params=_params(2),
        name="ffn_fused",
    )(h, x2, modt, w_g, w_u, w_d)


def _stream_layer(x2, modt, modrow, seq_len, tm, lw, mixers):
    l = lw["layer"]
    proj, h = _in_proj(x2, modt, modrow, lw["norm1"], lw["w_in"], l, lw["qw"], lw["kw"], min(tm, 512))
    attn, four = mixers(proj)
    y = _merge(h, proj, attn, four, lw["w_gate"], lw["w_pa"], lw["w_pb"], lw["w_pc"],
               lw["w_pd"], lw["w_pool"], l, lw["pool_scale"], lw["conv_w"], seq_len, min(tm, 512), 512)
    x2, h2 = _resid_norm(y, lw["w_o"], l, x2, modt, modrow, lw["norm2"], min(tm, 512))
    return _ffn_fused(h2, x2, lw["w_ffn_gate"], lw["w_ffn_up"], lw["w_ffn_down"], l, modt, modrow, min(tm, 512), 512)


def kernel(x, c, ctx, c_ctx, w_mod, b_mod, norm1_w, w_in, q_norm_w, k_norm_w, rpb, w_pool, pool_scale, conv_w,
           w_gate, w_pa, w_pb, w_pc, w_pd, w_o, norm2_w, w_ffn_gate, w_ffn_up, w_ffn_down):
    batch, seq, d = x.shape
    ctx_len = ctx.shape[1]
    depth = w_mod.shape[0]
    rows = seq // GRID_W
    assert seq % (NA_QROWS * GRID_W) == 0 and rows >= NA_KROWS and batch + 1 <= MOD_ROWS

    cvec = jnp.zeros((MOD_ROWS, d), F32).at[:batch].set(c).at[batch].set(c_ctx)
    mod = _modulation(cvec, w_mod, b_mod)

    tm_lat = 1024
    lat_row = (0, seq)
    ctx_row = (batch, batch * ctx_len)

    x2 = x.reshape(batch * seq, d)
    xc = ctx.reshape(batch * ctx_len, d)
    scale = HEAD_DIM ** -0.5 * LOG2_E
    stacked = {name: w.astype(BF16) for name, w in (
        ("w_in", w_in), ("w_gate", w_gate), ("w_pa", w_pa), ("w_pb", w_pb), ("w_pc", w_pc), ("w_pd", w_pd),
        ("w_o", w_o), ("w_pool", w_pool), ("w_ffn_gate", w_ffn_gate), ("w_ffn_up", w_ffn_up),
        ("w_ffn_down", w_ffn_down))}
    for l in range(depth):
        lw = dict(stacked)
        lw.update({
            "layer": l,
            "norm1": norm1_w[l].reshape(1, d), "norm2": norm2_w[l].reshape(1, d),
            "qw": (q_norm_w[l] * scale).reshape(1, HEAD_DIM), "kw": k_norm_w[l].reshape(1, HEAD_DIM),
            "pool_scale": pool_scale[l].reshape(1, MIX_WIDTH), "conv_w": conv_w[l],
        })
        modt = mod[l].reshape(MOD_ROWS * N_MOD, 1, d)
        bias = _na_bias_tables(rpb[l] * LOG2_E, rows)

        if l == depth - 1:
            proj_c, _ = _in_proj(xc, modt, ctx_row, lw["norm1"], lw["w_in"], l, lw["qw"], lw["kw"], ctx_len)
        else:
            holder = {}

            def ctx_mixers(proj):
                holder["proj"] = proj
                return _context_attention(proj, batch), _context_fourier(proj, batch)

            xc = _stream_layer(xc, modt, ctx_row, ctx_len, ctx_len, lw, ctx_mixers)
            proj_c = holder["proj"]

        def lat_mixers(proj):
            return (_neighbourhood_attention(proj, proj_c, bias, batch, seq),
                    _fourier_mix(proj, batch, seq))

        x2 = _stream_layer(x2, modt, lat_row, seq, tm_lat, lw, lat_mixers)
    return x2.reshape(batch, seq, d)
```
